```python
import jax, jax.numpy as jnp
from jax import lax
import numpy as np

D_MODEL = 2048
BATCH = 4
SEQ = 2048
DEPTH = 1
DEC_BATCH = 128
DEC_SEQ = 8
PAST_LEN = 16384
PAGE_SIZE = 128

MIX_WIDTH = D_MODEL
POOL_WIDTH = MIX_WIDTH // 2
POOL_WINDOWS = (2, 4, 8, 16)
POOL_GROUPS = len(POOL_WINDOWS)
POOL_GROUP_DIM = POOL_WIDTH // POOL_GROUPS
POOL_CACHE = max(POOL_WINDOWS) - 1
HG_WIDTH = MIX_WIDTH - POOL_WIDTH
HG_HEAD_DIM = 128
HG_HEADS = HG_WIDTH // HG_HEAD_DIM
CHUNK = 16
D_FF = ((8 * D_MODEL // 3 + 255) // 256) * 256
IN_COLS = POOL_WIDTH + 4 * HG_WIDTH
EPS = 1e-6

kernel_name = "hybrid_pool_hgrn2_decode_step"


def rms_norm(x, g):
    x32 = x.astype(jnp.float32)
    y = x32 * lax.rsqrt(jnp.mean(x32 * x32, axis=-1, keepdims=True) + EPS)
    return (y * g.astype(jnp.float32)).astype(x.dtype)


def pool_mix(u, prev, pos0, w_pool, pool_scale):
    B, T, _ = u.shape
    full = jnp.concatenate([prev.astype(u.dtype), u], axis=1)
    cs = jnp.cumsum(full.astype(jnp.float32), axis=1)
    cs = jnp.concatenate([jnp.zeros((B, 1, POOL_WIDTH), jnp.float32), cs], axis=1)
    end = cs[:, POOL_CACHE + 1:]
    pos = pos0 + jnp.arange(T)
    means = []
    for gi, w in enumerate(POOL_WINDOWS):
        sl = slice(gi * POOL_GROUP_DIM, (gi + 1) * POOL_GROUP_DIM)
        start = cs[:, POOL_CACHE + 1 - w:POOL_CACHE + 1 - w + T, sl]
        cnt = jnp.minimum(pos + 1, w).astype(jnp.float32)[None, :, None]
        means.append((end[..., sl] - start) / cnt)
    pooled = jnp.concatenate(means, axis=-1) - u.astype(jnp.float32)
    pooled = pooled.reshape(B, T, POOL_GROUPS, POOL_GROUP_DIM)
    out = jnp.einsum('btgc,gcd->btgd', pooled, w_pool.astype(jnp.float32))
    out = out.reshape(B, T, POOL_WIDTH) * pool_scale.astype(jnp.float32)
    return out.astype(u.dtype), full[:, -POOL_CACHE:]


def hgrn2_recurrence(q, k, v, logf, S0):
    B, T, H, K = q.shape
    V = v.shape[-1]
    pad = (-T) % CHUNK
    n = (T + pad) // CHUNK

    def prep(a):
        a = jnp.pad(a.astype(jnp.float32), ((0, 0), (0, pad), (0, 0), (0, 0)))
        return a.reshape(B, n, CHUNK, H, a.shape[-1]).transpose(1, 0, 3, 2, 4)

    qs, ks, vs, gs = prep(q), prep(k), prep(v), prep(logf)
    causal = jnp.tril(jnp.ones((CHUNK, CHUNK), bool))[:, :, None]

    def step(S, inp):
        qc, kc, vc, gc = inp
        b = jnp.cumsum(gc, axis=2)
        diff = b[:, :, :, None, :] - b[:, :, None, :, :]
        decay = jnp.exp(jnp.where(causal, diff, -jnp.inf))
        scores = jnp.einsum('bhtk,bhsk,bhtsk->bhts', qc, kc, decay)
        o = (jnp.einsum('bhts,bhsv->bhtv', scores, vc)
             + jnp.einsum('bhtk,bhkv->bhtv', qc * jnp.exp(b), S))
        b_last = b[:, :, -1:, :]
        S = (jnp.exp(b_last[:, :, 0, :])[..., None] * S
             + jnp.einsum('bhsk,bhsv->bhkv', kc * jnp.exp(b_last - b), vc))
        return S, o

    S, o = lax.scan(step, S0.astype(jnp.float32), (qs, ks, vs, gs))
    o = o.transpose(1, 0, 3, 2, 4).reshape(B, n * CHUNK, H, V)[:, :T]
    return o, S


def run_group(x, pool_prev, S_prev, pos0, lbs, norm1, w_in, w_pool, pool_scale,
              hg_norm, w_o, norm2, w_gate, w_up, w_down, norm_f):
    B, T, _ = x.shape
    pools, states = [], []
    for l in range(DEPTH):
        h = rms_norm(x, norm1[l])
        z = h @ w_in[l]
        u, q, f, i, g = jnp.split(
            z, [POOL_WIDTH, POOL_WIDTH + HG_WIDTH, POOL_WIDTH + 2 * HG_WIDTH,
                POOL_WIDTH + 3 * HG_WIDTH], axis=-1)
        pool_out, pool_new = pool_mix(u, pool_prev[l], pos0, w_pool[l], pool_scale[l])
        lb = lbs[l]
        forget = lb + (1.0 - lb) * jax.nn.sigmoid(f.astype(jnp.float32))
        logf = jnp.log(forget)
        k_in = 1.0 - forget
        q_f = jax.nn.silu(q.astype(jnp.float32))
        heads = lambda a: a.reshape(B, T, HG_HEADS, HG_HEAD_DIM)
        o, S_new = hgrn2_recurrence(heads(q_f), heads(k_in), heads(i), heads(logf), S_prev[l])
        o = o * lax.rsqrt(jnp.mean(o * o, axis=-1, keepdims=True) + EPS)
        o = o * hg_norm[l].astype(jnp.float32).reshape(HG_HEADS, HG_HEAD_DIM)
        o = o.reshape(B, T, HG_WIDTH) * jax.nn.silu(g.astype(jnp.float32))
        mix = jnp.concatenate([pool_out, o.astype(x.dtype)], axis=-1) @ w_o[l]
        x = x + mix
        h2 = rms_norm(x, norm2[l])
        x = x + (jax.nn.silu(h2 @ w_gate[l]) * (h2 @ w_up[l])) @ w_down[l]
        pools.append(pool_new)
        states.append(S_new.astype(S_prev.dtype))
    return rms_norm(x, norm_f), jnp.stack(pools), jnp.stack(states)


def setup_inputs(seed: int = 0) -> dict:
    key = jax.random.key(seed)
    ks = jax.random.split(key, 20)
    nrm = lambda k, s, sc: jax.random.normal(k, s, jnp.float32) * sc
    return {
        "x_prompt": nrm(ks[0], (BATCH, SEQ, D_MODEL), 1.0),
        "x_sample": nrm(ks[1], (DEC_BATCH, DEC_SEQ, D_MODEL), 1.0),
        "cache_pool": nrm(ks[2], (DEPTH, DEC_BATCH, POOL_CACHE, POOL_WIDTH), 1.0),
        "state_hgrn": nrm(ks[3], (DEPTH, DEC_BATCH, HG_HEADS, HG_HEAD_DIM, HG_HEAD_DIM), 0.3),
        "lb_param": nrm(ks[4], (DEPTH + 1, HG_WIDTH), 0.1),
        "norm1": 1.0 + nrm(ks[5], (DEPTH, D_MODEL), 0.02),
        "w_in": nrm(ks[6], (DEPTH, D_MODEL, IN_COLS), D_MODEL ** -0.5),
        "w_pool": nrm(ks[7], (DEPTH, POOL_GROUPS, POOL_GROUP_DIM, POOL_GROUP_DIM), POOL_GROUP_DIM ** -0.5),
        "pool_scale": 1.0 + nrm(ks[8], (DEPTH, POOL_WIDTH), 0.02),
        "hg_norm": 1.0 + nrm(ks[9], (DEPTH, HG_WIDTH), 0.02),
        "w_o": nrm(ks[10], (DEPTH, MIX_WIDTH, D_MODEL), MIX_WIDTH ** -0.5),
        "norm2": 1.0 + nrm(ks[11], (DEPTH, D_MODEL), 0.02),
        "w_gate": nrm(ks[12], (DEPTH, D_MODEL, D_FF), D_MODEL ** -0.5),
        "w_up": nrm(ks[13], (DEPTH, D_MODEL, D_FF), D_MODEL ** -0.5),
        "w_down": nrm(ks[14], (DEPTH, D_FF, D_MODEL), D_FF ** -0.5),
        "norm_f": 1.0 + nrm(ks[15], (D_MODEL,), 0.02),
    }


def reference(x_prompt, x_sample, cache_pool, state_hgrn, lb_param, norm1, w_in, w_pool,
              pool_scale, hg_norm, w_o, norm2, w_gate, w_up, w_down, norm_f):
    lbs = jnp.cumsum(jax.nn.softmax(lb_param.astype(jnp.float32), axis=0), axis=0)
    weights = (norm1, w_in, w_pool, pool_scale, hg_norm, w_o, norm2, w_gate, w_up, w_down, norm_f)
    pool_zero = jnp.zeros((DEPTH, BATCH, POOL_CACHE, POOL_WIDTH), x_prompt.dtype)
    state_zero = jnp.zeros((DEPTH, BATCH, HG_HEADS, HG_HEAD_DIM, HG_HEAD_DIM), state_hgrn.dtype)
    y_prompt, pool_state_prompt, hgrn_state_prompt = run_group(
        x_prompt, pool_zero, state_zero, 0, lbs, *weights)
    y_sample, pool_state_sample, hgrn_state_sample = run_group(
        x_sample, cache_pool, state_hgrn, PAST_LEN, lbs, *weights)
    return (y_prompt, y_sample, pool_state_prompt, hgrn_state_prompt, pool_state_sample, hgrn_state_sample)
```

```python
import functools

import jax
import jax.numpy as jnp
from jax import lax
from jax.experimental import pallas as pl
from jax.experimental.pallas import tpu as pltpu

D_MODEL = 2048
POOL_WIDTH = 1024
POOL_WINDOWS = (2, 4, 8, 16)
POOL_GROUP_DIM = POOL_WIDTH // len(POOL_WINDOWS)
POOL_CACHE = max(POOL_WINDOWS) - 1
HG_WIDTH = 1024
HEAD_DIM = 128
HEADS = HG_WIDTH // HEAD_DIM
IN_COLS = POOL_WIDTH + 4 * HG_WIDTH
EPS = 1e-6
SUBLANES = 8

F32 = jnp.float32
BF16 = jnp.bfloat16

MIB = 1024 * 1024


def _cparams(sem, vmem_mib):
    return pltpu.CompilerParams(dimension_semantics=sem, vmem_limit_bytes=vmem_mib * MIB)


def _rms(x, g):
    return x * lax.rsqrt(jnp.mean(x * x, axis=-1, keepdims=True) + EPS) * g


def _sigmoid(x):
    return 1.0 / (1.0 + jnp.exp(-x))


def _in_proj_body(x_ref, g_ref, w_ref, z_ref, h_ref):
    @pl.when(pl.program_id(1) == 0)
    def _():
        h_ref[...] = _rms(x_ref[...], g_ref[...]).astype(BF16)

    z_ref[...] = jnp.dot(h_ref[...], w_ref[...], preferred_element_type=F32)


def _in_proj(x, g, w, tm, tn):
    m = x.shape[0]
    return pl.pallas_call(
        _in_proj_body,
        grid=(m // tm, IN_COLS // tn),
        in_specs=[
            pl.BlockSpec((tm, D_MODEL), lambda i, j: (i, 0)),
            pl.BlockSpec((1, D_MODEL), lambda i, j: (0, 0)),
            pl.BlockSpec((D_MODEL, tn), lambda i, j: (0, j)),
        ],
        out_specs=pl.BlockSpec((tm, tn), lambda i, j: (i, j)),
        out_shape=jax.ShapeDtypeStruct((m, IN_COLS), F32),
        scratch_shapes=[pltpu.VMEM((tm, D_MODEL), BF16)],
        compiler_params=_cparams(("arbitrary", "arbitrary"), 48),
        name="in_proj",
    )(x, g, w)


def _lower_bound(lbp, layer):
    e = jnp.exp(lbp - jnp.max(lbp, axis=0, keepdims=True))
    return jnp.sum(e[: layer + 1], axis=0, keepdims=True) / jnp.sum(e, axis=0, keepdims=True)


def _pool_project(acc, cnt, u, gi, wpool_ref, pscale_ref, mix_ref):
    c0 = gi * POOL_GROUP_DIM
    pooled = acc / cnt - u
    out = jnp.dot(pooled.astype(BF16), wpool_ref[gi], preferred_element_type=F32)
    mix_ref[:, c0:c0 + POOL_GROUP_DIM] = (out * pscale_ref[:, c0:c0 + POOL_GROUP_DIM]).astype(BF16)


def _to_token_major(ref, x, rows):
    for j in range(HEADS):
        ref[pl.ds(j, rows, stride=HEADS), :] = x[:, j * HEAD_DIM:(j + 1) * HEAD_DIM]
    return ref[...].reshape(rows, HEADS, HEAD_DIM)


def _head_rows(ref3, lvl, j, rows):
    return ref3[lvl, pl.ds(j, rows, stride=HEADS), :]


def _level_operands(z_ref, lb, rows, levels, t8q_ref, t8k_ref, t8f_ref, qlev_ref, klev_ref):
    zq = z_ref[:, POOL_WIDTH:POOL_WIDTH + HG_WIDTH]
    zf = z_ref[:, POOL_WIDTH + HG_WIDTH:POOL_WIDTH + 2 * HG_WIDTH]
    fg = lb + (1.0 - lb) * _sigmoid(zf)
    q8 = _to_token_major(t8q_ref, zq * _sigmoid(zq), rows)
    k8 = _to_token_major(t8k_ref, 1.0 - fg, rows)
    f8 = _to_token_major(t8f_ref, fg, rows)

    tile = (HEADS, HEAD_DIM)
    p = f8
    r = jnp.ones_like(f8)
    for lvl in range(levels):
        h = 1 << lvl
        qlev_ref[lvl] = (q8 * p).reshape(rows * HEADS, HEAD_DIM)
        klev_ref[lvl] = (k8 * r).reshape(rows * HEADS, HEAD_DIM)
        blocks = rows // (2 * h)
        p4 = p.reshape(blocks, 2 * h, *tile)
        r4 = r.reshape(blocks, 2 * h, *tile)
        f4 = f8.reshape(blocks, 2 * h, *tile)
        p = jnp.concatenate([p4[:, :h], p4[:, h:] * p4[:, h - 1:h]], axis=1).reshape(rows, *tile)
        upper = f4[:, h:h + 1] * r4[:, h:h + 1]
        r = jnp.concatenate([r4[:, :h] * upper, r4[:, h:]], axis=1).reshape(rows, *tile)
    qlev_ref[levels] = (q8 * p).reshape(rows * HEADS, HEAD_DIM)
    klev_ref[levels] = (k8 * r).reshape(rows * HEADS, HEAD_DIM)
    return p


def _level_masks(rows, levels):
    ti = lax.broadcasted_iota(jnp.int32, (rows, rows), 0)
    si = lax.broadcasted_iota(jnp.int32, (rows, rows), 1)
    x = ti ^ si
    masks = [((x >> lvl) == 1) & (((ti >> lvl) & 1) == 1) for lvl in range(levels)]
    return masks, ti == si


def _nt(a, b):
    return lax.dot_general(a, b, (((1,), (1,)), ((), ())), preferred_element_type=F32)


def _intra_scores(j, rows, levels, masks, eye, t8q_ref, t8k_ref, qlev_ref, klev_ref):
    qd = t8q_ref[pl.ds(j, rows, stride=HEADS), :].astype(BF16)
    kd = t8k_ref[pl.ds(j, rows, stride=HEADS), :].astype(BF16)
    scores = jnp.where(eye, _nt(qd, kd), 0.0)
    for lvl in range(levels):
        ql = _head_rows(qlev_ref, lvl, j, rows).astype(BF16)
        kl = _head_rows(klev_ref, lvl, j, rows).astype(BF16)
        scores = jnp.where(masks[lvl], _nt(ql, kl), scores)
    return scores


def _head_output(o, j, z_ref, hgn_ref, mix_ref):
    c0 = j * HEAD_DIM
    zg = z_ref[:, POOL_WIDTH + 3 * HG_WIDTH + c0:POOL_WIDTH + 3 * HG_WIDTH + c0 + HEAD_DIM]
    o = _rms(o, hgn_ref[:, c0:c0 + HEAD_DIM]) * (zg * _sigmoid(zg))
    mix_ref[:, POOL_WIDTH + c0:POOL_WIDTH + c0 + HEAD_DIM] = o.astype(BF16)


PROMPT_ROWS = 128
PROMPT_LEVELS = 7


def _mixer_prompt_body(z_ref, lbp_ref, wpool_ref, pscale_ref, hgn_ref,
                       mix_ref, pooln_ref, snew_ref,
                       ext_ref, s_ref, t8q_ref, t8k_ref, t8f_ref, qlev_ref, klev_ref, *, layer):
    rows, levels = PROMPT_ROWS, PROMPT_LEVELS
    hist = 2 * SUBLANES
    t = pl.program_id(1)
    last = pl.num_programs(1) - 1

    @pl.when(t == 0)
    def _():
        ext_ref[0:hist, :] = jnp.zeros((hist, POOL_WIDTH), F32)
        s_ref[...] = jnp.zeros_like(s_ref)

    ext_ref[hist:hist + rows, :] = z_ref[:, 0:POOL_WIDTH]
    pos = t * rows + lax.broadcasted_iota(jnp.int32, (rows, 1), 0)
    for gi, w in enumerate(POOL_WINDOWS):
        c0 = gi * POOL_GROUP_DIM
        u = ext_ref[hist:hist + rows, c0:c0 + POOL_GROUP_DIM]
        acc = u
        for d in range(1, w):
            acc = acc + ext_ref[hist - d:hist - d + rows, c0:c0 + POOL_GROUP_DIM]
        cnt = jnp.minimum(pos + 1, w).astype(F32)
        _pool_project(acc, cnt, u, gi, wpool_ref, pscale_ref, mix_ref)

    @pl.when(t == last)
    def _():
        pooln_ref[0] = ext_ref[hist + rows - POOL_CACHE:hist + rows, :]

    ext_ref[0:hist, :] = ext_ref[rows:rows + hist, :]

    lb = _lower_bound(lbp_ref[...], layer)
    p = _level_operands(z_ref, lb, rows, levels, t8q_ref, t8k_ref, t8f_ref, qlev_ref, klev_ref)
    decay_t = jnp.transpose(p[rows - 1])
    masks, eye = _level_masks(rows, levels)
    for j in range(HEADS):
        c0 = j * HEAD_DIM
        scores = _intra_scores(j, rows, levels, masks, eye, t8q_ref, t8k_ref, qlev_ref, klev_ref)
        v = z_ref[:, POOL_WIDTH + 2 * HG_WIDTH + c0:POOL_WIDTH + 2 * HG_WIDTH + c0 + HEAD_DIM].astype(BF16)
        qt = _head_rows(qlev_ref, levels, j, rows).astype(BF16)
        kt = _head_rows(klev_ref, levels, j, rows).astype(BF16)
        s_in = s_ref[j]
        o = (jnp.dot(scores.astype(BF16), v, preferred_element_type=F32)
             + jnp.dot(qt, s_in.astype(BF16), preferred_element_type=F32))
        s_ref[j] = decay_t[:, j:j + 1] * s_in + lax.dot_general(
            kt, v, (((0,), (0,)), ((), ())), preferred_element_type=F32)
        _head_output(o, j, z_ref, hgn_ref, mix_ref)

    @pl.when(t == last)
    def _():
        snew_ref[0] = s_ref[...]


def _mixer_prompt(z, lbp, wpool, pscale, hgn, batch, seq, layer):
    rows = PROMPT_ROWS
    nt = seq // rows
    hist = 2 * SUBLANES
    const2 = lambda b, t: (0, 0)
    return pl.pallas_call(
        functools.partial(_mixer_prompt_body, layer=layer),
        grid=(batch, nt),
        in_specs=[
            pl.BlockSpec((rows, IN_COLS), lambda b, t: (b * nt + t, 0)),
            pl.BlockSpec(lbp.shape, const2),
            pl.BlockSpec(wpool.shape, lambda b, t: (0, 0, 0)),
            pl.BlockSpec((1, POOL_WIDTH), const2),
            pl.BlockSpec((1, HG_WIDTH), const2),
        ],
        out_specs=[
            pl.BlockSpec((rows, D_MODEL), lambda b, t: (b * nt + t, 0)),
            pl.BlockSpec((1, POOL_CACHE, POOL_WIDTH), lambda b, t: (b, 0, 0)),
            pl.BlockSpec((1, HEADS, HEAD_DIM, HEAD_DIM), lambda b, t: (b, 0, 0, 0)),
        ],
        out_shape=[
            jax.ShapeDtypeStruct((batch * seq, D_MODEL), BF16),
            jax.ShapeDtypeStruct((batch, POOL_CACHE, POOL_WIDTH), F32),
            jax.ShapeDtypeStruct((batch, HEADS, HEAD_DIM, HEAD_DIM), F32),
        ],
        scratch_shapes=[
            pltpu.VMEM((hist + rows, POOL_WIDTH), F32),
            pltpu.VMEM((HEADS, HEAD_DIM, HEAD_DIM), F32),
            pltpu.VMEM((rows * HEADS, HEAD_DIM), F32),
            pltpu.VMEM((rows * HEADS, HEAD_DIM), F32),
            pltpu.VMEM((rows * HEADS, HEAD_DIM), F32),
            pltpu.VMEM((PROMPT_LEVELS + 1, rows * HEADS, HEAD_DIM), F32),
            pltpu.VMEM((PROMPT_LEVELS + 1, rows * HEADS, HEAD_DIM), F32),
        ],
        compiler_params=_cparams(("arbitrary", "arbitrary"), 48),
        name="mixer_prompt",
    )(z, lbp, wpool, pscale, hgn)


def _mixer_sample_body(z_ref, cache_ref, sin_ref, lbp_ref, wpool_ref, pscale_ref, hgn_ref,
                       mix_ref, pooln_ref, snew_ref,
                       ext_ref, t8q_ref, t8k_ref, t8f_ref, qlev_ref, klev_ref, *, layer, nseq, steps, pos0):
    rows = nseq * steps
    levels = steps.bit_length() - 1
    hist = 2 * SUBLANES

    ext_ref[:, 0:hist - POOL_CACHE, :] = jnp.zeros((nseq, hist - POOL_CACHE, POOL_WIDTH), F32)
    ext_ref[:, hist - POOL_CACHE:hist, :] = cache_ref[...]
    ext_ref[:, hist:hist + steps, :] = z_ref[:, 0:POOL_WIDTH].reshape(nseq, steps, POOL_WIDTH)
    pos = pos0 + lax.broadcasted_iota(jnp.int32, (1, steps, 1), 1)
    for gi, w in enumerate(POOL_WINDOWS):
        c0 = gi * POOL_GROUP_DIM
        u = ext_ref[:, hist:hist + steps, c0:c0 + POOL_GROUP_DIM]
        acc = u
        for d in range(1, w):
            acc = acc + ext_ref[:, hist - d:hist - d + steps, c0:c0 + POOL_GROUP_DIM]
        cnt = jnp.minimum(pos + 1, w).astype(F32)
        pooled_in = (acc / cnt).reshape(rows, POOL_GROUP_DIM)
        _pool_project(pooled_in, 1.0, u.reshape(rows, POOL_GROUP_DIM), gi, wpool_ref, pscale_ref, mix_ref)
    pooln_ref[...] = ext_ref[:, hist + steps - POOL_CACHE:hist + steps, :]

    lb = _lower_bound(lbp_ref[...], layer)
    p = _level_operands(z_ref, lb, rows, levels, t8q_ref, t8k_ref, t8f_ref, qlev_ref, klev_ref)
    decay = p.reshape(nseq, steps, HEADS, HEAD_DIM)[:, steps - 1]
    decay_t = jnp.swapaxes(decay, 1, 2)
    masks, eye = _level_masks(rows, levels)
    for j in range(HEADS):
        c0 = j * HEAD_DIM
        scores = _intra_scores(j, rows, levels, masks, eye, t8q_ref, t8k_ref, qlev_ref, klev_ref)
        v32 = z_ref[:, POOL_WIDTH + 2 * HG_WIDTH + c0:POOL_WIDTH + 2 * HG_WIDTH + c0 + HEAD_DIM]
        v = v32.astype(BF16)
        v3 = v32.reshape(nseq, steps, HEAD_DIM).astype(BF16)
        qt3 = _head_rows(qlev_ref, levels, j, rows).reshape(nseq, steps, HEAD_DIM).astype(BF16)
        kt3 = _head_rows(klev_ref, levels, j, rows).reshape(nseq, steps, HEAD_DIM).astype(BF16)
        s_in = sin_ref[:, j]
        o_state = lax.dot_general(qt3, s_in.astype(BF16), (((2,), (1,)), ((0,), (0,))),
                                  preferred_element_type=F32)
        o = jnp.dot(scores.astype(BF16), v, preferred_element_type=F32) + o_state.reshape(rows, HEAD_DIM)
        upd = lax.dot_general(kt3, v3, (((1,), (1,)), ((0,), (0,))), preferred_element_type=F32)
        snew_ref[:, j] = decay_t[:, :, j:j + 1] * s_in + upd
        _head_output(o, j, z_ref, hgn_ref, mix_ref)


def _mixer_sample(z, cache, state, lbp, wpool, pscale, hgn, nseq_total, steps, pos0, layer, nseq):
    rows = nseq * steps
    levels = steps.bit_length() - 1
    assert 1 << levels == steps and nseq_total % nseq == 0
    hist = 2 * SUBLANES
    const2 = lambda i: (0, 0)
    return pl.pallas_call(
        functools.partial(_mixer_sample_body, layer=layer, nseq=nseq, steps=steps, pos0=pos0),
        grid=(nseq_total // nseq,),
        in_specs=[
            pl.BlockSpec((rows, IN_COLS), lambda i: (i, 0)),
            pl.BlockSpec((nseq, POOL_CACHE, POOL_WIDTH), lambda i: (i, 0, 0)),
            pl.BlockSpec((nseq, HEADS, HEAD_DIM, HEAD_DIM), lambda i: (i, 0, 0, 0)),
            pl.BlockSpec(lbp.shape, const2),
            pl.BlockSpec(wpool.shape, lambda i: (0, 0, 0)),
            pl.BlockSpec((1, POOL_WIDTH), const2),
            pl.BlockSpec((1, HG_WIDTH), const2),
        ],
        out_specs=[
            pl.BlockSpec((rows, D_MODEL), lambda i: (i, 0)),
            pl.BlockSpec((nseq, POOL_CACHE, POOL_WIDTH), lambda i: (i, 0, 0)),
            pl.BlockSpec((nseq, HEADS, HEAD_DIM, HEAD_DIM), lambda i: (i, 0, 0, 0)),
        ],
        out_shape=[
            jax.ShapeDtypeStruct((nseq_total * steps, D_MODEL), BF16),
            jax.ShapeDtypeStruct((nseq_total, POOL_CACHE, POOL_WIDTH), F32),
            jax.ShapeDtypeStruct((nseq_total, HEADS, HEAD_DIM, HEAD_DIM), F32),
        ],
        scratch_shapes=[
            pltpu.VMEM((nseq, hist + steps, POOL_WIDTH), F32),
            pltpu.VMEM((rows * HEADS, HEAD_DIM), F32),
            pltpu.VMEM((rows * HEADS, HEAD_DIM), F32),
            pltpu.VMEM((rows * HEADS, HEAD_DIM), F32),
            pltpu.VMEM((levels + 1, rows * HEADS, HEAD_DIM), F32),
            pltpu.VMEM((levels + 1, rows * HEADS, HEAD_DIM), F32),
        ],
        compiler_params=_cparams(("arbitrary",), 48),
        name="mixer_sample",
    )(z, cache, state, lbp, wpool, pscale, hgn)


def _out_proj_body(x_ref, mix_ref, w_ref, o_ref):
    o_ref[...] = x_ref[...] + jnp.dot(mix_ref[...], w_ref[...], preferred_element_type=F32)


def _out_proj(x, mix, w, tm):
    m = x.shape[0]
    return pl.pallas_call(
        _out_proj_body,
        grid=(m // tm,),
        in_specs=[
            pl.BlockSpec((tm, D_MODEL), lambda i: (i, 0)),
            pl.BlockSpec((tm, D_MODEL), lambda i: (i, 0)),
            pl.BlockSpec((D_MODEL, D_MODEL), lambda i: (0, 0)),
        ],
        out_specs=pl.BlockSpec((tm, D_MODEL), lambda i: (i, 0)),
        out_shape=jax.ShapeDtypeStruct((m, D_MODEL), F32),
        compiler_params=_cparams(("arbitrary",), 48),
        name="out_proj",
    )(x, mix, w)


def _ffn_body(x_ref, g2_ref, wg_ref, wu_ref, wd_ref, gf_ref, y_ref, h_ref, acc_ref):
    j = pl.program_id(1)

    @pl.when(j == 0)
    def _():
        h_ref[...] = _rms(x_ref[...], g2_ref[...]).astype(BF16)
        acc_ref[...] = jnp.zeros_like(acc_ref)

    h = h_ref[...]
    gate = jnp.dot(h, wg_ref[...], preferred_element_type=F32)
    up = jnp.dot(h, wu_ref[...], preferred_element_type=F32)
    act = (gate * _sigmoid(gate) * up).astype(BF16)
    acc_ref[...] += jnp.dot(act, wd_ref[...], preferred_element_type=F32)

    @pl.when(j == pl.num_programs(1) - 1)
    def _():
        y_ref[...] = _rms(x_ref[...] + acc_ref[...], gf_ref[...])


def _ffn(x, g2, wg, wu, wd, gf, tm, tf):
    m = x.shape[0]
    d_ff = wg.shape[1]
    return pl.pallas_call(
        _ffn_body,
        grid=(m // tm, d_ff // tf),
        in_specs=[
            pl.BlockSpec((tm, D_MODEL), lambda i, j: (i, 0)),
            pl.BlockSpec((1, D_MODEL), lambda i, j: (0, 0)),
            pl.BlockSpec((D_MODEL, tf), lambda i, j: (0, j)),
            pl.BlockSpec((D_MODEL, tf), lambda i, j: (0, j)),
            pl.BlockSpec((tf, D_MODEL), lambda i, j: (j, 0)),
            pl.BlockSpec((1, D_MODEL), lambda i, j: (0, 0)),
        ],
        out_specs=pl.BlockSpec((tm, D_MODEL), lambda i, j: (i, 0)),
        out_shape=jax.ShapeDtypeStruct((m, D_MODEL), F32),
        scratch_shapes=[pltpu.VMEM((tm, D_MODEL), BF16), pltpu.VMEM((tm, D_MODEL), F32)],
        compiler_params=_cparams(("arbitrary", "arbitrary"), 48),
        name="ffn",
    )(x, g2, wg, wu, wd, gf)


def _token_stages(x2, mixer, w):
    z = _in_proj(x2, w["norm1"], w["w_in"], tm=1024, tn=1024)
    mix, pool_new, s_new = mixer(z)
    x1 = _out_proj(x2, mix, w["w_o"], tm=512)
    y = _ffn(x1, w["norm2"], w["w_gate"], w["w_up"], w["w_down"], w["norm_f"], tm=512, tf=512)
    return y, pool_new, s_new


def kernel(x_prompt, x_sample, cache_pool, state_hgrn, lb_param, norm1, w_in, w_pool, pool_scale,
           hg_norm, w_o, norm2, w_gate, w_up, w_down, norm_f):
    depth = w_in.shape[0]
    assert depth == 1, "single-layer trunk"
    layer = 0
    batch, seq, _ = x_prompt.shape
    dec_batch, dec_seq, _ = x_sample.shape
    past_len = 16384

    row = lambda a: a.reshape(1, -1).astype(F32)
    w = dict(
        norm1=row(norm1[layer]), w_in=w_in[layer].astype(BF16),
        w_o=w_o[layer].astype(BF16), norm2=row(norm2[layer]),
        w_gate=w_gate[layer].astype(BF16), w_up=w_up[layer].astype(BF16),
        w_down=w_down[layer].astype(BF16), norm_f=row(norm_f),
    )
    lbp = lb_param.astype(F32)
    wpool = w_pool[layer].astype(BF16)
    pscale = row(pool_scale[layer])
    hgn = row(hg_norm[layer])

    mixer_p = lambda z: _mixer_prompt(z, lbp, wpool, pscale, hgn, batch, seq, layer)
    y_p, pool_p, s_p = _token_stages(x_prompt.reshape(batch * seq, D_MODEL), mixer_p, w)

    mixer_s = lambda z: _mixer_sample(z, cache_pool[layer], state_hgrn[layer], lbp, wpool, pscale, hgn,
                                      dec_batch, dec_seq, past_len, layer, nseq=8)
    y_s, pool_s, s_s = _token_stages(x_sample.reshape(dec_batch * dec_seq, D_MODEL), mixer_s, w)

    return (y_p.reshape(batch, seq, D_MODEL), y_s.reshape(dec_batch, dec_seq, D_MODEL),
            pool_p[None], s_p[None], pool_s[None], s_s[None])
```

```python
import functools

import jax
import jax.numpy as jnp
from jax import lax
from jax.experimental import pallas as pl
from jax.experimental.pallas import tpu as pltpu

D_MODEL = 2048
POOL_WIDTH = 1024
POOL_WINDOWS = (2, 4, 8, 16)
POOL_GROUP_DIM = POOL_WIDTH // len(POOL_WINDOWS)
POOL_CACHE = max(POOL_WINDOWS) - 1
HG_WIDTH = 1024
HEAD_DIM = 128
HEADS = HG_WIDTH // HEAD_DIM
IN_COLS = POOL_WIDTH + 4 * HG_WIDTH
EPS = 1e-6
SUBLANES = 8
assert HEADS == SUBLANES
HIST_ROWS = 2 * SUBLANES
PAST_LEN = 16384

F32 = jnp.float32
BF16 = jnp.bfloat16

MIB = 1024 * 1024


def _cparams(sem, vmem_mib):
    return pltpu.CompilerParams(dimension_semantics=sem, vmem_limit_bytes=vmem_mib * MIB)


def _rms(x, g):
    return x * lax.rsqrt(jnp.mean(x * x, axis=-1, keepdims=True) + EPS) * g


def _sigmoid(x):
    return 1.0 / (1.0 + jnp.exp(-x))


def _in_proj_body(x_ref, g_ref, w_ref, z_ref, h_ref):
    @pl.when(pl.program_id(1) == 0)
    def _():
        h_ref[...] = _rms(x_ref[...], g_ref[...]).astype(BF16)

    z_ref[...] = jnp.dot(h_ref[...], w_ref[...], preferred_element_type=F32)


def _in_proj(x, g, w, tm, tn):
    m = x.shape[0]
    return pl.pallas_call(
        _in_proj_body,
        grid=(m // tm, IN_COLS // tn),
        in_specs=[
            pl.BlockSpec((tm, D_MODEL), lambda i, j: (i, 0)),
            pl.BlockSpec((1, D_MODEL), lambda i, j: (0, 0)),
            pl.BlockSpec((D_MODEL, tn), lambda i, j: (0, j)),
        ],
        out_specs=pl.BlockSpec((tm, tn), lambda i, j: (i, j)),
        out_shape=jax.ShapeDtypeStruct((m, IN_COLS), F32),
        scratch_shapes=[pltpu.VMEM((tm, D_MODEL), BF16)],
        compiler_params=_cparams(("arbitrary", "arbitrary"), 48),
        name="in_proj",
    )(x, g, w)


def _lower_bound(lbp, layer):
    e = jnp.exp(lbp - jnp.max(lbp, axis=0, keepdims=True))
    return jnp.sum(e[: layer + 1], axis=0, keepdims=True) / jnp.sum(e, axis=0, keepdims=True)


def _pool_project(acc, cnt, u, gi, wpool_ref, pscale_ref, mix_ref):
    c0 = gi * POOL_GROUP_DIM
    pooled = acc / cnt - u
    out = jnp.dot(pooled.astype(BF16), wpool_ref[gi], preferred_element_type=F32)
    mix_ref[:, c0:c0 + POOL_GROUP_DIM] = (out * pscale_ref[:, c0:c0 + POOL_GROUP_DIM]).astype(BF16)


def _to_token_major(nat_ref, a, x, rows):
    groups = rows // SUBLANES
    for c in range(HEADS):
        nat_ref[a, :, c * SUBLANES:(c + 1) * SUBLANES, :] = (
            x[:, c * HEAD_DIM:(c + 1) * HEAD_DIM].reshape(groups, SUBLANES, HEAD_DIM))
    return jnp.stack([nat_ref[a, g, pl.ds(sl, HEADS, stride=SUBLANES), :]
                      for g in range(groups) for sl in range(SUBLANES)])


def _head_rows(lev_ref, lvl, j, rows):
    return lev_ref[lvl, pl.ds(j, rows, stride=HEADS), :]


def _level_operands(z_ref, lb, rows, levels, nat_ref, qk_ref, lev_ref):
    zq = z_ref[:, POOL_WIDTH:POOL_WIDTH + HG_WIDTH]
    zf = z_ref[:, POOL_WIDTH + HG_WIDTH:POOL_WIDTH + 2 * HG_WIDTH]
    fg = lb + (1.0 - lb) * _sigmoid(zf)
    qs = zq * _sigmoid(zq)
    kk = 1.0 - fg
    qk_ref[0] = qs.astype(BF16)
    qk_ref[1] = kk.astype(BF16)
    q8 = _to_token_major(nat_ref, 0, qs, rows)
    k8 = _to_token_major(nat_ref, 1, kk, rows)
    f8 = _to_token_major(nat_ref, 2, fg, rows)

    tile = (HEADS, HEAD_DIM)
    flat = (rows * HEADS, HEAD_DIM)
    p = f8
    r = jnp.ones_like(f8)
    for lvl in range(levels):
        h = 1 << lvl
        blocks = rows // (2 * h)
        p4, r4, f4, q4, k4 = (a.reshape(blocks, 2 * h, *tile) for a in (p, r, f8, q8, k8))
        lev_ref[lvl] = jnp.concatenate([k4[:, :h] * r4[:, :h], q4[:, h:] * p4[:, h:]], axis=1).reshape(flat)
        p = jnp.concatenate([p4[:, :h], p4[:, h:] * p4[:, h - 1:h]], axis=1).reshape(rows, *tile)
        upper = f4[:, h:h + 1] * r4[:, h:h + 1]
        r = jnp.concatenate([r4[:, :h] * upper, r4[:, h:]], axis=1).reshape(rows, *tile)
    lev_ref[levels] = (q8 * p).reshape(flat)
    lev_ref[levels + 1] = (k8 * r).reshape(flat)
    return p


def _level_scratch(rows, levels):
    return [
        pltpu.VMEM((3, rows // SUBLANES, HEADS * SUBLANES, HEAD_DIM), F32),
        pltpu.VMEM((2, rows, HG_WIDTH), BF16),
        pltpu.VMEM((levels + 2, rows * HEADS, HEAD_DIM), F32),
        pltpu.VMEM((rows, rows), F32),
    ]


def _level_masks(rows, levels):
    ti = lax.broadcasted_iota(jnp.int32, (rows, rows), 0)
    si = lax.broadcasted_iota(jnp.int32, (rows, rows), 1)
    x = ti ^ si
    masks = []
    for lvl in range(levels):
        h = 1 << lvl
        if h < SUBLANES:
            masks.append(((x >> lvl) == 1) & (((ti >> lvl) & 1) == 1))
        else:
            half = lax.broadcasted_iota(jnp.int32, (h, rows), 1) >> lvl
            masks.append([half == 2 * b for b in range(rows // (2 * h))])
    return masks, ti == si


def _nt(a, b):
    return lax.dot_general(a, b, (((1,), (1,)), ((), ())), preferred_element_type=F32)


def _intra_scores(j, rows, levels, masks, eye, qk_ref, lev_ref, sc_ref):
    c0 = j * HEAD_DIM
    pltpu.store(sc_ref, _nt(qk_ref[0, :, c0:c0 + HEAD_DIM], qk_ref[1, :, c0:c0 + HEAD_DIM]), mask=eye)
    for lvl in range(levels):
        h = 1 << lvl
        x32 = _head_rows(lev_ref, lvl, j, rows)
        x = x32.astype(BF16)
        if h < SUBLANES:
            pltpu.store(sc_ref, _nt(x, x), mask=masks[lvl])
            continue
        blocks = rows // (2 * h)
        upper = x32.reshape(blocks, 2 * h, HEAD_DIM)[:, h:].reshape(rows // 2, HEAD_DIM).astype(BF16)
        s = _nt(upper, x)
        for b in range(blocks):
            pltpu.store(sc_ref.at[b * 2 * h + h:(b + 1) * 2 * h, :], s[b * h:(b + 1) * h], mask=masks[lvl][b])
    return sc_ref[...]


def _window_sums(e, w, axis, hist, rows):
    s, d = e, 1
    while d < min(w, SUBLANES):
        s = s + pltpu.roll(s, d, axis)
        d *= 2
    take = lambda a, lo: lax.slice_in_dim(a, lo, lo + rows, axis=axis)
    out = take(s, hist)
    if w > SUBLANES:
        assert w == 2 * SUBLANES
        out = out + take(s, hist - SUBLANES)
    return out


def _head_output(o, j, z_ref, hgn_ref, mix_ref):
    c0 = j * HEAD_DIM
    zg = z_ref[:, POOL_WIDTH + 3 * HG_WIDTH + c0:POOL_WIDTH + 3 * HG_WIDTH + c0 + HEAD_DIM]
    o = _rms(o, hgn_ref[:, c0:c0 + HEAD_DIM]) * (zg * _sigmoid(zg))
    mix_ref[:, POOL_WIDTH + c0:POOL_WIDTH + c0 + HEAD_DIM] = o.astype(BF16)


PROMPT_ROWS = 128
PROMPT_LEVELS = 7


def _mixer_prompt_body(z_ref, lbp_ref, wpool_ref, pscale_ref, hgn_ref,
                       mix_ref, pooln_ref, snew_ref,
                       ext_ref, s_ref, nat_ref, qk_ref, lev_ref, sc_ref, *, layer):
    rows, levels = PROMPT_ROWS, PROMPT_LEVELS
    hist = HIST_ROWS
    t = pl.program_id(1)
    last = pl.num_programs(1) - 1

    @pl.when((pl.program_id(0) == 0) & (t == 0))
    def _():
        sc_ref[...] = jnp.zeros_like(sc_ref)

    @pl.when(t == 0)
    def _():
        ext_ref[0:hist, :] = jnp.zeros((hist, POOL_WIDTH), F32)
        s_ref[...] = jnp.zeros_like(s_ref)

    ext_ref[hist:hist + rows, :] = z_ref[:, 0:POOL_WIDTH]
    pos = t * rows + lax.broadcasted_iota(jnp.int32, (rows, 1), 0)
    for gi, w in enumerate(POOL_WINDOWS):
        c0 = gi * POOL_GROUP_DIM
        e = ext_ref[:, c0:c0 + POOL_GROUP_DIM]
        cnt = jnp.minimum(pos + 1, w).astype(F32)
        _pool_project(_window_sums(e, w, 0, hist, rows), cnt, e[hist:hist + rows], gi,
                      wpool_ref, pscale_ref, mix_ref)

    @pl.when(t == last)
    def _():
        pooln_ref[0] = ext_ref[hist + rows - POOL_CACHE:hist + rows, :]

    ext_ref[0:hist, :] = ext_ref[rows:rows + hist, :]

    lb = _lower_bound(lbp_ref[...], layer)
    p = _level_operands(z_ref, lb, rows, levels, nat_ref, qk_ref, lev_ref)
    decay_t = jnp.transpose(p[rows - 1])
    masks, eye = _level_masks(rows, levels)
    for j in range(HEADS):
        c0 = j * HEAD_DIM
        scores = _intra_scores(j, rows, levels, masks, eye, qk_ref, lev_ref, sc_ref)
        v = z_ref[:, POOL_WIDTH + 2 * HG_WIDTH + c0:POOL_WIDTH + 2 * HG_WIDTH + c0 + HEAD_DIM].astype(BF16)
        qt = _head_rows(lev_ref, levels, j, rows).astype(BF16)
        kt = _head_rows(lev_ref, levels + 1, j, rows).astype(BF16)
        s_in = s_ref[j]
        o = (jnp.dot(scores.astype(BF16), v, preferred_element_type=F32)
             + jnp.dot(qt, s_in.astype(BF16), preferred_element_type=F32))
        s_ref[j] = decay_t[:, j:j + 1] * s_in + lax.dot_general(
            kt, v, (((0,), (0,)), ((), ())), preferred_element_type=F32)
        _head_output(o, j, z_ref, hgn_ref, mix_ref)

    @pl.when(t == last)
    def _():
        snew_ref[0] = s_ref[...]


def _mixer_prompt(z, lbp, wpool, pscale, hgn, batch, seq, layer):
    rows = PROMPT_ROWS
    nt = seq // rows
    hist = HIST_ROWS
    const2 = lambda b, t: (0, 0)
    return pl.pallas_call(
        functools.partial(_mixer_prompt_body, layer=layer),
        grid=(batch, nt),
        in_specs=[
            pl.BlockSpec((rows, IN_COLS), lambda b, t: (b * nt + t, 0)),
            pl.BlockSpec(lbp.shape, const2),
            pl.BlockSpec(wpool.shape, lambda b, t: (0, 0, 0)),
            pl.BlockSpec((1, POOL_WIDTH), const2),
            pl.BlockSpec((1, HG_WIDTH), const2),
        ],
        out_specs=[
            pl.BlockSpec((rows, D_MODEL), lambda b, t: (b * nt + t, 0)),
            pl.BlockSpec((1, POOL_CACHE, POOL_WIDTH), lambda b, t: (b, 0, 0)),
            pl.BlockSpec((1, HEADS, HEAD_DIM, HEAD_DIM), lambda b, t: (b, 0, 0, 0)),
        ],
        out_shape=[
            jax.ShapeDtypeStruct((batch * seq, D_MODEL), BF16),
            jax.ShapeDtypeStruct((batch, POOL_CACHE, POOL_WIDTH), F32),
            jax.ShapeDtypeStruct((batch, HEADS, HEAD_DIM, HEAD_DIM), F32),
        ],
        scratch_shapes=[
            pltpu.VMEM((hist + rows, POOL_WIDTH), F32),
            pltpu.VMEM((HEADS, HEAD_DIM, HEAD_DIM), F32),
            *_level_scratch(rows, PROMPT_LEVELS),
        ],
        compiler_params=_cparams(("arbitrary", "arbitrary"), 48),
        name="mixer_prompt",
    )(z, lbp, wpool, pscale, hgn)


def _mixer_sample_body(z_ref, cache_ref, sin_ref, lbp_ref, wpool_ref, pscale_ref, hgn_ref,
                       mix_ref, pooln_ref, snew_ref,
                       ext_ref, nat_ref, qk_ref, lev_ref, sc_ref, *, layer, nseq, steps, pos0):
    rows = nseq * steps
    levels = steps.bit_length() - 1
    hist = HIST_ROWS

    @pl.when(pl.program_id(0) == 0)
    def _():
        sc_ref[...] = jnp.zeros_like(sc_ref)

    ext_ref[:, 0:hist - POOL_CACHE, :] = jnp.zeros((nseq, hist - POOL_CACHE, POOL_WIDTH), F32)
    ext_ref[:, hist - POOL_CACHE:hist, :] = cache_ref[...]
    ext_ref[:, hist:hist + steps, :] = z_ref[:, 0:POOL_WIDTH].reshape(nseq, steps, POOL_WIDTH)
    pos = pos0 + lax.broadcasted_iota(jnp.int32, (1, steps, 1), 1)
    for gi, w in enumerate(POOL_WINDOWS):
        c0 = gi * POOL_GROUP_DIM
        e = ext_ref[:, :, c0:c0 + POOL_GROUP_DIM]
        cnt = jnp.minimum(pos + 1, w).astype(F32)
        mean = (_window_sums(e, w, 1, hist, steps) / cnt).reshape(rows, POOL_GROUP_DIM)
        u = e[:, hist:hist + steps].reshape(rows, POOL_GROUP_DIM)
        _pool_project(mean, 1.0, u, gi, wpool_ref, pscale_ref, mix_ref)
    pooln_ref[...] = ext_ref[:, hist + steps - POOL_CACHE:hist + steps, :]

    lb = _lower_bound(lbp_ref[...], layer)
    p = _level_operands(z_ref, lb, rows, levels, nat_ref, qk_ref, lev_ref)
    decay = p.reshape(nseq, steps, HEADS, HEAD_DIM)[:, steps - 1]
    decay_t = jnp.swapaxes(decay, 1, 2)
    masks, eye = _level_masks(rows, levels)
    for j in range(HEADS):
        c0 = j * HEAD_DIM
        scores = _intra_scores(j, rows, levels, masks, eye, qk_ref, lev_ref, sc_ref)
        v32 = z_ref[:, POOL_WIDTH + 2 * HG_WIDTH + c0:POOL_WIDTH + 2 * HG_WIDTH + c0 + HEAD_DIM]
        v = v32.astype(BF16)
        v3 = v32.reshape(nseq, steps, HEAD_DIM).astype(BF16)
        qt3 = _head_rows(lev_ref, levels, j, rows).reshape(nseq, steps, HEAD_DIM).astype(BF16)
        kt3 = _head_rows(lev_ref, levels + 1, j, rows).reshape(nseq, steps, HEAD_DIM).astype(BF16)
        s_in = sin_ref[:, j]
        o_state = lax.dot_general(qt3, s_in.astype(BF16), (((2,), (1,)), ((0,), (0,))),
                                  preferred_element_type=F32)
        o = jnp.dot(scores.astype(BF16), v, preferred_element_type=F32) + o_state.reshape(rows, HEAD_DIM)
        upd = lax.dot_general(kt3, v3, (((1,), (1,)), ((0,), (0,))), preferred_element_type=F32)
        snew_ref[:, j] = decay_t[:, :, j:j + 1] * s_in + upd
        _head_output(o, j, z_ref, hgn_ref, mix_ref)


def _mixer_sample(z, cache, state, lbp, wpool, pscale, hgn, nseq_total, steps, pos0, layer, nseq):
    rows = nseq * steps
    levels = steps.bit_length() - 1
    assert 1 << levels == steps and nseq_total % nseq == 0
    hist = HIST_ROWS
    const2 = lambda i: (0, 0)
    return pl.pallas_call(
        functools.partial(_mixer_sample_body, layer=layer, nseq=nseq, steps=steps, pos0=pos0),
        grid=(nseq_total // nseq,),
        in_specs=[
            pl.BlockSpec((rows, IN_COLS), lambda i: (i, 0)),
            pl.BlockSpec((nseq, POOL_CACHE, POOL_WIDTH), lambda i: (i, 0, 0)),
            pl.BlockSpec((nseq, HEADS, HEAD_DIM, HEAD_DIM), lambda i: (i, 0, 0, 0)),
            pl.BlockSpec(lbp.shape, const2),
            pl.BlockSpec(wpool.shape, lambda i: (0, 0, 0)),
            pl.BlockSpec((1, POOL_WIDTH), const2),
            pl.BlockSpec((1, HG_WIDTH), const2),
        ],
        out_specs=[
            pl.BlockSpec((rows, D_MODEL), lambda i: (i, 0)),
            pl.BlockSpec((nseq, POOL_CACHE, POOL_WIDTH), lambda i: (i, 0, 0)),
            pl.BlockSpec((nseq, HEADS, HEAD_DIM, HEAD_DIM), lambda i: (i, 0, 0, 0)),
        ],
        out_shape=[
            jax.ShapeDtypeStruct((nseq_total * steps, D_MODEL), BF16),
            jax.ShapeDtypeStruct((nseq_total, POOL_CACHE, POOL_WIDTH), F32),
            jax.ShapeDtypeStruct((nseq_total, HEADS, HEAD_DIM, HEAD_DIM), F32),
        ],
        scratch_shapes=[
            pltpu.VMEM((nseq, hist + steps, POOL_WIDTH), F32),
            *_level_scratch(rows, levels),
        ],
        compiler_params=_cparams(("arbitrary",), 48),
        name="mixer_sample",
    )(z, cache, state, lbp, wpool, pscale, hgn)


def _out_proj_body(x_ref, mix_ref, w_ref, o_ref):
    o_ref[...] = x_ref[...] + jnp.dot(mix_ref[...], w_ref[...], preferred_element_type=F32)


def _out_proj(x, mix, w, tm):
    m = x.shape[0]
    return pl.pallas_call(
        _out_proj_body,
        grid=(m // tm,),
        in_specs=[
            pl.BlockSpec((tm, D_MODEL), lambda i: (i, 0)),
            pl.BlockSpec((tm, D_MODEL), lambda i: (i, 0)),
            pl.BlockSpec((D_MODEL, D_MODEL), lambda i: (0, 0)),
        ],
        out_specs=pl.BlockSpec((tm, D_MODEL), lambda i: (i, 0)),
        out_shape=jax.ShapeDtypeStruct((m, D_MODEL), F32),
        compiler_params=_cparams(("arbitrary",), 48),
        name="out_proj",
    )(x, mix, w)


def _ffn_body(x_ref, g2_ref, wg_ref, wu_ref, wd_ref, gf_ref, y_ref, h_ref):
    j = pl.program_id(1)

    @pl.when(j == 0)
    def _():
        h_ref[...] = _rms(x_ref[...], g2_ref[...]).astype(BF16)
        y_ref[...] = jnp.zeros_like(y_ref)

    h = h_ref[...]
    gate = jnp.dot(h, wg_ref[...].astype(BF16), preferred_element_type=F32)
    up = jnp.dot(h, wu_ref[...].astype(BF16), preferred_element_type=F32)
    act = (gate * _sigmoid(gate) * up).astype(BF16)
    y_ref[...] += jnp.dot(act, wd_ref[...].astype(BF16), preferred_element_type=F32)

    @pl.when(j == pl.num_programs(1) - 1)
    def _():
        y_ref[...] = _rms(x_ref[...] + y_ref[...], gf_ref[...])


def _ffn(x, g2, wg, wu, wd, gf, tm, tf):
    m = x.shape[0]
    d_ff = wg.shape[1]
    return pl.pallas_call(
        _ffn_body,
        grid=(m // tm, d_ff // tf),
        in_specs=[
            pl.BlockSpec((tm, D_MODEL), lambda i, j: (i, 0)),
            pl.BlockSpec((1, D_MODEL), lambda i, j: (0, 0)),
            pl.BlockSpec((D_MODEL, tf), lambda i, j: (0, j)),
            pl.BlockSpec((D_MODEL, tf), lambda i, j: (0, j)),
            pl.BlockSpec((tf, D_MODEL), lambda i, j: (j, 0)),
            pl.BlockSpec((1, D_MODEL), lambda i, j: (0, 0)),
        ],
        out_specs=pl.BlockSpec((tm, D_MODEL), lambda i, j: (i, 0)),
        out_shape=jax.ShapeDtypeStruct((m, D_MODEL), F32),
        scratch_shapes=[pltpu.VMEM((tm, D_MODEL), BF16)],
        compiler_params=_cparams(("arbitrary", "arbitrary"), 58),
        name="ffn",
    )(x, g2, wg, wu, wd, gf)


def _token_stages(x2, mixer, w):
    z = _in_proj(x2, w["norm1"], w["w_in"], tm=1024, tn=1024)
    mix, pool_new, s_new = mixer(z)
    x1 = _out_proj(x2, mix, w["w_o"], tm=512)
    y = _ffn(x1, w["norm2"], w["w_gate"], w["w_up"], w["w_down"], w["norm_f"], tm=1024, tf=256)
    return y, pool_new, s_new


def kernel(x_prompt, x_sample, cache_pool, state_hgrn, lb_param, norm1, w_in, w_pool, pool_scale,
           hg_norm, w_o, norm2, w_gate, w_up, w_down, norm_f):
    depth = w_in.shape[0]
    assert depth == 1, "single-layer trunk"
    layer = 0
    batch, seq, _ = x_prompt.shape
    dec_batch, dec_seq, _ = x_sample.shape

    row = lambda a: a.reshape(1, -1).astype(F32)
    w = dict(
        norm1=row(norm1[layer]), w_in=w_in[layer].astype(BF16),
        w_o=w_o[layer].astype(BF16), norm2=row(norm2[layer]),
        w_gate=w_gate[layer], w_up=w_up[layer], w_down=w_down[layer], norm_f=row(norm_f),
    )
    lbp = lb_param.astype(F32)
    wpool = w_pool[layer].astype(BF16)
    pscale = row(pool_scale[layer])
    hgn = row(hg_norm[layer])

    mixer_p = lambda z: _mixer_prompt(z, lbp, wpool, pscale, hgn, batch, seq, layer)
    y_p, pool_p, s_p = _token_stages(x_prompt.reshape(batch * seq, D_MODEL), mixer_p, w)

    mixer_s = lambda z: _mixer_sample(z, cache_pool[layer], state_hgrn[layer], lbp, wpool, pscale, hgn,
                                      dec_batch, dec_seq, PAST_LEN, layer, nseq=8)
    y_s, pool_s, s_s = _token_stages(x_sample.reshape(dec_batch * dec_seq, D_MODEL), mixer_s, w)

    return (y_p.reshape(batch, seq, D_MODEL), y_s.reshape(dec_batch, dec_seq, D_MODEL),
            pool_p[None], s_p[None], pool_s[None], s_s[None])
```

```python
import functools

import jax
import jax.numpy as jnp
from jax import lax
from jax.experimental import pallas as pl
from jax.experimental.pallas import tpu as pltpu

D_MODEL = 2048
POOL_WIDTH = 1024
POOL_WINDOWS = (2, 4, 8, 16)
POOL_GROUP_DIM = POOL_WIDTH // len(POOL_WINDOWS)
POOL_CACHE = max(POOL_WINDOWS) - 1
HG_WIDTH = 1024
HEAD_DIM = 128
HEADS = HG_WIDTH // HEAD_DIM
IN_COLS = POOL_WIDTH + 4 * HG_WIDTH
Z_Q, Z_F, Z_I, Z_G = (POOL_WIDTH + n * HG_WIDTH for n in range(4))
EPS = 1e-6
SUBLANES = 8
assert HEADS == SUBLANES
HIST_ROWS = 2 * SUBLANES
PAST_LEN = 16384

F32 = jnp.float32
BF16 = jnp.bfloat16

MIB = 1024 * 1024


def _cparams(sem, vmem_mib):
    return pltpu.CompilerParams(dimension_semantics=sem, vmem_limit_bytes=vmem_mib * MIB)


def _rms(x, g):
    return x * lax.rsqrt(jnp.mean(x * x, axis=-1, keepdims=True) + EPS) * g


def _sigmoid(x):
    return 1.0 / (1.0 + jnp.exp(-x))


ROW_CHUNK = 256


def _row_chunks(n):
    return [slice(r, r + ROW_CHUNK) for r in range(0, n, ROW_CHUNK)]


def _in_proj_body(x_ref, g_ref, w_ref, z_ref, h_ref):
    j = pl.program_id(1)

    @pl.when(j == 0)
    def _():
        w = w_ref[...]
        for rs in _row_chunks(x_ref.shape[0]):
            h = _rms(x_ref[rs, :], g_ref[...]).astype(BF16)
            h_ref[rs, :] = h
            z_ref[rs, :] = jnp.dot(h, w, preferred_element_type=F32)

    @pl.when(j > 0)
    def _():
        z_ref[...] = jnp.dot(h_ref[...], w_ref[...], preferred_element_type=F32)


def _in_proj(x, g, w, tm, tn):
    m = x.shape[0]
    return pl.pallas_call(
        _in_proj_body,
        grid=(m // tm, IN_COLS // tn),
        in_specs=[
            pl.BlockSpec((tm, D_MODEL), lambda i, j: (i, 0)),
            pl.BlockSpec((1, D_MODEL), lambda i, j: (0, 0)),
            pl.BlockSpec((D_MODEL, tn), lambda i, j: (0, j)),
        ],
        out_specs=pl.BlockSpec((tm, tn), lambda i, j: (i, j)),
        out_shape=jax.ShapeDtypeStruct((m, IN_COLS), F32),
        scratch_shapes=[pltpu.VMEM((tm, D_MODEL), BF16)],
        compiler_params=_cparams(("arbitrary", "arbitrary"), 48),
        name="in_proj",
    )(x, g, w)


def _lower_bound(lbp, layer):
    e = jnp.exp(lbp - jnp.max(lbp, axis=0, keepdims=True))
    return jnp.sum(e[: layer + 1], axis=0, keepdims=True) / jnp.sum(e, axis=0, keepdims=True)


def _pool_project(acc, cnt, u, gi, wpool_ref, pscale_ref, mix_ref):
    c0 = gi * POOL_GROUP_DIM
    pooled = acc / cnt - u
    out = jnp.dot(pooled.astype(BF16), wpool_ref[gi], preferred_element_type=F32)
    mix_ref[:, c0:c0 + POOL_GROUP_DIM] = (out * pscale_ref[:, c0:c0 + POOL_GROUP_DIM]).astype(BF16)


def _token_tile(nat_ref, a, t):
    return nat_ref[a, t // SUBLANES, pl.ds(t % SUBLANES, HEADS, stride=SUBLANES), :]


def _head_rows(lev_ref, lvl, j, rows):
    return lev_ref[lvl, pl.ds(j, rows, stride=HEADS), :]


def _head_gates(zq, zf, lb, c, rows, nat_ref, qk_ref):
    fg = lb + (1.0 - lb) * _sigmoid(zf)
    qs = zq * _sigmoid(zq)
    kk = 1.0 - fg
    qk_ref[0, :, c * HEAD_DIM:(c + 1) * HEAD_DIM] = qs.astype(BF16)
    qk_ref[1, :, c * HEAD_DIM:(c + 1) * HEAD_DIM] = kk.astype(BF16)
    for a, x in enumerate((qs, kk, fg)):
        nat_ref[a, :, c * SUBLANES:(c + 1) * SUBLANES, :] = x.reshape(rows // SUBLANES, SUBLANES, HEAD_DIM)


def _segment_offsets(rows, levels):
    off = [0]
    for lvl in range(levels):
        off.append(off[-1] + (rows >> lvl))
    return off


def _segment_products(rows, levels, nat_ref, seg_ref):
    off = _segment_offsets(rows, levels)
    for t in range(rows):
        seg_ref[t] = _token_tile(nat_ref, 2, t)
    for lvl in range(1, levels):
        for m in range(rows >> lvl):
            seg_ref[off[lvl] + m] = seg_ref[off[lvl - 1] + 2 * m] * seg_ref[off[lvl - 1] + 2 * m + 1]


def _token_levels(tokens, rows, levels, nat_ref, seg_ref, lev_ref):
    off = _segment_offsets(rows, levels)
    prefix = []
    for t in tokens:
        tok = slice(t * HEADS, (t + 1) * HEADS)
        q_t, k_t = _token_tile(nat_ref, 0, t), _token_tile(nat_ref, 1, t)
        p_t, r_t = seg_ref[t], None
        for lvl in range(levels):
            m = t >> lvl
            sibling = seg_ref[off[lvl] + (m ^ 1)]
            if m & 1:
                lev_ref[lvl, tok, :] = q_t * p_t
                p_t = p_t * sibling
            else:
                lev_ref[lvl, tok, :] = k_t if r_t is None else k_t * r_t
                r_t = sibling if r_t is None else r_t * sibling
        lev_ref[levels, tok, :] = q_t * p_t
        lev_ref[levels + 1, tok, :] = k_t if r_t is None else k_t * r_t
        prefix.append(p_t)
    return prefix


def _level_scratch(rows, levels):
    return [
        pltpu.VMEM((3, rows // SUBLANES, HEADS * SUBLANES, HEAD_DIM), F32),
        pltpu.VMEM((2 * rows, HEADS, HEAD_DIM), F32),
        pltpu.VMEM((2, rows, HG_WIDTH), BF16),
        pltpu.VMEM((levels + 2, rows * HEADS, HEAD_DIM), F32),
        pltpu.VMEM((HEADS, rows, rows), F32),
    ]


def _level_masks(rows, levels):
    ti = lax.broadcasted_iota(jnp.int32, (rows, rows), 0)
    si = lax.broadcasted_iota(jnp.int32, (rows, rows), 1)
    x = ti ^ si
    masks = []
    for lvl in range(levels):
        h = 1 << lvl
        if h < SUBLANES:
            masks.append(((x >> lvl) == 1) & (((ti >> lvl) & 1) == 1))
        else:
            half = lax.broadcasted_iota(jnp.int32, (h, rows), 1) >> lvl
            masks.append([half == 2 * b for b in range(rows // (2 * h))])
    return masks, ti == si


def _nt(a, b):
    return lax.dot_general(a, b, (((1,), (1,)), ((), ())), preferred_element_type=F32)


def _intra_scores(j, rows, levels, masks, eye, qk_ref, lev_ref, sc_ref):
    c0 = j * HEAD_DIM
    sc_ref = sc_ref.at[j]
    pltpu.store(sc_ref, _nt(qk_ref[0, :, c0:c0 + HEAD_DIM], qk_ref[1, :, c0:c0 + HEAD_DIM]), mask=eye)
    for lvl in range(levels):
        h = 1 << lvl
        x32 = _head_rows(lev_ref, lvl, j, rows)
        x = x32.astype(BF16)
        if h < SUBLANES:
            pltpu.store(sc_ref, _nt(x, x), mask=masks[lvl])
            continue
        blocks = rows // (2 * h)
        upper = x32.reshape(blocks, 2 * h, HEAD_DIM)[:, h:].reshape(rows // 2, HEAD_DIM).astype(BF16)
        s = _nt(upper, x)
        for b in range(blocks):
            pltpu.store(sc_ref.at[b * 2 * h + h:(b + 1) * 2 * h, :], s[b * h:(b + 1) * h], mask=masks[lvl][b])
    return sc_ref[...]


def _window_sums(e, w, axis, hist, rows):
    s, d = e, 1
    while d < min(w, SUBLANES):
        s = s + pltpu.roll(s, d, axis)
        d *= 2
    take = lambda a, lo: lax.slice_in_dim(a, lo, lo + rows, axis=axis)
    out = take(s, hist)
    if w > SUBLANES:
        assert w == 2 * SUBLANES
        out = out + take(s, hist - SUBLANES)
    return out


def _head_output(o, j, zg, hgn_ref, mix_ref):
    c0 = j * HEAD_DIM
    o = _rms(o, hgn_ref[:, c0:c0 + HEAD_DIM]) * (zg * _sigmoid(zg))
    mix_ref[:, POOL_WIDTH + c0:POOL_WIDTH + c0 + HEAD_DIM] = o.astype(BF16)


PROMPT_ROWS = 128
PROMPT_LEVELS = 7


PROMPT_SLOTS = 2


def _prompt_prepare(z_blk, t_blk, starts_seq, slot, lb, wpool_ref, pscale_ref,
                    ext_ref, pool_ref, decay_ref, nat_ref, seg_ref, qk_ref, lev_ref):
    rows, levels, hist = PROMPT_ROWS, PROMPT_LEVELS, HIST_ROWS
    nat, seg, qk, lev = nat_ref[slot], seg_ref[slot], qk_ref[slot], lev_ref[slot]

    def pool_group(gi):
        def run():
            if gi == 0:
                ext_ref[0:hist, :] = jnp.where(starts_seq, 0.0, ext_ref[rows:rows + hist, :])
                ext_ref[hist:hist + rows, :] = z_blk[:, 0:POOL_WIDTH]
            w, c0 = POOL_WINDOWS[gi], gi * POOL_GROUP_DIM
            e = ext_ref[:, c0:c0 + POOL_GROUP_DIM]
            pos = t_blk * rows + lax.broadcasted_iota(jnp.int32, (rows, 1), 0)
            cnt = jnp.minimum(pos + 1, w).astype(F32)
            _pool_project(_window_sums(e, w, 0, hist, rows), cnt, e[hist:hist + rows], gi,
                          wpool_ref, pscale_ref, pool_ref[slot])
        return run

    def gates(c):
        cols = slice(c * HEAD_DIM, (c + 1) * HEAD_DIM)
        return lambda: _head_gates(z_blk[:, Z_Q + c * HEAD_DIM:Z_Q + (c + 1) * HEAD_DIM],
                                   z_blk[:, Z_F + c * HEAD_DIM:Z_F + (c + 1) * HEAD_DIM], lb[:, cols], c, rows, nat, qk)

    def tokens(k, n):
        def run():
            prefix = _token_levels(range(k * n, (k + 1) * n), rows, levels, nat, seg, lev)
            if (k + 1) * n == rows:
                decay_ref[slot][...] = prefix[-1]
        return run

    a, b = [pool_group(g) for g in range(len(POOL_WINDOWS))], [gates(c) for c in range(HEADS)]
    t = [tokens(k, rows // HEADS) for k in range(HEADS)]
    segments = lambda: _segment_products(rows, levels, nat, seg)
    return [[a[0], a[1], b[0]], [a[2], a[3], b[1]], b[2:5], b[5:8], [segments, t[0]], t[1:3], t[3:6], t[6:8]]


def _prompt_heads(slot, z_blk, mix_blk, hgn_ref, s_ref, pool_ref, decay_ref, qk_ref, lev_ref, sc_ref,
                  masks, eye):
    rows, levels = PROMPT_ROWS, PROMPT_LEVELS

    shared = {}

    def head(j):
        c0 = j * HEAD_DIM
        if j == 0:
            mix_blk[:, 0:POOL_WIDTH] = pool_ref[slot][...]
            shared["decay_t"] = jnp.transpose(decay_ref[slot][...])
        decay_t = shared["decay_t"]
        scores = _intra_scores(j, rows, levels, masks, eye, qk_ref[slot], lev_ref[slot], sc_ref[slot])
        v = z_blk[:, Z_I + c0:Z_I + c0 + HEAD_DIM].astype(BF16)
        qt = _head_rows(lev_ref[slot], levels, j, rows).astype(BF16)
        kt = _head_rows(lev_ref[slot], levels + 1, j, rows).astype(BF16)
        s_in = s_ref[j]
        o = (jnp.dot(scores.astype(BF16), v, preferred_element_type=F32)
             + jnp.dot(qt, s_in.astype(BF16), preferred_element_type=F32))
        s_ref[j] = decay_t[:, j:j + 1] * s_in + lax.dot_general(
            kt, v, (((0,), (0,)), ((), ())), preferred_element_type=F32)
        _head_output(o, j, z_blk[:, Z_G + c0:Z_G + c0 + HEAD_DIM], hgn_ref, mix_blk)

    return [functools.partial(head, j) for j in range(HEADS)]


def _mixer_prompt_body(zc_ref, zn_ref, lbp_ref, wpool_ref, pscale_ref, hgn_ref,
                       mix_ref, pooln_ref, snew_ref,
                       ext_ref, s_ref, *slot_refs, layer, nt):
    rows, hist = PROMPT_ROWS, HIST_ROWS
    per_slot = len(slot_refs) // PROMPT_SLOTS
    pool_ref, decay_ref, nat_ref, seg_ref, qk_ref, lev_ref, sc_ref = (
        [slot_refs[s * per_slot + n] for s in range(PROMPT_SLOTS)] for n in range(per_slot))
    i = pl.program_id(0)
    steps_per_seq = nt // PROMPT_SLOTS
    tt = i % steps_per_seq
    ends_seq = tt == steps_per_seq - 1
    lb = _lower_bound(lbp_ref[...], layer)
    prepare = functools.partial(
        _prompt_prepare, lb=lb, wpool_ref=wpool_ref, pscale_ref=pscale_ref, ext_ref=ext_ref, pool_ref=pool_ref,
        decay_ref=decay_ref, nat_ref=nat_ref, seg_ref=seg_ref, qk_ref=qk_ref, lev_ref=lev_ref)

    @pl.when(i == 0)
    def _():
        for sc in sc_ref:
            sc[...] = jnp.zeros_like(sc)
        ext_ref[...] = jnp.zeros_like(ext_ref)
        for batch in prepare(zc_ref.at[0:rows], 0, True, 0):
            for item in batch:
                item()

    @pl.when(tt == 0)
    def _():
        s_ref[...] = jnp.zeros_like(s_ref)

    masks, eye = _level_masks(rows, PROMPT_LEVELS)
    heads = functools.partial(
        _prompt_heads, hgn_ref=hgn_ref, s_ref=s_ref, pool_ref=pool_ref, decay_ref=decay_ref, qk_ref=qk_ref,
        lev_ref=lev_ref, sc_ref=sc_ref, masks=masks, eye=eye)

    def interleave(head_items, batches):
        for head, batch in zip(head_items, batches, strict=True):
            head()
            for item in batch:
                item()

    interleave(heads(0, zc_ref.at[0:rows], mix_ref.at[0:rows]),
               prepare(zc_ref.at[rows:2 * rows], PROMPT_SLOTS * tt + 1, False, 1))
    pooln_ref[0] = ext_ref[hist + rows - POOL_CACHE:hist + rows, :]
    interleave(heads(1, zc_ref.at[rows:2 * rows], mix_ref.at[rows:2 * rows]),
               prepare(zn_ref, jnp.where(ends_seq, 0, PROMPT_SLOTS * tt + 2), ends_seq, 0))

    @pl.when(ends_seq)
    def _():
        snew_ref[0] = s_ref[...]


def _mixer_prompt(z, lbp, wpool, pscale, hgn, batch, seq, layer):
    rows, slots = PROMPT_ROWS, PROMPT_SLOTS
    nt = seq // rows
    assert seq % (rows * slots) == 0
    nblk = batch * nt
    steps_per_seq = nt // slots
    const2 = lambda i: (0, 0)
    return pl.pallas_call(
        functools.partial(_mixer_prompt_body, layer=layer, nt=nt),
        grid=(nblk // slots,),
        in_specs=[
            pl.BlockSpec((slots * rows, IN_COLS), lambda i: (i, 0)),
            pl.BlockSpec((rows, Z_I), lambda i: (jnp.minimum(slots * i + slots, nblk - 1), 0)),
            pl.BlockSpec(lbp.shape, const2),
            pl.BlockSpec(wpool.shape, lambda i: (0, 0, 0)),
            pl.BlockSpec((1, POOL_WIDTH), const2),
            pl.BlockSpec((1, HG_WIDTH), const2),
        ],
        out_specs=[
            pl.BlockSpec((slots * rows, D_MODEL), lambda i: (i, 0)),
            pl.BlockSpec((1, POOL_CACHE, POOL_WIDTH), lambda i: (i // steps_per_seq, 0, 0)),
            pl.BlockSpec((1, HEADS, HEAD_DIM, HEAD_DIM), lambda i: (i // steps_per_seq, 0, 0, 0)),
        ],
        out_shape=[
            jax.ShapeDtypeStruct((batch * seq, D_MODEL), BF16),
            jax.ShapeDtypeStruct((batch, POOL_CACHE, POOL_WIDTH), F32),
            jax.ShapeDtypeStruct((batch, HEADS, HEAD_DIM, HEAD_DIM), F32),
        ],
        scratch_shapes=[
            pltpu.VMEM((HIST_ROWS + rows, POOL_WIDTH), F32),
            pltpu.VMEM((HEADS, HEAD_DIM, HEAD_DIM), F32),
            *(slots * [pltpu.VMEM((rows, POOL_WIDTH), BF16), pltpu.VMEM((HEADS, HEAD_DIM), F32),
                       *_level_scratch(rows, PROMPT_LEVELS)]),
        ],
        compiler_params=_cparams(("arbitrary",), 48),
        name="mixer_prompt",
    )(z, z, lbp, wpool, pscale, hgn)


def _mixer_sample_body(z_ref, cache_ref, sin_ref, lbp_ref, wpool_ref, pscale_ref, hgn_ref,
                       mix_ref, pooln_ref, snew_ref,
                       ext_ref, nat_ref, seg_ref, qk_ref, lev_ref, sc_ref, *, layer, nseq, steps, pos0):
    rows = nseq * steps
    levels = steps.bit_length() - 1
    hist = HIST_ROWS

    @pl.when(pl.program_id(0) == 0)
    def _():
        sc_ref[...] = jnp.zeros_like(sc_ref)

    ext_ref[:, 0:hist - POOL_CACHE, :] = jnp.zeros((nseq, hist - POOL_CACHE, POOL_WIDTH), F32)
    ext_ref[:, hist - POOL_CACHE:hist, :] = cache_ref[...]
    ext_ref[:, hist:hist + steps, :] = z_ref[:, 0:POOL_WIDTH].reshape(nseq, steps, POOL_WIDTH)
    pos = pos0 + lax.broadcasted_iota(jnp.int32, (1, steps, 1), 1)
    for gi, w in enumerate(POOL_WINDOWS):
        c0 = gi * POOL_GROUP_DIM
        e = ext_ref[:, :, c0:c0 + POOL_GROUP_DIM]
        cnt = jnp.minimum(pos + 1, w).astype(F32)
        mean = (_window_sums(e, w, 1, hist, steps) / cnt).reshape(rows, POOL_GROUP_DIM)
        u = e[:, hist:hist + steps].reshape(rows, POOL_GROUP_DIM)
        _pool_project(mean, 1.0, u, gi, wpool_ref, pscale_ref, mix_ref)
    pooln_ref[...] = ext_ref[:, hist + steps - POOL_CACHE:hist + steps, :]

    lb = _lower_bound(lbp_ref[...], layer)
    for c in range(HEADS):
        cols = slice(c * HEAD_DIM, (c + 1) * HEAD_DIM)
        _head_gates(z_ref[:, Z_Q + c * HEAD_DIM:Z_Q + (c + 1) * HEAD_DIM],
                    z_ref[:, Z_F + c * HEAD_DIM:Z_F + (c + 1) * HEAD_DIM], lb[:, cols], c, rows, nat_ref, qk_ref)
    _segment_products(rows, levels, nat_ref, seg_ref)
    prefix = _token_levels(range(rows), rows, levels, nat_ref, seg_ref, lev_ref)
    decay = jnp.stack(prefix[steps - 1::steps])
    decay_t = jnp.swapaxes(decay, 1, 2)
    masks, eye = _level_masks(rows, levels)
    for j in range(HEADS):
        c0 = j * HEAD_DIM
        scores = _intra_scores(j, rows, levels, masks, eye, qk_ref, lev_ref, sc_ref)
        v32 = z_ref[:, Z_I + c0:Z_I + c0 + HEAD_DIM]
        v = v32.astype(BF16)
        v3 = v32.reshape(nseq, steps, HEAD_DIM).astype(BF16)
        qt3 = _head_rows(lev_ref, levels, j, rows).reshape(nseq, steps, HEAD_DIM).astype(BF16)
        kt3 = _head_rows(lev_ref, levels + 1, j, rows).reshape(nseq, steps, HEAD_DIM).astype(BF16)
        s_in = sin_ref[:, j]
        o_state = lax.dot_general(qt3, s_in.astype(BF16), (((2,), (1,)), ((0,), (0,))),
                                  preferred_element_type=F32)
        o = jnp.dot(scores.astype(BF16), v, preferred_element_type=F32) + o_state.reshape(rows, HEAD_DIM)
        upd = lax.dot_general(kt3, v3, (((1,), (1,)), ((0,), (0,))), preferred_element_type=F32)
        snew_ref[:, j] = decay_t[:, :, j:j + 1] * s_in + upd
        _head_output(o, j, z_ref[:, Z_G + c0:Z_G + c0 + HEAD_DIM], hgn_ref, mix_ref)


def _mixer_sample(z, cache, state, lbp, wpool, pscale, hgn, nseq_total, steps, pos0, layer, nseq):
    rows = nseq * steps
    levels = steps.bit_length() - 1
    assert 1 << levels == steps and nseq_total % nseq == 0
    hist = HIST_ROWS
    const2 = lambda i: (0, 0)
    return pl.pallas_call(
        functools.partial(_mixer_sample_body, layer=layer, nseq=nseq, steps=steps, pos0=pos0),
        grid=(nseq_total // nseq,),
        in_specs=[
            pl.BlockSpec((rows, IN_COLS), lambda i: (i, 0)),
            pl.BlockSpec((nseq, POOL_CACHE, POOL_WIDTH), lambda i: (i, 0, 0)),
            pl.BlockSpec((nseq, HEADS, HEAD_DIM, HEAD_DIM), lambda i: (i, 0, 0, 0)),
            pl.BlockSpec(lbp.shape, const2),
            pl.BlockSpec(wpool.shape, lambda i: (0, 0, 0)),
            pl.BlockSpec((1, POOL_WIDTH), const2),
            pl.BlockSpec((1, HG_WIDTH), const2),
        ],
        out_specs=[
            pl.BlockSpec((rows, D_MODEL), lambda i: (i, 0)),
            pl.BlockSpec((nseq, POOL_CACHE, POOL_WIDTH), lambda i: (i, 0, 0)),
            pl.BlockSpec((nseq, HEADS, HEAD_DIM, HEAD_DIM), lambda i: (i, 0, 0, 0)),
        ],
        out_shape=[
            jax.ShapeDtypeStruct((nseq_total * steps, D_MODEL), BF16),
            jax.ShapeDtypeStruct((nseq_total, POOL_CACHE, POOL_WIDTH), F32),
            jax.ShapeDtypeStruct((nseq_total, HEADS, HEAD_DIM, HEAD_DIM), F32),
        ],
        scratch_shapes=[
            pltpu.VMEM((nseq, hist + steps, POOL_WIDTH), F32),
            *_level_scratch(rows, levels),
        ],
        compiler_params=_cparams(("arbitrary",), 48),
        name="mixer_sample",
    )(z, cache, state, lbp, wpool, pscale, hgn)


def _out_proj_body(x_ref, mix_ref, w_ref, o_ref):
    o_ref[...] = x_ref[...] + jnp.dot(mix_ref[...], w_ref[...], preferred_element_type=F32)


def _out_proj(x, mix, w, tm):
    m = x.shape[0]
    return pl.pallas_call(
        _out_proj_body,
        grid=(m // tm,),
        in_specs=[
            pl.BlockSpec((tm, D_MODEL), lambda i: (i, 0)),
            pl.BlockSpec((tm, D_MODEL), lambda i: (i, 0)),
            pl.BlockSpec((D_MODEL, D_MODEL), lambda i: (0, 0)),
        ],
        out_specs=pl.BlockSpec((tm, D_MODEL), lambda i: (i, 0)),
        out_shape=jax.ShapeDtypeStruct((m, D_MODEL), F32),
        compiler_params=_cparams(("arbitrary",), 48),
        name="out_proj",
    )(x, mix, w)


def _ffn_body(x_ref, g2_ref, wg_ref, wu_ref, wd_ref, gf_ref, y_ref, h_ref):
    j = pl.program_id(1)
    last = pl.num_programs(1) - 1
    chunks = _row_chunks(x_ref.shape[0])

    def down(h, wg, wu, wd):
        gate = jnp.dot(h, wg, preferred_element_type=F32)
        up = jnp.dot(h, wu, preferred_element_type=F32)
        act = (gate * _sigmoid(gate) * up).astype(BF16)
        return jnp.dot(act, wd, preferred_element_type=F32)

    weights = lambda: (wg_ref[...].astype(BF16), wu_ref[...].astype(BF16), wd_ref[...].astype(BF16))

    @pl.when(j == 0)
    def _():
        w = weights()
        for rs in chunks:
            h = _rms(x_ref[rs, :], g2_ref[...]).astype(BF16)
            h_ref[rs, :] = h
            y_ref[rs, :] = down(h, *w)

    @pl.when((j > 0) & (j < last))
    def _():
        y_ref[...] += down(h_ref[...], *weights())

    @pl.when(j == last)
    def _():
        w = weights()
        for rs in chunks:
            y_ref[rs, :] = _rms(x_ref[rs, :] + y_ref[rs, :] + down(h_ref[rs, :], *w), gf_ref[...])


def _ffn(x, g2, wg, wu, wd, gf, tm, tf):
    m = x.shape[0]
    d_ff = wg.shape[1]
    return pl.pallas_call(
        _ffn_body,
        grid=(m // tm, d_ff // tf),
        in_specs=[
            pl.BlockSpec((tm, D_MODEL), lambda i, j: (i, 0)),
            pl.BlockSpec((1, D_MODEL), lambda i, j: (0, 0)),
            pl.BlockSpec((D_MODEL, tf), lambda i, j: (0, j)),
            pl.BlockSpec((D_MODEL, tf), lambda i, j: (0, j)),
            pl.BlockSpec((tf, D_MODEL), lambda i, j: (j, 0)),
            pl.BlockSpec((1, D_MODEL), lambda i, j: (0, 0)),
        ],
        out_specs=pl.BlockSpec((tm, D_MODEL), lambda i, j: (i, 0)),
        out_shape=jax.ShapeDtypeStruct((m, D_MODEL), F32),
        scratch_shapes=[pltpu.VMEM((tm, D_MODEL), BF16)],
        compiler_params=_cparams(("arbitrary", "arbitrary"), 58),
        name="ffn",
    )(x, g2, wg, wu, wd, gf)


def _token_stages(x2, mixer, w):
    z = _in_proj(x2, w["norm1"], w["w_in"], tm=1024, tn=1024)
    mix, pool_new, s_new = mixer(z)
    x1 = _out_proj(x2, mix, w["w_o"], tm=512)
    y = _ffn(x1, w["norm2"], w["w_gate"], w["w_up"], w["w_down"], w["norm_f"], tm=1024, tf=256)
    return y, pool_new, s_new


def kernel(x_prompt, x_sample, cache_pool, state_hgrn, lb_param, norm1, w_in, w_pool, pool_scale,
           hg_norm, w_o, norm2, w_gate, w_up, w_down, norm_f):
    depth = w_in.shape[0]
    assert depth == 1, "single-layer trunk"
    layer = 0
    batch, seq, _ = x_prompt.shape
    dec_batch, dec_seq, _ = x_sample.shape

    row = lambda a: a.reshape(1, -1).astype(F32)
    w = dict(
        norm1=row(norm1[layer]), w_in=w_in[layer].astype(BF16),
        w_o=w_o[layer].astype(BF16), norm2=row(norm2[layer]),
        w_gate=w_gate[layer], w_up=w_up[layer], w_down=w_down[layer], norm_f=row(norm_f),
    )
    lbp = lb_param.astype(F32)
    wpool = w_pool[layer].astype(BF16)
    pscale = row(pool_scale[layer])
    hgn = row(hg_norm[layer])

    mixer_p = lambda z: _mixer_prompt(z, lbp, wpool, pscale, hgn, batch, seq, layer)
    y_p, pool_p, s_p = _token_stages(x_prompt.reshape(batch * seq, D_MODEL), mixer_p, w)

    mixer_s = lambda z: _mixer_sample(z, cache_pool[layer], state_hgrn[layer], lbp, wpool, pscale, hgn,
                                      dec_batch, dec_seq, PAST_LEN, layer, nseq=8)
    y_s, pool_s, s_s = _token_stages(x_sample.reshape(dec_batch * dec_seq, D_MODEL), mixer_s, w)

    return (y_p.reshape(batch, seq, D_MODEL), y_s.reshape(dec_batch, dec_seq, D_MODEL),
            pool_p[None], s_p[None], pool_s[None], s_s[None])
```

```python
import functools

import jax
import jax.numpy as jnp
from jax import lax
from jax.experimental import pallas as pl
from jax.experimental.pallas import tpu as pltpu

D_MODEL = 2048
POOL_WIDTH = 1024
POOL_WINDOWS = (2, 4, 8, 16)
POOL_GROUP_DIM = POOL_WIDTH // len(POOL_WINDOWS)
POOL_CACHE = max(POOL_WINDOWS) - 1
HG_WIDTH = 1024
HEAD_DIM = 128
HEADS = HG_WIDTH // HEAD_DIM
IN_COLS = POOL_WIDTH + 4 * HG_WIDTH
Z_Q, Z_F, Z_I, Z_G = (POOL_WIDTH + n * HG_WIDTH for n in range(4))
EPS = 1e-6
SUBLANES = 8
assert HEADS == SUBLANES
HIST_ROWS = 2 * SUBLANES
PAST_LEN = 16384

F32 = jnp.float32
BF16 = jnp.bfloat16

MIB = 1024 * 1024


def _cparams(sem, vmem_mib):
    return pltpu.CompilerParams(dimension_semantics=sem, vmem_limit_bytes=vmem_mib * MIB)


def _rms(x, g):
    return x * lax.rsqrt(jnp.mean(x * x, axis=-1, keepdims=True) + EPS) * g


def _sigmoid(x):
    return 1.0 / (1.0 + jnp.exp(-x))


ROW_CHUNK = 256


def _row_chunks(n):
    return [slice(r, r + ROW_CHUNK) for r in range(0, n, ROW_CHUNK)]


def _in_proj_body(x_ref, g_ref, w_ref, z_ref, h_ref):
    j = pl.program_id(1)

    @pl.when(j == 0)
    def _():
        w = w_ref[...]
        for rs in _row_chunks(x_ref.shape[0]):
            h = _rms(x_ref[rs, :], g_ref[...]).astype(BF16)
            h_ref[rs, :] = h
            z_ref[rs, :] = jnp.dot(h, w, preferred_element_type=F32)

    @pl.when(j > 0)
    def _():
        z_ref[...] = jnp.dot(h_ref[...], w_ref[...], preferred_element_type=F32)


def _in_proj(x, g, w, tm, tn):
    m = x.shape[0]
    return pl.pallas_call(
        _in_proj_body,
        grid=(m // tm, IN_COLS // tn),
        in_specs=[
            pl.BlockSpec((tm, D_MODEL), lambda i, j: (i, 0)),
            pl.BlockSpec((1, D_MODEL), lambda i, j: (0, 0)),
            pl.BlockSpec((D_MODEL, tn), lambda i, j: (0, j)),
        ],
        out_specs=pl.BlockSpec((tm, tn), lambda i, j: (i, j)),
        out_shape=jax.ShapeDtypeStruct((m, IN_COLS), F32),
        scratch_shapes=[pltpu.VMEM((tm, D_MODEL), BF16)],
        compiler_params=_cparams(("arbitrary", "arbitrary"), 48),
        name="in_proj",
    )(x, g, w)


def _lower_bound(lbp, layer):
    e = jnp.exp(lbp - jnp.max(lbp, axis=0, keepdims=True))
    return jnp.sum(e[: layer + 1], axis=0, keepdims=True) / jnp.sum(e, axis=0, keepdims=True)


def _pool_project(acc, cnt, u, gi, wpool_ref, pscale_ref, mix_ref):
    c0 = gi * POOL_GROUP_DIM
    pooled = acc / cnt - u
    out = jnp.dot(pooled.astype(BF16), wpool_ref[gi], preferred_element_type=F32)
    mix_ref[:, c0:c0 + POOL_GROUP_DIM] = (out * pscale_ref[:, c0:c0 + POOL_GROUP_DIM]).astype(BF16)


def _token_tile(nat_ref, a, t):
    return nat_ref[a, t // SUBLANES, pl.ds(t % SUBLANES, HEADS, stride=SUBLANES), :]


def _head_rows(lev_ref, lvl, j, rows):
    return lev_ref[lvl, pl.ds(j, rows, stride=HEADS), :]


def _head_gates(zq, zf, lb, c, rows, nat_ref, qk_ref):
    fg = lb + (1.0 - lb) * _sigmoid(zf)
    qs = zq * _sigmoid(zq)
    kk = 1.0 - fg
    qk_ref[0, :, c * HEAD_DIM:(c + 1) * HEAD_DIM] = qs.astype(BF16)
    qk_ref[1, :, c * HEAD_DIM:(c + 1) * HEAD_DIM] = kk.astype(BF16)
    for a, x in enumerate((qs, kk, fg)):
        nat_ref[a, :, c * SUBLANES:(c + 1) * SUBLANES, :] = x.reshape(rows // SUBLANES, SUBLANES, HEAD_DIM)


def _segment_offsets(rows, levels):
    off = [0]
    for lvl in range(levels):
        off.append(off[-1] + (rows >> lvl))
    return off


def _segment_products(rows, levels, nat_ref, seg_ref):
    off = _segment_offsets(rows, levels)
    for t in range(rows):
        seg_ref[t] = _token_tile(nat_ref, 2, t)
    for lvl in range(1, levels):
        for m in range(rows >> lvl):
            seg_ref[off[lvl] + m] = seg_ref[off[lvl - 1] + 2 * m] * seg_ref[off[lvl - 1] + 2 * m + 1]


def _token_levels(tokens, rows, levels, nat_ref, seg_ref, lev_ref):
    off = _segment_offsets(rows, levels)
    prefix = []
    for t in tokens:
        tok = slice(t * HEADS, (t + 1) * HEADS)
        q_t, k_t = _token_tile(nat_ref, 0, t), _token_tile(nat_ref, 1, t)
        p_t, r_t = seg_ref[t], None
        for lvl in range(levels):
            m = t >> lvl
            sibling = seg_ref[off[lvl] + (m ^ 1)]
            if m & 1:
                lev_ref[lvl, tok, :] = q_t * p_t
                p_t = p_t * sibling
            else:
                lev_ref[lvl, tok, :] = k_t if r_t is None else k_t * r_t
                r_t = sibling if r_t is None else r_t * sibling
        lev_ref[levels, tok, :] = q_t * p_t
        lev_ref[levels + 1, tok, :] = k_t if r_t is None else k_t * r_t
        prefix.append(p_t)
    return prefix


def _level_scratch(rows, levels):
    return [
        pltpu.VMEM((3, rows // SUBLANES, HEADS * SUBLANES, HEAD_DIM), F32),
        pltpu.VMEM((2 * rows, HEADS, HEAD_DIM), F32),
        pltpu.VMEM((2, rows, HG_WIDTH), BF16),
        pltpu.VMEM((levels + 2, rows * HEADS, HEAD_DIM), F32),
        pltpu.VMEM((HEADS, rows, rows), F32),
    ]


def _level_masks(rows, levels):
    ti = lax.broadcasted_iota(jnp.int32, (rows, rows), 0)
    si = lax.broadcasted_iota(jnp.int32, (rows, rows), 1)
    x = ti ^ si
    masks = []
    for lvl in range(levels):
        h = 1 << lvl
        if h < SUBLANES:
            masks.append(((x >> lvl) == 1) & (((ti >> lvl) & 1) == 1))
        else:
            half = lax.broadcasted_iota(jnp.int32, (h, rows), 1) >> lvl
            masks.append([half == 2 * b for b in range(rows // (2 * h))])
    return masks, ti == si


def _nt(a, b):
    return lax.dot_general(a, b, (((1,), (1,)), ((), ())), preferred_element_type=F32)


def _intra_scores(j, rows, levels, masks, eye, qk_ref, lev_ref, sc_ref):
    c0 = j * HEAD_DIM
    sc_ref = sc_ref.at[j]
    pltpu.store(sc_ref, _nt(qk_ref[0, :, c0:c0 + HEAD_DIM], qk_ref[1, :, c0:c0 + HEAD_DIM]), mask=eye)
    for lvl in range(levels):
        h = 1 << lvl
        x32 = _head_rows(lev_ref, lvl, j, rows)
        x = x32.astype(BF16)
        if h < SUBLANES:
            pltpu.store(sc_ref, _nt(x, x), mask=masks[lvl])
            continue
        blocks = rows // (2 * h)
        upper = x32.reshape(blocks, 2 * h, HEAD_DIM)[:, h:].reshape(rows // 2, HEAD_DIM).astype(BF16)
        s = _nt(upper, x)
        for b in range(blocks):
            pltpu.store(sc_ref.at[b * 2 * h + h:(b + 1) * 2 * h, :], s[b * h:(b + 1) * h], mask=masks[lvl][b])
    return sc_ref[...]


def _window_sums(e, w, axis, hist, rows):
    s, d = e, 1
    while d < min(w, SUBLANES):
        s = s + pltpu.roll(s, d, axis)
        d *= 2
    take = lambda a, lo: lax.slice_in_dim(a, lo, lo + rows, axis=axis)
    out = take(s, hist)
    if w > SUBLANES:
        assert w == 2 * SUBLANES
        out = out + take(s, hist - SUBLANES)
    return out


def _head_output(o, j, zg, hgn_ref, mix_ref):
    c0 = j * HEAD_DIM
    o = _rms(o, hgn_ref[:, c0:c0 + HEAD_DIM]) * (zg * _sigmoid(zg))
    mix_ref[:, POOL_WIDTH + c0:POOL_WIDTH + c0 + HEAD_DIM] = o.astype(BF16)


PROMPT_ROWS = 128
PROMPT_LEVELS = 7


PROMPT_SLOTS = 2


def _prompt_prepare(z_blk, t_blk, starts_seq, slot, lb, wpool_ref, pscale_ref,
                    ext_ref, pool_ref, decay_ref, nat_ref, seg_ref, qk_ref, lev_ref):
    rows, levels, hist = PROMPT_ROWS, PROMPT_LEVELS, HIST_ROWS
    nat, seg, qk, lev = nat_ref[slot], seg_ref[slot], qk_ref[slot], lev_ref[slot]

    def pool_group(gi):
        def run():
            if gi == 0:
                ext_ref[0:hist, :] = jnp.where(starts_seq, 0.0, ext_ref[rows:rows + hist, :])
                ext_ref[hist:hist + rows, :] = z_blk[:, 0:POOL_WIDTH]
            w, c0 = POOL_WINDOWS[gi], gi * POOL_GROUP_DIM
            e = ext_ref[:, c0:c0 + POOL_GROUP_DIM]
            pos = t_blk * rows + lax.broadcasted_iota(jnp.int32, (rows, 1), 0)
            cnt = jnp.minimum(pos + 1, w).astype(F32)
            _pool_project(_window_sums(e, w, 0, hist, rows), cnt, e[hist:hist + rows], gi,
                          wpool_ref, pscale_ref, pool_ref[slot])
        return run

    def gates(c):
        cols = slice(c * HEAD_DIM, (c + 1) * HEAD_DIM)
        return lambda: _head_gates(z_blk[:, Z_Q + c * HEAD_DIM:Z_Q + (c + 1) * HEAD_DIM],
                                   z_blk[:, Z_F + c * HEAD_DIM:Z_F + (c + 1) * HEAD_DIM], lb[:, cols], c, rows, nat, qk)

    def tokens(k, n):
        def run():
            prefix = _token_levels(range(k * n, (k + 1) * n), rows, levels, nat, seg, lev)
            if (k + 1) * n == rows:
                decay_ref[slot][...] = prefix[-1]
        return run

    a, b = [pool_group(g) for g in range(len(POOL_WINDOWS))], [gates(c) for c in range(HEADS)]
    t = [tokens(k, rows // HEADS) for k in range(HEADS)]
    segments = lambda: _segment_products(rows, levels, nat, seg)
    return [[a[0], a[1], b[0]], [a[2], a[3], b[1]], b[2:5], b[5:8], [segments, t[0]], t[1:3], t[3:6], t[6:8]]


def _prompt_heads(slot, z_blk, mix_blk, hgn_ref, s_ref, pool_ref, decay_ref, qk_ref, lev_ref, sc_ref,
                  masks, eye):
    rows, levels = PROMPT_ROWS, PROMPT_LEVELS

    shared = {}

    def head(j):
        c0 = j * HEAD_DIM
        if j == 0:
            mix_blk[:, 0:POOL_WIDTH] = pool_ref[slot][...]
            shared["decay_t"] = jnp.transpose(decay_ref[slot][...])
        decay_t = shared["decay_t"]
        scores = _intra_scores(j, rows, levels, masks, eye, qk_ref[slot], lev_ref[slot], sc_ref[slot])
        v = z_blk[:, Z_I + c0:Z_I + c0 + HEAD_DIM].astype(BF16)
        qt = _head_rows(lev_ref[slot], levels, j, rows).astype(BF16)
        kt = _head_rows(lev_ref[slot], levels + 1, j, rows).astype(BF16)
        s_in = s_ref[j]
        o = (jnp.dot(scores.astype(BF16), v, preferred_element_type=F32)
             + jnp.dot(qt, s_in.astype(BF16), preferred_element_type=F32))
        s_ref[j] = decay_t[:, j:j + 1] * s_in + lax.dot_general(
            kt, v, (((0,), (0,)), ((), ())), preferred_element_type=F32)
        _head_output(o, j, z_blk[:, Z_G + c0:Z_G + c0 + HEAD_DIM], hgn_ref, mix_blk)

    return [functools.partial(head, j) for j in range(HEADS)]


def _prompt_layer_body(xc_ref, xn_ref, g1_ref, win_ref, wo_ref, lbp_ref, wpool_ref, pscale_ref, hgn_ref,
                       x1_ref, pooln_ref, snew_ref,
                       z_even, z_odd, mix_ref, ext_ref, s_ref, pool_ref, decay_ref, nat_ref, seg_ref, qk_ref,
                       lev_ref, sc_ref, *, layer, nt):
    rows, hist = PROMPT_ROWS, HIST_ROWS
    i = pl.program_id(0)
    steps_per_seq = nt // PROMPT_SLOTS
    tt = i % steps_per_seq
    lb = _lower_bound(lbp_ref[...], layer)
    one = lambda ref: [ref]
    prepare = functools.partial(
        _prompt_prepare, lb=lb, wpool_ref=wpool_ref, pscale_ref=pscale_ref, ext_ref=ext_ref,
        pool_ref=one(pool_ref), decay_ref=one(decay_ref), nat_ref=one(nat_ref), seg_ref=one(seg_ref),
        qk_ref=one(qk_ref), lev_ref=one(lev_ref))
    masks, eye = _level_masks(rows, PROMPT_LEVELS)
    heads = functools.partial(
        _prompt_heads, hgn_ref=hgn_ref, s_ref=s_ref, pool_ref=one(pool_ref), decay_ref=one(decay_ref),
        qk_ref=one(qk_ref), lev_ref=one(lev_ref), sc_ref=one(sc_ref), masks=masks, eye=eye)

    def in_proj(x_rows, z_ref):
        h = _rms(x_rows[...], g1_ref[...]).astype(BF16)
        z_ref[...] = jnp.dot(h, win_ref[...], preferred_element_type=F32)

    def mix_block(z_ref, t_blk, starts_seq, x_rows, out_rows):
        for batch in prepare(z_ref, t_blk, starts_seq, 0):
            for item in batch:
                item()
        for head in heads(0, z_ref, mix_ref):
            head()
        out_rows[...] = x_rows[...] + jnp.dot(mix_ref[...], wo_ref[...], preferred_element_type=F32)

    @pl.when(i == 0)
    def _():
        sc_ref[...] = jnp.zeros_like(sc_ref)
        ext_ref[...] = jnp.zeros_like(ext_ref)
        in_proj(xc_ref.at[0:rows], z_even)

    @pl.when(tt == 0)
    def _():
        s_ref[...] = jnp.zeros_like(s_ref)

    in_proj(xc_ref.at[rows:2 * rows], z_odd)
    mix_block(z_even, PROMPT_SLOTS * tt, tt == 0, xc_ref.at[0:rows], x1_ref.at[0:rows])
    in_proj(xn_ref, z_even)
    mix_block(z_odd, PROMPT_SLOTS * tt + 1, False, xc_ref.at[rows:2 * rows], x1_ref.at[rows:2 * rows])
    pooln_ref[0] = ext_ref[hist + rows - POOL_CACHE:hist + rows, :]

    @pl.when(tt == steps_per_seq - 1)
    def _():
        snew_ref[0] = s_ref[...]


def _prompt_layer(x, g1, w_in, w_o, lbp, wpool, pscale, hgn, batch, seq, layer):
    rows, slots = PROMPT_ROWS, PROMPT_SLOTS
    nt = seq // rows
    assert seq % (rows * slots) == 0
    nblk = batch * nt
    steps_per_seq = nt // slots
    const2 = lambda i: (0, 0)
    resident = lambda shape: pl.BlockSpec(shape, const2, pipeline_mode=pl.Buffered(1))
    return pl.pallas_call(
        functools.partial(_prompt_layer_body, layer=layer, nt=nt),
        grid=(nblk // slots,),
        in_specs=[
            pl.BlockSpec((slots * rows, D_MODEL), lambda i: (i, 0)),
            pl.BlockSpec((rows, D_MODEL), lambda i: (jnp.minimum(slots * i + slots, nblk - 1), 0)),
            pl.BlockSpec((1, D_MODEL), const2),
            resident(w_in.shape),
            resident(w_o.shape),
            pl.BlockSpec(lbp.shape, const2),
            pl.BlockSpec(wpool.shape, lambda i: (0, 0, 0)),
            pl.BlockSpec((1, POOL_WIDTH), const2),
            pl.BlockSpec((1, HG_WIDTH), const2),
        ],
        out_specs=[
            pl.BlockSpec((slots * rows, D_MODEL), lambda i: (i, 0)),
            pl.BlockSpec((1, POOL_CACHE, POOL_WIDTH), lambda i: (i // steps_per_seq, 0, 0)),
            pl.BlockSpec((1, HEADS, HEAD_DIM, HEAD_DIM), lambda i: (i // steps_per_seq, 0, 0, 0)),
        ],
        out_shape=[
            jax.ShapeDtypeStruct((batch * seq, D_MODEL), F32),
            jax.ShapeDtypeStruct((batch, POOL_CACHE, POOL_WIDTH), F32),
            jax.ShapeDtypeStruct((batch, HEADS, HEAD_DIM, HEAD_DIM), F32),
        ],
        scratch_shapes=[
            pltpu.VMEM((rows, IN_COLS), F32),
            pltpu.VMEM((rows, IN_COLS), F32),
            pltpu.VMEM((rows, D_MODEL), BF16),
            pltpu.VMEM((HIST_ROWS + rows, POOL_WIDTH), F32),
            pltpu.VMEM((HEADS, HEAD_DIM, HEAD_DIM), F32),
            pltpu.VMEM((rows, POOL_WIDTH), BF16),
            pltpu.VMEM((HEADS, HEAD_DIM), F32),
            *_level_scratch(rows, PROMPT_LEVELS),
        ],
        compiler_params=_cparams(("arbitrary",), 58),
        name="prompt_layer",
    )(x, x, g1, w_in, w_o, lbp, wpool, pscale, hgn)


def _mixer_sample_body(z_ref, cache_ref, sin_ref, lbp_ref, wpool_ref, pscale_ref, hgn_ref,
                       mix_ref, pooln_ref, snew_ref,
                       ext_ref, nat_ref, seg_ref, qk_ref, lev_ref, sc_ref, *, layer, nseq, steps, pos0):
    rows = nseq * steps
    levels = steps.bit_length() - 1
    hist = HIST_ROWS

    @pl.when(pl.program_id(0) == 0)
    def _():
        sc_ref[...] = jnp.zeros_like(sc_ref)

    ext_ref[:, 0:hist - POOL_CACHE, :] = jnp.zeros((nseq, hist - POOL_CACHE, POOL_WIDTH), F32)
    ext_ref[:, hist - POOL_CACHE:hist, :] = cache_ref[...]
    ext_ref[:, hist:hist + steps, :] = z_ref[:, 0:POOL_WIDTH].reshape(nseq, steps, POOL_WIDTH)
    pos = pos0 + lax.broadcasted_iota(jnp.int32, (1, steps, 1), 1)
    for gi, w in enumerate(POOL_WINDOWS):
        c0 = gi * POOL_GROUP_DIM
        e = ext_ref[:, :, c0:c0 + POOL_GROUP_DIM]
        cnt = jnp.minimum(pos + 1, w).astype(F32)
        mean = (_window_sums(e, w, 1, hist, steps) / cnt).reshape(rows, POOL_GROUP_DIM)
        u = e[:, hist:hist + steps].reshape(rows, POOL_GROUP_DIM)
        _pool_project(mean, 1.0, u, gi, wpool_ref, pscale_ref, mix_ref)
    pooln_ref[...] = ext_ref[:, hist + steps - POOL_CACHE:hist + steps, :]

    lb = _lower_bound(lbp_ref[...], layer)
    for c in range(HEADS):
        cols = slice(c * HEAD_DIM, (c + 1) * HEAD_DIM)
        _head_gates(z_ref[:, Z_Q + c * HEAD_DIM:Z_Q + (c + 1) * HEAD_DIM],
                    z_ref[:, Z_F + c * HEAD_DIM:Z_F + (c + 1) * HEAD_DIM], lb[:, cols], c, rows, nat_ref, qk_ref)
    _segment_products(rows, levels, nat_ref, seg_ref)
    prefix = _token_levels(range(rows), rows, levels, nat_ref, seg_ref, lev_ref)
    decay = jnp.stack(prefix[steps - 1::steps])
    decay_t = jnp.swapaxes(decay, 1, 2)
    masks, eye = _level_masks(rows, levels)
    for j in range(HEADS):
        c0 = j * HEAD_DIM
        scores = _intra_scores(j, rows, levels, masks, eye, qk_ref, lev_ref, sc_ref)
        v32 = z_ref[:, Z_I + c0:Z_I + c0 + HEAD_DIM]
        v = v32.astype(BF16)
        v3 = v32.reshape(nseq, steps, HEAD_DIM).astype(BF16)
        qt3 = _head_rows(lev_ref, levels, j, rows).reshape(nseq, steps, HEAD_DIM).astype(BF16)
        kt3 = _head_rows(lev_ref, levels + 1, j, rows).reshape(nseq, steps, HEAD_DIM).astype(BF16)
        s_in = sin_ref[:, j]
        o_state = lax.dot_general(qt3, s_in.astype(BF16), (((2,), (1,)), ((0,), (0,))),
                                  preferred_element_type=F32)
        o = jnp.dot(scores.astype(BF16), v, preferred_element_type=F32) + o_state.reshape(rows, HEAD_DIM)
        upd = lax.dot_general(kt3, v3, (((1,), (1,)), ((0,), (0,))), preferred_element_type=F32)
        snew_ref[:, j] = decay_t[:, :, j:j + 1] * s_in + upd
        _head_output(o, j, z_ref[:, Z_G + c0:Z_G + c0 + HEAD_DIM], hgn_ref, mix_ref)


def _mixer_sample(z, cache, state, lbp, wpool, pscale, hgn, nseq_total, steps, pos0, layer, nseq):
    rows = nseq * steps
    levels = steps.bit_length() - 1
    assert 1 << levels == steps and nseq_total % nseq == 0
    hist = HIST_ROWS
    const2 = lambda i: (0, 0)
    return pl.pallas_call(
        functools.partial(_mixer_sample_body, layer=layer, nseq=nseq, steps=steps, pos0=pos0),
        grid=(nseq_total // nseq,),
        in_specs=[
            pl.BlockSpec((rows, IN_COLS), lambda i: (i, 0)),
            pl.BlockSpec((nseq, POOL_CACHE, POOL_WIDTH), lambda i: (i, 0, 0)),
            pl.BlockSpec((nseq, HEADS, HEAD_DIM, HEAD_DIM), lambda i: (i, 0, 0, 0)),
            pl.BlockSpec(lbp.shape, const2),
            pl.BlockSpec(wpool.shape, lambda i: (0, 0, 0)),
            pl.BlockSpec((1, POOL_WIDTH), const2),
            pl.BlockSpec((1, HG_WIDTH), const2),
        ],
        out_specs=[
            pl.BlockSpec((rows, D_MODEL), lambda i: (i, 0)),
            pl.BlockSpec((nseq, POOL_CACHE, POOL_WIDTH), lambda i: (i, 0, 0)),
            pl.BlockSpec((nseq, HEADS, HEAD_DIM, HEAD_DIM), lambda i: (i, 0, 0, 0)),
        ],
        out_shape=[
            jax.ShapeDtypeStruct((nseq_total * steps, D_MODEL), BF16),
            jax.ShapeDtypeStruct((nseq_total, POOL_CACHE, POOL_WIDTH), F32),
            jax.ShapeDtypeStruct((nseq_total, HEADS, HEAD_DIM, HEAD_DIM), F32),
        ],
        scratch_shapes=[
            pltpu.VMEM((nseq, hist + steps, POOL_WIDTH), F32),
            *_level_scratch(rows, levels),
        ],
        compiler_params=_cparams(("arbitrary",), 48),
        name="mixer_sample",
    )(z, cache, state, lbp, wpool, pscale, hgn)


def _out_proj_body(x_ref, mix_ref, w_ref, o_ref):
    o_ref[...] = x_ref[...] + jnp.dot(mix_ref[...], w_ref[...], preferred_element_type=F32)


def _out_proj(x, mix, w, tm):
    m = x.shape[0]
    return pl.pallas_call(
        _out_proj_body,
        grid=(m // tm,),
        in_specs=[
            pl.BlockSpec((tm, D_MODEL), lambda i: (i, 0)),
            pl.BlockSpec((tm, D_MODEL), lambda i: (i, 0)),
            pl.BlockSpec((D_MODEL, D_MODEL), lambda i: (0, 0)),
        ],
        out_specs=pl.BlockSpec((tm, D_MODEL), lambda i: (i, 0)),
        out_shape=jax.ShapeDtypeStruct((m, D_MODEL), F32),
        compiler_params=_cparams(("arbitrary",), 48),
        name="out_proj",
    )(x, mix, w)


def _ffn_body(x_ref, g2_ref, wg_ref, wu_ref, wd_ref, gf_ref, y_ref, h_ref):
    j = pl.program_id(1)
    last = pl.num_programs(1) - 1
    chunks = _row_chunks(x_ref.shape[0])

    def down(h, wg, wu, wd):
        gate = jnp.dot(h, wg, preferred_element_type=F32)
        up = jnp.dot(h, wu, preferred_element_type=F32)
        act = (gate * _sigmoid(gate) * up).astype(BF16)
        return jnp.dot(act, wd, preferred_element_type=F32)

    weights = lambda: (wg_ref[...].astype(BF16), wu_ref[...].astype(BF16), wd_ref[...].astype(BF16))

    @pl.when(j == 0)
    def _():
        w = weights()
        for rs in chunks:
            h = _rms(x_ref[rs, :], g2_ref[...]).astype(BF16)
            h_ref[rs, :] = h
            y_ref[rs, :] = down(h, *w)

    @pl.when((j > 0) & (j < last))
    def _():
        y_ref[...] += down(h_ref[...], *weights())

    @pl.when(j == last)
    def _():
        w = weights()
        for rs in chunks:
            y_ref[rs, :] = _rms(x_ref[rs, :] + y_ref[rs, :] + down(h_ref[rs, :], *w), gf_ref[...])


def _ffn(x, g2, wg, wu, wd, gf, tm, tf):
    m = x.shape[0]
    d_ff = wg.shape[1]
    return pl.pallas_call(
        _ffn_body,
        grid=(m // tm, d_ff // tf),
        in_specs=[
            pl.BlockSpec((tm, D_MODEL), lambda i, j: (i, 0)),
            pl.BlockSpec((1, D_MODEL), lambda i, j: (0, 0)),
            pl.BlockSpec((D_MODEL, tf), lambda i, j: (0, j)),
            pl.BlockSpec((D_MODEL, tf), lambda i, j: (0, j)),
            pl.BlockSpec((tf, D_MODEL), lambda i, j: (j, 0)),
            pl.BlockSpec((1, D_MODEL), lambda i, j: (0, 0)),
        ],
        out_specs=pl.BlockSpec((tm, D_MODEL), lambda i, j: (i, 0)),
        out_shape=jax.ShapeDtypeStruct((m, D_MODEL), F32),
        scratch_shapes=[pltpu.VMEM((tm, D_MODEL), BF16)],
        compiler_params=_cparams(("arbitrary", "arbitrary"), 58),
        name="ffn",
    )(x, g2, wg, wu, wd, gf)


def _ffn_stage(x1, w):
    return _ffn(x1, w["norm2"], w["w_gate"], w["w_up"], w["w_down"], w["norm_f"], tm=1024, tf=256)


def kernel(x_prompt, x_sample, cache_pool, state_hgrn, lb_param, norm1, w_in, w_pool, pool_scale,
           hg_norm, w_o, norm2, w_gate, w_up, w_down, norm_f):
    depth = w_in.shape[0]
    assert depth == 1, "single-layer trunk"
    layer = 0
    batch, seq, _ = x_prompt.shape
    dec_batch, dec_seq, _ = x_sample.shape

    row = lambda a: a.reshape(1, -1).astype(F32)
    w = dict(
        norm1=row(norm1[layer]), w_in=w_in[layer].astype(BF16),
        w_o=w_o[layer].astype(BF16), norm2=row(norm2[layer]),
        w_gate=w_gate[layer], w_up=w_up[layer], w_down=w_down[layer], norm_f=row(norm_f),
    )
    lbp = lb_param.astype(F32)
    wpool = w_pool[layer].astype(BF16)
    pscale = row(pool_scale[layer])
    hgn = row(hg_norm[layer])

    x1_p, pool_p, s_p = _prompt_layer(x_prompt.reshape(batch * seq, D_MODEL), w["norm1"], w["w_in"], w["w_o"],
                                      lbp, wpool, pscale, hgn, batch, seq, layer)
    y_p = _ffn_stage(x1_p, w)

    xs = x_sample.reshape(dec_batch * dec_seq, D_MODEL)
    z_s = _in_proj(xs, w["norm1"], w["w_in"], tm=1024, tn=1024)
    mix_s, pool_s, s_s = _mixer_sample(z_s, cache_pool[layer], state_hgrn[layer], lbp, wpool, pscale, hgn,
                                       dec_batch, dec_seq, PAST_LEN, layer, nseq=8)
    y_s = _ffn_stage(_out_proj(xs, mix_s, w["w_o"], tm=512), w)

    return (y_p.reshape(batch, seq, D_MODEL), y_s.reshape(dec_batch, dec_seq, D_MODEL),
            pool_p[None], s_p[None], pool_s[None], s_s[None])
```

```python
import functools

import jax
import jax.numpy as jnp
from jax import lax
from jax.experimental import pallas as pl
from jax.experimental.pallas import tpu as pltpu

D_MODEL = 2048
POOL_WIDTH = 1024
POOL_WINDOWS = (2, 4, 8, 16)
POOL_GROUP_DIM = POOL_WIDTH // len(POOL_WINDOWS)
POOL_CACHE = max(POOL_WINDOWS) - 1
HG_WIDTH = 1024
HEAD_DIM = 128
HEADS = HG_WIDTH // HEAD_DIM
IN_COLS = POOL_WIDTH + 4 * HG_WIDTH
Z_Q, Z_F, Z_I, Z_G = (POOL_WIDTH + n * HG_WIDTH for n in range(4))
EPS = 1e-6
SUBLANES = 8
assert HEADS == SUBLANES
HIST_ROWS = 2 * SUBLANES
PAST_LEN = 16384

F32 = jnp.float32
BF16 = jnp.bfloat16

MIB = 1024 * 1024


def _cparams(sem, vmem_mib):
    return pltpu.CompilerParams(dimension_semantics=sem, vmem_limit_bytes=vmem_mib * MIB)


def _rms(x, g):
    return x * lax.rsqrt(jnp.mean(x * x, axis=-1, keepdims=True) + EPS) * g


def _sigmoid(x):
    return 1.0 / (1.0 + jnp.exp(-x))


ROW_CHUNK = 256


def _row_chunks(n):
    return [slice(r, r + ROW_CHUNK) for r in range(0, n, ROW_CHUNK)]


def _in_proj_body(x_ref, g_ref, w_ref, z_ref, wb_ref, h_ref):
    j = pl.program_id(1)
    w = w_ref[...].astype(BF16)
    wb_ref[...] = w

    @pl.when(j == 0)
    def _():
        for rs in _row_chunks(x_ref.shape[0]):
            h = _rms(x_ref[rs, :], g_ref[...]).astype(BF16)
            h_ref[rs, :] = h
            z_ref[rs, :] = jnp.dot(h, w, preferred_element_type=F32)

    @pl.when(j > 0)
    def _():
        z_ref[...] = jnp.dot(h_ref[...], w, preferred_element_type=F32)


def _in_proj(x, g, w, tm, tn):
    m = x.shape[0]
    return pl.pallas_call(
        _in_proj_body,
        grid=(m // tm, IN_COLS // tn),
        in_specs=[
            pl.BlockSpec((tm, D_MODEL), lambda i, j: (i, 0)),
            pl.BlockSpec((1, D_MODEL), lambda i, j: (0, 0)),
            pl.BlockSpec((D_MODEL, tn), lambda i, j: (0, j)),
        ],
        out_specs=[pl.BlockSpec((tm, tn), lambda i, j: (i, j)), pl.BlockSpec((D_MODEL, tn), lambda i, j: (0, j))],
        out_shape=[jax.ShapeDtypeStruct((m, IN_COLS), F32), jax.ShapeDtypeStruct(w.shape, BF16)],
        scratch_shapes=[pltpu.VMEM((tm, D_MODEL), BF16)],
        compiler_params=_cparams(("arbitrary", "arbitrary"), 48),
        name="in_proj",
    )(x, g, w)


def _lower_bound(lbp, layer):
    e = jnp.exp(lbp - jnp.max(lbp, axis=0, keepdims=True))
    return jnp.sum(e[: layer + 1], axis=0, keepdims=True) / jnp.sum(e, axis=0, keepdims=True)


def _pool_project(acc, cnt, u, gi, wpool_ref, pscale_ref, mix_ref):
    c0 = gi * POOL_GROUP_DIM
    pooled = acc / cnt - u
    out = jnp.dot(pooled.astype(BF16), wpool_ref[gi], preferred_element_type=F32)
    mix_ref[:, c0:c0 + POOL_GROUP_DIM] = (out * pscale_ref[:, c0:c0 + POOL_GROUP_DIM]).astype(BF16)


def _token_tile(nat_ref, a, t):
    return nat_ref[a, t // SUBLANES, pl.ds(t % SUBLANES, HEADS, stride=SUBLANES), :]


def _head_rows(lev_ref, lvl, j, rows):
    return lev_ref[lvl, pl.ds(j, rows, stride=HEADS), :]


def _head_gates(zq, zf, lb, c, rows, nat_ref, qk_ref):
    fg = lb + (1.0 - lb) * _sigmoid(zf)
    qs = zq * _sigmoid(zq)
    kk = 1.0 - fg
    qk_ref[0, :, c * HEAD_DIM:(c + 1) * HEAD_DIM] = qs.astype(BF16)
    qk_ref[1, :, c * HEAD_DIM:(c + 1) * HEAD_DIM] = kk.astype(BF16)
    for a, x in enumerate((qs, kk, fg)):
        nat_ref[a, :, c * SUBLANES:(c + 1) * SUBLANES, :] = x.reshape(rows // SUBLANES, SUBLANES, HEAD_DIM)


def _segment_offsets(rows, levels):
    off = [0]
    for lvl in range(levels):
        off.append(off[-1] + (rows >> lvl))
    return off


def _segment_products(rows, levels, nat_ref, seg_ref):
    off = _segment_offsets(rows, levels)
    for t in range(rows):
        seg_ref[t] = _token_tile(nat_ref, 2, t)
    for lvl in range(1, levels):
        for m in range(rows >> lvl):
            seg_ref[off[lvl] + m] = seg_ref[off[lvl - 1] + 2 * m] * seg_ref[off[lvl - 1] + 2 * m + 1]


def _token_levels(tokens, rows, levels, nat_ref, seg_ref, lev_ref):
    off = _segment_offsets(rows, levels)
    prefix = []
    for t in tokens:
        tok = slice(t * HEADS, (t + 1) * HEADS)
        q_t, k_t = _token_tile(nat_ref, 0, t), _token_tile(nat_ref, 1, t)
        p_t, r_t = seg_ref[t], None
        for lvl in range(levels):
            m = t >> lvl
            sibling = seg_ref[off[lvl] + (m ^ 1)]
            if m & 1:
                lev_ref[lvl, tok, :] = q_t * p_t
                p_t = p_t * sibling
            else:
                lev_ref[lvl, tok, :] = k_t if r_t is None else k_t * r_t
                r_t = sibling if r_t is None else r_t * sibling
        lev_ref[levels, tok, :] = q_t * p_t
        lev_ref[levels + 1, tok, :] = k_t if r_t is None else k_t * r_t
        prefix.append(p_t)
    return prefix


def _level_scratch(rows, levels):
    return [
        pltpu.VMEM((3, rows // SUBLANES, HEADS * SUBLANES, HEAD_DIM), F32),
        pltpu.VMEM((2 * rows, HEADS, HEAD_DIM), F32),
        pltpu.VMEM((2, rows, HG_WIDTH), BF16),
        pltpu.VMEM((levels + 2, rows * HEADS, HEAD_DIM), F32),
        pltpu.VMEM((HEADS, rows, rows), F32),
    ]


def _level_masks(rows, levels):
    ti = lax.broadcasted_iota(jnp.int32, (rows, rows), 0)
    si = lax.broadcasted_iota(jnp.int32, (rows, rows), 1)
    x = ti ^ si
    masks = []
    for lvl in range(levels):
        h = 1 << lvl
        if h < SUBLANES:
            masks.append(((x >> lvl) == 1) & (((ti >> lvl) & 1) == 1))
        else:
            half = lax.broadcasted_iota(jnp.int32, (h, rows), 1) >> lvl
            masks.append([half == 2 * b for b in range(rows // (2 * h))])
    return masks, ti == si


def _nt(a, b):
    return lax.dot_general(a, b, (((1,), (1,)), ((), ())), preferred_element_type=F32)


def _intra_scores(j, rows, levels, masks, eye, qk_ref, lev_ref, sc_ref):
    c0 = j * HEAD_DIM
    sc_ref = sc_ref.at[j]
    pltpu.store(sc_ref, _nt(qk_ref[0, :, c0:c0 + HEAD_DIM], qk_ref[1, :, c0:c0 + HEAD_DIM]), mask=eye)
    for lvl in range(levels):
        h = 1 << lvl
        x32 = _head_rows(lev_ref, lvl, j, rows)
        x = x32.astype(BF16)
        if h < SUBLANES:
            pltpu.store(sc_ref, _nt(x, x), mask=masks[lvl])
            continue
        blocks = rows // (2 * h)
        upper = x32.reshape(blocks, 2 * h, HEAD_DIM)[:, h:].reshape(rows // 2, HEAD_DIM).astype(BF16)
        s = _nt(upper, x)
        for b in range(blocks):
            pltpu.store(sc_ref.at[b * 2 * h + h:(b + 1) * 2 * h, :], s[b * h:(b + 1) * h], mask=masks[lvl][b])
    return sc_ref[...]


def _window_sums(e, w, axis, hist, rows):
    s, d = e, 1
    while d < min(w, SUBLANES):
        s = s + pltpu.roll(s, d, axis)
        d *= 2
    take = lambda a, lo: lax.slice_in_dim(a, lo, lo + rows, axis=axis)
    out = take(s, hist)
    if w > SUBLANES:
        assert w == 2 * SUBLANES
        out = out + take(s, hist - SUBLANES)
    return out


def _head_output(o, j, zg, hgn_ref, mix_ref):
    c0 = j * HEAD_DIM
    o = _rms(o, hgn_ref[:, c0:c0 + HEAD_DIM]) * (zg * _sigmoid(zg))
    mix_ref[:, POOL_WIDTH + c0:POOL_WIDTH + c0 + HEAD_DIM] = o.astype(BF16)


PROMPT_ROWS = 128
PROMPT_LEVELS = 7


PROMPT_SLOTS = 2


def _prompt_prepare(z_blk, t_blk, starts_seq, slot, lb, wpool_ref, pscale_ref,
                    ext_ref, pool_ref, decay_ref, nat_ref, seg_ref, qk_ref, lev_ref):
    rows, levels, hist = PROMPT_ROWS, PROMPT_LEVELS, HIST_ROWS
    nat, seg, qk, lev = nat_ref[slot], seg_ref[slot], qk_ref[slot], lev_ref[slot]

    def pool_group(gi):
        def run():
            if gi == 0:
                ext_ref[0:hist, :] = jnp.where(starts_seq, 0.0, ext_ref[rows:rows + hist, :])
                ext_ref[hist:hist + rows, :] = z_blk[:, 0:POOL_WIDTH]
            w, c0 = POOL_WINDOWS[gi], gi * POOL_GROUP_DIM
            e = ext_ref[:, c0:c0 + POOL_GROUP_DIM]
            pos = t_blk * rows + lax.broadcasted_iota(jnp.int32, (rows, 1), 0)
            cnt = jnp.minimum(pos + 1, w).astype(F32)
            _pool_project(_window_sums(e, w, 0, hist, rows), cnt, e[hist:hist + rows], gi,
                          wpool_ref, pscale_ref, pool_ref[slot])
        return run

    def gates(c):
        cols = slice(c * HEAD_DIM, (c + 1) * HEAD_DIM)
        return lambda: _head_gates(z_blk[:, Z_Q + c * HEAD_DIM:Z_Q + (c + 1) * HEAD_DIM],
                                   z_blk[:, Z_F + c * HEAD_DIM:Z_F + (c + 1) * HEAD_DIM], lb[:, cols], c, rows, nat, qk)

    def tokens(k, n):
        def run():
            prefix = _token_levels(range(k * n, (k + 1) * n), rows, levels, nat, seg, lev)
            if (k + 1) * n == rows:
                decay_ref[slot][...] = prefix[-1]
        return run

    a, b = [pool_group(g) for g in range(len(POOL_WINDOWS))], [gates(c) for c in range(HEADS)]
    t = [tokens(k, rows // HEADS) for k in range(HEADS)]
    segments = lambda: _segment_products(rows, levels, nat, seg)
    return [[a[0], a[1], b[0]], [a[2], a[3], b[1]], b[2:5], b[5:8], [segments, t[0]], t[1:3], t[3:6], t[6:8]]


def _prompt_heads(slot, z_blk, mix_blk, hgn_ref, s_ref, pool_ref, decay_ref, qk_ref, lev_ref, sc_ref,
                  masks, eye):
    rows, levels = PROMPT_ROWS, PROMPT_LEVELS

    shared = {}

    def head(j):
        c0 = j * HEAD_DIM
        if j == 0:
            mix_blk[:, 0:POOL_WIDTH] = pool_ref[slot][...]
            shared["decay_t"] = jnp.transpose(decay_ref[slot][...])
        decay_t = shared["decay_t"]
        scores = _intra_scores(j, rows, levels, masks, eye, qk_ref[slot], lev_ref[slot], sc_ref[slot])
        v = z_blk[:, Z_I + c0:Z_I + c0 + HEAD_DIM].astype(BF16)
        qt = _head_rows(lev_ref[slot], levels, j, rows).astype(BF16)
        kt = _head_rows(lev_ref[slot], levels + 1, j, rows).astype(BF16)
        s_in = s_ref[j]
        o = (jnp.dot(scores.astype(BF16), v, preferred_element_type=F32)
             + jnp.dot(qt, s_in.astype(BF16), preferred_element_type=F32))
        s_ref[j] = decay_t[:, j:j + 1] * s_in + lax.dot_general(
            kt, v, (((0,), (0,)), ((), ())), preferred_element_type=F32)
        _head_output(o, j, z_blk[:, Z_G + c0:Z_G + c0 + HEAD_DIM], hgn_ref, mix_blk)

    return [functools.partial(head, j) for j in range(HEADS)]


def _prompt_layer_body(xc_ref, xn_ref, g1_ref, win_ref, wo_ref, lbp_ref, wpool_ref, pscale_ref, hgn_ref,
                       x1_ref, pooln_ref, snew_ref,
                       z_even, z_odd, mix_ref, ext_ref, s_ref, pool_ref, decay_ref, nat_ref, seg_ref, qk_ref,
                       lev_ref, sc_ref, *, layer, nt):
    rows, hist = PROMPT_ROWS, HIST_ROWS
    i = pl.program_id(0)
    steps_per_seq = nt // PROMPT_SLOTS
    tt = i % steps_per_seq
    lb = _lower_bound(lbp_ref[...], layer)
    one = lambda ref: [ref]
    prepare = functools.partial(
        _prompt_prepare, lb=lb, wpool_ref=wpool_ref, pscale_ref=pscale_ref, ext_ref=ext_ref,
        pool_ref=one(pool_ref), decay_ref=one(decay_ref), nat_ref=one(nat_ref), seg_ref=one(seg_ref),
        qk_ref=one(qk_ref), lev_ref=one(lev_ref))
    masks, eye = _level_masks(rows, PROMPT_LEVELS)
    heads = functools.partial(
        _prompt_heads, hgn_ref=hgn_ref, s_ref=s_ref, pool_ref=one(pool_ref), decay_ref=one(decay_ref),
        qk_ref=one(qk_ref), lev_ref=one(lev_ref), sc_ref=one(sc_ref), masks=masks, eye=eye)

    def in_proj(x_rows, z_ref):
        h = _rms(x_rows[...], g1_ref[...]).astype(BF16)
        z_ref[...] = jnp.dot(h, win_ref[...], preferred_element_type=F32)

    def mix_block(z_ref, t_blk, starts_seq, x_rows, out_rows):
        for batch in prepare(z_ref, t_blk, starts_seq, 0):
            for item in batch:
                item()
        for head in heads(0, z_ref, mix_ref):
            head()
        out_rows[...] = x_rows[...] + jnp.dot(mix_ref[...], wo_ref[...], preferred_element_type=F32)

    @pl.when(i == 0)
    def _():
        sc_ref[...] = jnp.zeros_like(sc_ref)
        ext_ref[...] = jnp.zeros_like(ext_ref)
        in_proj(xc_ref.at[0:rows], z_even)

    @pl.when(tt == 0)
    def _():
        s_ref[...] = jnp.zeros_like(s_ref)

    in_proj(xc_ref.at[rows:2 * rows], z_odd)
    mix_block(z_even, PROMPT_SLOTS * tt, tt == 0, xc_ref.at[0:rows], x1_ref.at[0:rows])
    in_proj(xn_ref, z_even)
    mix_block(z_odd, PROMPT_SLOTS * tt + 1, False, xc_ref.at[rows:2 * rows], x1_ref.at[rows:2 * rows])
    pooln_ref[0] = ext_ref[hist + rows - POOL_CACHE:hist + rows, :]

    @pl.when(tt == steps_per_seq - 1)
    def _():
        snew_ref[0] = s_ref[...]


def _prompt_layer(x, g1, w_in, w_o, lbp, wpool, pscale, hgn, batch, seq, layer):
    rows, slots = PROMPT_ROWS, PROMPT_SLOTS
    nt = seq // rows
    assert seq % (rows * slots) == 0
    nblk = batch * nt
    steps_per_seq = nt // slots
    const2 = lambda i: (0, 0)
    resident = lambda shape: pl.BlockSpec(shape, const2, pipeline_mode=pl.Buffered(1))
    return pl.pallas_call(
        functools.partial(_prompt_layer_body, layer=layer, nt=nt),
        grid=(nblk // slots,),
        in_specs=[
            pl.BlockSpec((slots * rows, D_MODEL), lambda i: (i, 0)),
            pl.BlockSpec((rows, D_MODEL), lambda i: (jnp.minimum(slots * i + slots, nblk - 1), 0)),
            pl.BlockSpec((1, D_MODEL), const2),
            resident(w_in.shape),
            resident(w_o.shape),
            pl.BlockSpec(lbp.shape, const2),
            pl.BlockSpec(wpool.shape, lambda i: (0, 0, 0)),
            pl.BlockSpec((1, POOL_WIDTH), const2),
            pl.BlockSpec((1, HG_WIDTH), const2),
        ],
        out_specs=[
            pl.BlockSpec((slots * rows, D_MODEL), lambda i: (i, 0)),
            pl.BlockSpec((1, POOL_CACHE, POOL_WIDTH), lambda i: (i // steps_per_seq, 0, 0)),
            pl.BlockSpec((1, HEADS, HEAD_DIM, HEAD_DIM), lambda i: (i // steps_per_seq, 0, 0, 0)),
        ],
        out_shape=[
            jax.ShapeDtypeStruct((batch * seq, D_MODEL), F32),
            jax.ShapeDtypeStruct((batch, POOL_CACHE, POOL_WIDTH), F32),
            jax.ShapeDtypeStruct((batch, HEADS, HEAD_DIM, HEAD_DIM), F32),
        ],
        scratch_shapes=[
            pltpu.VMEM((rows, IN_COLS), F32),
            pltpu.VMEM((rows, IN_COLS), F32),
            pltpu.VMEM((rows, D_MODEL), BF16),
            pltpu.VMEM((HIST_ROWS + rows, POOL_WIDTH), F32),
            pltpu.VMEM((HEADS, HEAD_DIM, HEAD_DIM), F32),
            pltpu.VMEM((rows, POOL_WIDTH), BF16),
            pltpu.VMEM((HEADS, HEAD_DIM), F32),
            *_level_scratch(rows, PROMPT_LEVELS),
        ],
        compiler_params=_cparams(("arbitrary",), 58),
        name="prompt_layer",
    )(x, x, g1, w_in, w_o, lbp, wpool, pscale, hgn)


def _mixer_sample_body(z_ref, cache_ref, sin_ref, lbp_ref, wpool_ref, pscale_ref, hgn_ref,
                       mix_ref, pooln_ref, snew_ref,
                       ext_ref, nat_ref, seg_ref, qk_ref, lev_ref, sc_ref, *, layer, nseq, steps, pos0):
    rows = nseq * steps
    levels = steps.bit_length() - 1
    hist = HIST_ROWS

    @pl.when(pl.program_id(0) == 0)
    def _():
        sc_ref[...] = jnp.zeros_like(sc_ref)

    ext_ref[:, 0:hist - POOL_CACHE, :] = jnp.zeros((nseq, hist - POOL_CACHE, POOL_WIDTH), F32)
    ext_ref[:, hist - POOL_CACHE:hist, :] = cache_ref[0]
    ext_ref[:, hist:hist + steps, :] = z_ref[:, 0:POOL_WIDTH].reshape(nseq, steps, POOL_WIDTH)
    pos = pos0 + lax.broadcasted_iota(jnp.int32, (1, steps, 1), 1)
    for gi, w in enumerate(POOL_WINDOWS):
        c0 = gi * POOL_GROUP_DIM
        e = ext_ref[:, :, c0:c0 + POOL_GROUP_DIM]
        cnt = jnp.minimum(pos + 1, w).astype(F32)
        mean = (_window_sums(e, w, 1, hist, steps) / cnt).reshape(rows, POOL_GROUP_DIM)
        u = e[:, hist:hist + steps].reshape(rows, POOL_GROUP_DIM)
        _pool_project(mean, 1.0, u, gi, wpool_ref, pscale_ref, mix_ref)
    pooln_ref[...] = ext_ref[:, hist + steps - POOL_CACHE:hist + steps, :]

    lb = _lower_bound(lbp_ref[...], layer)
    for c in range(HEADS):
        cols = slice(c * HEAD_DIM, (c + 1) * HEAD_DIM)
        _head_gates(z_ref[:, Z_Q + c * HEAD_DIM:Z_Q + (c + 1) * HEAD_DIM],
                    z_ref[:, Z_F + c * HEAD_DIM:Z_F + (c + 1) * HEAD_DIM], lb[:, cols], c, rows, nat_ref, qk_ref)
    _segment_products(rows, levels, nat_ref, seg_ref)
    prefix = _token_levels(range(rows), rows, levels, nat_ref, seg_ref, lev_ref)
    decay = jnp.stack(prefix[steps - 1::steps])
    decay_t = jnp.swapaxes(decay, 1, 2)
    masks, eye = _level_masks(rows, levels)
    for j in range(HEADS):
        c0 = j * HEAD_DIM
        scores = _intra_scores(j, rows, levels, masks, eye, qk_ref, lev_ref, sc_ref)
        v32 = z_ref[:, Z_I + c0:Z_I + c0 + HEAD_DIM]
        v = v32.astype(BF16)
        v3 = v32.reshape(nseq, steps, HEAD_DIM).astype(BF16)
        qt3 = _head_rows(lev_ref, levels, j, rows).reshape(nseq, steps, HEAD_DIM).astype(BF16)
        kt3 = _head_rows(lev_ref, levels + 1, j, rows).reshape(nseq, steps, HEAD_DIM).astype(BF16)
        s_in = sin_ref[:, j]
        o_state = lax.dot_general(qt3, s_in.astype(BF16), (((2,), (1,)), ((0,), (0,))),
                                  preferred_element_type=F32)
        o = jnp.dot(scores.astype(BF16), v, preferred_element_type=F32) + o_state.reshape(rows, HEAD_DIM)
        upd = lax.dot_general(kt3, v3, (((1,), (1,)), ((0,), (0,))), preferred_element_type=F32)
        snew_ref[:, j] = decay_t[:, :, j:j + 1] * s_in + upd
        _head_output(o, j, z_ref[:, Z_G + c0:Z_G + c0 + HEAD_DIM], hgn_ref, mix_ref)


def _mixer_sample(z, cache, state, lbp, wpool, pscale, hgn, nseq_total, steps, pos0, layer, nseq):
    rows = nseq * steps
    levels = steps.bit_length() - 1
    assert 1 << levels == steps and nseq_total % nseq == 0
    hist = HIST_ROWS
    const2 = lambda i: (0, 0)
    return pl.pallas_call(
        functools.partial(_mixer_sample_body, layer=layer, nseq=nseq, steps=steps, pos0=pos0),
        grid=(nseq_total // nseq,),
        in_specs=[
            pl.BlockSpec((rows, IN_COLS), lambda i: (i, 0)),
            pl.BlockSpec((1, nseq, POOL_CACHE, POOL_WIDTH), lambda i: (layer, i, 0, 0)),
            pl.BlockSpec((nseq, HEADS, HEAD_DIM, HEAD_DIM), lambda i: (i, 0, 0, 0)),
            pl.BlockSpec(lbp.shape, const2),
            pl.BlockSpec(wpool.shape, lambda i: (0, 0, 0)),
            pl.BlockSpec((1, POOL_WIDTH), const2),
            pl.BlockSpec((1, HG_WIDTH), const2),
        ],
        out_specs=[
            pl.BlockSpec((rows, D_MODEL), lambda i: (i, 0)),
            pl.BlockSpec((nseq, POOL_CACHE, POOL_WIDTH), lambda i: (i, 0, 0)),
            pl.BlockSpec((nseq, HEADS, HEAD_DIM, HEAD_DIM), lambda i: (i, 0, 0, 0)),
        ],
        out_shape=[
            jax.ShapeDtypeStruct((nseq_total * steps, D_MODEL), BF16),
            jax.ShapeDtypeStruct((nseq_total, POOL_CACHE, POOL_WIDTH), F32),
            jax.ShapeDtypeStruct((nseq_total, HEADS, HEAD_DIM, HEAD_DIM), F32),
        ],
        scratch_shapes=[
            pltpu.VMEM((nseq, hist + steps, POOL_WIDTH), F32),
            *_level_scratch(rows, levels),
        ],
        compiler_params=_cparams(("arbitrary",), 48),
        name="mixer_sample",
    )(z, cache, state, lbp, wpool, pscale, hgn)


def _out_proj_body(x_ref, mix_ref, w_ref, o_ref, wb_ref):
    w = w_ref[...].astype(BF16)
    wb_ref[...] = w
    o_ref[...] = x_ref[...] + jnp.dot(mix_ref[...], w, preferred_element_type=F32)


def _out_proj(x, mix, w, tm, tn):
    m = x.shape[0]
    return pl.pallas_call(
        _out_proj_body,
        grid=(m // tm, D_MODEL // tn),
        in_specs=[
            pl.BlockSpec((tm, tn), lambda i, j: (i, j)),
            pl.BlockSpec((tm, D_MODEL), lambda i, j: (i, 0)),
            pl.BlockSpec((D_MODEL, tn), lambda i, j: (0, j)),
        ],
        out_specs=[pl.BlockSpec((tm, tn), lambda i, j: (i, j)), pl.BlockSpec((D_MODEL, tn), lambda i, j: (0, j))],
        out_shape=[jax.ShapeDtypeStruct((m, D_MODEL), F32), jax.ShapeDtypeStruct(w.shape, BF16)],
        compiler_params=_cparams(("arbitrary", "arbitrary"), 48),
        name="out_proj",
    )(x, mix, w)


def _ffn_body(x_ref, g2_ref, wg_ref, wu_ref, wd_ref, gf_ref, y_ref, *rest):
    *wb_refs, h_ref = rest
    j = pl.program_id(1)
    last = pl.num_programs(1) - 1
    chunks = _row_chunks(x_ref.shape[0])

    def down(h, wg, wu, wd):
        gate = jnp.dot(h, wg, preferred_element_type=F32)
        up = jnp.dot(h, wu, preferred_element_type=F32)
        act = (gate * _sigmoid(gate) * up).astype(BF16)
        return jnp.dot(act, wd, preferred_element_type=F32)

    def weights():
        if not wb_refs:
            return wg_ref[...], wu_ref[...], wd_ref[...]
        w = tuple(ref[...].astype(BF16) for ref in (wg_ref, wu_ref, wd_ref))
        for wb_ref, wb in zip(wb_refs, w):
            wb_ref[...] = wb
        return w

    @pl.when(j == 0)
    def _():
        w = weights()
        for rs in chunks:
            h = _rms(x_ref[rs, :], g2_ref[...]).astype(BF16)
            h_ref[rs, :] = h
            y_ref[rs, :] = down(h, *w)

    @pl.when((j > 0) & (j < last))
    def _():
        y_ref[...] += down(h_ref[...], *weights())

    @pl.when(j == last)
    def _():
        w = weights()
        for rs in chunks:
            y_ref[rs, :] = _rms(x_ref[rs, :] + y_ref[rs, :] + down(h_ref[rs, :], *w), gf_ref[...])


def _ffn(x, g2, wg, wu, wd, gf, tm, tf):
    m = x.shape[0]
    d_ff = wg.shape[1]
    emit = wg.dtype != BF16
    w_specs = [
        pl.BlockSpec((D_MODEL, tf), lambda i, j: (0, j)),
        pl.BlockSpec((D_MODEL, tf), lambda i, j: (0, j)),
        pl.BlockSpec((tf, D_MODEL), lambda i, j: (j, 0)),
    ]
    x_mode = dict(pipeline_mode=pl.Buffered(1)) if m == tm else {}
    return pl.pallas_call(
        _ffn_body,
        grid=(m // tm, d_ff // tf),
        in_specs=[
            pl.BlockSpec((tm, D_MODEL), lambda i, j: (i, 0), **x_mode),
            pl.BlockSpec((1, D_MODEL), lambda i, j: (0, 0)),
            *w_specs,
            pl.BlockSpec((1, D_MODEL), lambda i, j: (0, 0)),
        ],
        out_specs=[pl.BlockSpec((tm, D_MODEL), lambda i, j: (i, 0))] + (w_specs if emit else []),
        out_shape=[jax.ShapeDtypeStruct((m, D_MODEL), F32)]
        + ([jax.ShapeDtypeStruct(a.shape, BF16) for a in (wg, wu, wd)] if emit else []),
        scratch_shapes=[pltpu.VMEM((tm, D_MODEL), BF16)],
        compiler_params=_cparams(("arbitrary", "arbitrary"), 58),
        name="ffn",
    )(x, g2, wg, wu, wd, gf)


def kernel(x_prompt, x_sample, cache_pool, state_hgrn, lb_param, norm1, w_in, w_pool, pool_scale,
           hg_norm, w_o, norm2, w_gate, w_up, w_down, norm_f):
    depth = w_in.shape[0]
    assert depth == 1, "single-layer trunk"
    layer = 0
    batch, seq, _ = x_prompt.shape
    dec_batch, dec_seq, _ = x_sample.shape

    row = lambda a: a.reshape(1, -1).astype(F32)
    g1, g2, gf = row(norm1[layer]), row(norm2[layer]), row(norm_f)
    lbp = lb_param.astype(F32)
    wpool = w_pool[layer].astype(BF16)
    pscale = row(pool_scale[layer])
    hgn = row(hg_norm[layer])

    m_s = dec_batch * dec_seq
    xs = x_sample.reshape(m_s, D_MODEL)
    z_s, w_in_b = _in_proj(xs, g1, w_in[layer], tm=m_s, tn=512)
    mix_s, pool_s, s_s = _mixer_sample(z_s, cache_pool, state_hgrn[layer], lbp, wpool, pscale, hgn,
                                       dec_batch, dec_seq, PAST_LEN, layer, nseq=8)
    x1_s, w_o_b = _out_proj(xs, mix_s, w_o[layer], tm=m_s, tn=512)
    y_s, w_gate_b, w_up_b, w_down_b = _ffn(x1_s, g2, w_gate[layer], w_up[layer], w_down[layer], gf, tm=m_s, tf=256)

    x1_p, pool_p, s_p = _prompt_layer(x_prompt.reshape(batch * seq, D_MODEL), g1, w_in_b, w_o_b,
                                      lbp, wpool, pscale, hgn, batch, seq, layer)
    (y_p,) = _ffn(x1_p, g2, w_gate_b, w_up_b, w_down_b, gf, tm=1024, tf=512)

    return (y_p.reshape(batch, seq, D_MODEL), y_s.reshape(dec_batch, dec_seq, D_MODEL),
            pool_p[None], s_p[None], pool_s[None], s_s[None])
```

```python
import functools

import jax
import jax.numpy as jnp
from jax import lax
from jax.experimental import pallas as pl
from jax.experimental.pallas import tpu as pltpu

D_MODEL = 2048
POOL_WIDTH = 1024
POOL_WINDOWS = (2, 4, 8, 16)
POOL_GROUP_DIM = POOL_WIDTH // len(POOL_WINDOWS)
POOL_CACHE = max(POOL_WINDOWS) - 1
HG_WIDTH = 1024
HEAD_DIM = 128
HEADS = HG_WIDTH // HEAD_DIM
IN_COLS = POOL_WIDTH + 4 * HG_WIDTH
Z_Q, Z_F, Z_I, Z_G = (POOL_WIDTH + n * HG_WIDTH for n in range(4))
EPS = 1e-6
SUBLANES = 8
assert HEADS == SUBLANES
HIST_ROWS = 2 * SUBLANES
PAST_LEN = 16384

F32 = jnp.float32
BF16 = jnp.bfloat16

MIB = 1024 * 1024


def _cparams(sem, vmem_mib):
    return pltpu.CompilerParams(dimension_semantics=sem, vmem_limit_bytes=vmem_mib * MIB)


def _rms(x, g):
    return x * lax.rsqrt(jnp.mean(x * x, axis=-1, keepdims=True) + EPS) * g


def _sigmoid(x):
    return 1.0 / (1.0 + jnp.exp(-x))


ROW_CHUNK = 256


def _row_chunks(n):
    return [slice(r, r + ROW_CHUNK) for r in range(0, n, ROW_CHUNK)]


def _in_proj_body(x_ref, g_ref, w_ref, z_ref, wb_ref, h_ref):
    j = pl.program_id(1)
    w = w_ref[...].astype(BF16)
    wb_ref[...] = w

    @pl.when(j == 0)
    def _():
        for rs in _row_chunks(x_ref.shape[0]):
            h = _rms(x_ref[rs, :], g_ref[...]).astype(BF16)
            h_ref[rs, :] = h
            z_ref[rs, :] = jnp.dot(h, w, preferred_element_type=F32)

    @pl.when(j > 0)
    def _():
        z_ref[...] = jnp.dot(h_ref[...], w, preferred_element_type=F32)


def _in_proj(x, g, w, tm, tn):
    m = x.shape[0]
    return pl.pallas_call(
        _in_proj_body,
        grid=(m // tm, IN_COLS // tn),
        in_specs=[
            pl.BlockSpec((tm, D_MODEL), lambda i, j: (i, 0)),
            pl.BlockSpec((1, D_MODEL), lambda i, j: (0, 0)),
            pl.BlockSpec((D_MODEL, tn), lambda i, j: (0, j)),
        ],
        out_specs=[pl.BlockSpec((tm, tn), lambda i, j: (i, j)), pl.BlockSpec((D_MODEL, tn), lambda i, j: (0, j))],
        out_shape=[jax.ShapeDtypeStruct((m, IN_COLS), F32), jax.ShapeDtypeStruct(w.shape, BF16)],
        scratch_shapes=[pltpu.VMEM((tm, D_MODEL), BF16)],
        compiler_params=_cparams(("arbitrary", "arbitrary"), 48),
        name="in_proj",
    )(x, g, w)


def _lower_bound(lbp, layer):
    e = jnp.exp(lbp - jnp.max(lbp, axis=0, keepdims=True))
    return jnp.sum(e[: layer + 1], axis=0, keepdims=True) / jnp.sum(e, axis=0, keepdims=True)


def _pool_project(acc, cnt, u, gi, wpool_ref, pscale_ref, mix_ref):
    c0 = gi * POOL_GROUP_DIM
    pooled = acc / cnt - u
    out = jnp.dot(pooled.astype(BF16), wpool_ref[gi], preferred_element_type=F32)
    mix_ref[:, c0:c0 + POOL_GROUP_DIM] = (out * pscale_ref[:, c0:c0 + POOL_GROUP_DIM]).astype(BF16)


def _token_tile(nat_ref, a, t):
    return nat_ref[a, t // SUBLANES, pl.ds(t % SUBLANES, HEADS, stride=SUBLANES), :]


def _head_rows(lev_ref, lvl, j, rows):
    return lev_ref[lvl, pl.ds(j, rows, stride=HEADS), :]


def _head_gates(zq, zf, lb, c, rows, nat_ref, qk_ref):
    fg = lb + (1.0 - lb) * _sigmoid(zf)
    qs = zq * _sigmoid(zq)
    kk = 1.0 - fg
    qk_ref[0, :, c * HEAD_DIM:(c + 1) * HEAD_DIM] = qs.astype(BF16)
    qk_ref[1, :, c * HEAD_DIM:(c + 1) * HEAD_DIM] = kk.astype(BF16)
    for a, x in enumerate((qs, kk, fg)):
        nat_ref[a, :, c * SUBLANES:(c + 1) * SUBLANES, :] = x.reshape(rows // SUBLANES, SUBLANES, HEAD_DIM)


def _segment_offsets(rows, levels):
    off = [0]
    for lvl in range(levels):
        off.append(off[-1] + (rows >> lvl))
    return off


def _segment_products(rows, levels, nat_ref, seg_ref):
    off = _segment_offsets(rows, levels)
    for t in range(rows):
        seg_ref[t] = _token_tile(nat_ref, 2, t)
    for lvl in range(1, levels):
        for m in range(rows >> lvl):
            seg_ref[off[lvl] + m] = seg_ref[off[lvl - 1] + 2 * m] * seg_ref[off[lvl - 1] + 2 * m + 1]


def _token_levels(tokens, rows, levels, nat_ref, seg_ref, lev_ref):
    off = _segment_offsets(rows, levels)
    prefix = []
    for t in tokens:
        tok = slice(t * HEADS, (t + 1) * HEADS)
        q_t, k_t = _token_tile(nat_ref, 0, t), _token_tile(nat_ref, 1, t)
        p_t, r_t = seg_ref[t], None
        for lvl in range(levels):
            m = t >> lvl
            sibling = seg_ref[off[lvl] + (m ^ 1)]
            if m & 1:
                lev_ref[lvl, tok, :] = q_t * p_t
                p_t = p_t * sibling
            else:
                lev_ref[lvl, tok, :] = k_t if r_t is None else k_t * r_t
                r_t = sibling if r_t is None else r_t * sibling
        lev_ref[levels, tok, :] = q_t * p_t
        lev_ref[levels + 1, tok, :] = k_t if r_t is None else k_t * r_t
        prefix.append(p_t)
    return prefix


def _level_scratch(rows, levels):
    return [
        pltpu.VMEM((3, rows // SUBLANES, HEADS * SUBLANES, HEAD_DIM), F32),
        pltpu.VMEM((2 * rows, HEADS, HEAD_DIM), F32),
        pltpu.VMEM((2, rows, HG_WIDTH), BF16),
        pltpu.VMEM((levels + 2, rows * HEADS, HEAD_DIM), F32),
        pltpu.VMEM((HEADS, rows, rows), F32),
    ]


def _level_masks(rows, levels):
    ti = lax.broadcasted_iota(jnp.int32, (rows, rows), 0)
    si = lax.broadcasted_iota(jnp.int32, (rows, rows), 1)
    x = ti ^ si
    masks = []
    for lvl in range(levels):
        h = 1 << lvl
        if h < SUBLANES:
            masks.append(((x >> lvl) == 1) & (((ti >> lvl) & 1) == 1))
        else:
            half = lax.broadcasted_iota(jnp.int32, (h, rows), 1) >> lvl
            masks.append([half == 2 * b for b in range(rows // (2 * h))])
    return masks, ti == si


def _nt(a, b):
    return lax.dot_general(a, b, (((1,), (1,)), ((), ())), preferred_element_type=F32)


def _score_level(j, lvl, rows, masks, eye, qk_ref, lev_ref, sc_ref):
    c0 = j * HEAD_DIM
    sc_ref = sc_ref.at[j]
    if lvl < 0:
        pltpu.store(sc_ref, _nt(qk_ref[0, :, c0:c0 + HEAD_DIM], qk_ref[1, :, c0:c0 + HEAD_DIM]), mask=eye)
        return
    h = 1 << lvl
    x32 = _head_rows(lev_ref, lvl, j, rows)
    x = x32.astype(BF16)
    if h < SUBLANES:
        pltpu.store(sc_ref, _nt(x, x), mask=masks[lvl])
        return
    blocks = rows // (2 * h)
    upper = x32.reshape(blocks, 2 * h, HEAD_DIM)[:, h:].reshape(rows // 2, HEAD_DIM).astype(BF16)
    s = _nt(upper, x)
    for b in range(blocks):
        pltpu.store(sc_ref.at[b * 2 * h + h:(b + 1) * 2 * h, :], s[b * h:(b + 1) * h], mask=masks[lvl][b])


def _intra_scores(j, rows, levels, masks, eye, qk_ref, lev_ref, sc_ref):
    for lvl in range(-1, levels):
        _score_level(j, lvl, rows, masks, eye, qk_ref, lev_ref, sc_ref)
    return sc_ref[j]


def _window_sums(e, w, axis, hist, rows):
    s, d = e, 1
    while d < min(w, SUBLANES):
        s = s + pltpu.roll(s, d, axis)
        d *= 2
    take = lambda a, lo: lax.slice_in_dim(a, lo, lo + rows, axis=axis)
    out = take(s, hist)
    if w > SUBLANES:
        assert w == 2 * SUBLANES
        out = out + take(s, hist - SUBLANES)
    return out


def _head_output(o, j, zg, hgn_ref, mix_ref):
    c0 = j * HEAD_DIM
    o = _rms(o, hgn_ref[:, c0:c0 + HEAD_DIM]) * (zg * _sigmoid(zg))
    mix_ref[:, POOL_WIDTH + c0:POOL_WIDTH + c0 + HEAD_DIM] = o.astype(BF16)


PROMPT_ROWS = 128
PROMPT_LEVELS = 7


PROMPT_SLOTS = 2


def _prompt_prepare(z_blk, t_blk, starts_seq, slot, lb, wpool_ref, pscale_ref,
                    ext_ref, pool_ref, decay_ref, nat_ref, seg_ref, qk_ref, lev_ref):
    rows, levels, hist = PROMPT_ROWS, PROMPT_LEVELS, HIST_ROWS
    nat, seg, qk, lev = nat_ref[slot], seg_ref[slot], qk_ref[slot], lev_ref[slot]

    def pool_group(gi):
        def run():
            if gi == 0:
                ext_ref[0:hist, :] = jnp.where(starts_seq, 0.0, ext_ref[rows:rows + hist, :])
                ext_ref[hist:hist + rows, :] = z_blk[:, 0:POOL_WIDTH]
            w, c0 = POOL_WINDOWS[gi], gi * POOL_GROUP_DIM
            e = ext_ref[:, c0:c0 + POOL_GROUP_DIM]
            pos = t_blk * rows + lax.broadcasted_iota(jnp.int32, (rows, 1), 0)
            cnt = jnp.minimum(pos + 1, w).astype(F32)
            _pool_project(_window_sums(e, w, 0, hist, rows), cnt, e[hist:hist + rows], gi,
                          wpool_ref, pscale_ref, pool_ref[slot])
        return run

    def gates(c):
        cols = slice(c * HEAD_DIM, (c + 1) * HEAD_DIM)
        return lambda: _head_gates(z_blk[:, Z_Q + c * HEAD_DIM:Z_Q + (c + 1) * HEAD_DIM],
                                   z_blk[:, Z_F + c * HEAD_DIM:Z_F + (c + 1) * HEAD_DIM], lb[:, cols], c, rows, nat, qk)

    def tokens(k, n):
        def run():
            prefix = _token_levels(range(k * n, (k + 1) * n), rows, levels, nat, seg, lev)
            if (k + 1) * n == rows:
                decay_ref[slot][...] = prefix[-1]
        return run

    a, b = [pool_group(g) for g in range(len(POOL_WINDOWS))], [gates(c) for c in range(HEADS)]
    t = [tokens(k, rows // HEADS) for k in range(HEADS)]
    segments = lambda: _segment_products(rows, levels, nat, seg)
    return [[a[0], a[1], b[0]], [a[2], a[3], b[1]], b[2:5], b[5:8], [segments, t[0]], t[1:3], t[3:6], t[6:8]]


def _prompt_heads(slot, z_blk, mix_blk, hgn_ref, s_ref, pool_ref, decay_ref, qk_ref, lev_ref, sc_ref,
                  masks, eye):
    rows, levels = PROMPT_ROWS, PROMPT_LEVELS
    qk, lev, sc = qk_ref[slot], lev_ref[slot], sc_ref[slot]
    value = lambda j: z_blk[:, Z_I + j * HEAD_DIM:Z_I + (j + 1) * HEAD_DIM].astype(BF16)

    def pool_part():
        mix_blk[:, 0:POOL_WIDTH] = pool_ref[slot][...]

    def output(j):
        c0 = j * HEAD_DIM
        qt = _head_rows(lev, levels, j, rows).astype(BF16)
        o = (jnp.dot(sc[j].astype(BF16), value(j), preferred_element_type=F32)
             + jnp.dot(qt, s_ref[j].astype(BF16), preferred_element_type=F32))
        _head_output(o, j, z_blk[:, Z_G + c0:Z_G + c0 + HEAD_DIM], hgn_ref, mix_blk)

    def state(j):
        decay_t = jnp.transpose(decay_ref[slot][...])
        kt = _head_rows(lev, levels + 1, j, rows).astype(BF16)
        s_ref[j] = decay_t[:, j:j + 1] * s_ref[j] + lax.dot_general(
            kt, value(j), (((0,), (0,)), ((), ())), preferred_element_type=F32)

    heads = range(HEADS)
    scores = [functools.partial(_score_level, j, lvl, rows, masks, eye, qk, lev, sc)
              for lvl in range(-1, levels) for j in heads]
    return ([pool_part] + scores + [functools.partial(output, j) for j in heads]
            + [functools.partial(state, j) for j in heads])


def _prompt_layer_body(xc_ref, xn_ref, g1_ref, win_ref, wo_ref, lbp_ref, wpool_ref, pscale_ref, hgn_ref,
                       x1_ref, pooln_ref, snew_ref,
                       z_even, z_odd, mix_ref, ext_ref, s_ref, pool_ref, decay_ref, nat_ref, seg_ref, qk_ref,
                       lev_ref, sc_ref, *, layer, nt):
    rows, hist = PROMPT_ROWS, HIST_ROWS
    i = pl.program_id(0)
    steps_per_seq = nt // PROMPT_SLOTS
    tt = i % steps_per_seq
    lb = _lower_bound(lbp_ref[...], layer)
    one = lambda ref: [ref]
    prepare = functools.partial(
        _prompt_prepare, lb=lb, wpool_ref=wpool_ref, pscale_ref=pscale_ref, ext_ref=ext_ref,
        pool_ref=one(pool_ref), decay_ref=one(decay_ref), nat_ref=one(nat_ref), seg_ref=one(seg_ref),
        qk_ref=one(qk_ref), lev_ref=one(lev_ref))
    masks, eye = _level_masks(rows, PROMPT_LEVELS)
    heads = functools.partial(
        _prompt_heads, hgn_ref=hgn_ref, s_ref=s_ref, pool_ref=one(pool_ref), decay_ref=one(decay_ref),
        qk_ref=one(qk_ref), lev_ref=one(lev_ref), sc_ref=one(sc_ref), masks=masks, eye=eye)

    def in_proj(x_rows, z_ref):
        h = _rms(x_rows[...], g1_ref[...]).astype(BF16)
        z_ref[...] = jnp.dot(h, win_ref[...], preferred_element_type=F32)

    def mix_block(z_ref, t_blk, starts_seq, x_rows, out_rows):
        for batch in prepare(z_ref, t_blk, starts_seq, 0):
            for item in batch:
                item()
        for head in heads(0, z_ref, mix_ref):
            head()
        out_rows[...] = x_rows[...] + jnp.dot(mix_ref[...], wo_ref[...], preferred_element_type=F32)

    @pl.when(i == 0)
    def _():
        sc_ref[...] = jnp.zeros_like(sc_ref)
        ext_ref[...] = jnp.zeros_like(ext_ref)
        in_proj(xc_ref.at[0:rows], z_even)

    @pl.when(tt == 0)
    def _():
        s_ref[...] = jnp.zeros_like(s_ref)

    in_proj(xc_ref.at[rows:2 * rows], z_odd)
    mix_block(z_even, PROMPT_SLOTS * tt, tt == 0, xc_ref.at[0:rows], x1_ref.at[0:rows])
    in_proj(xn_ref, z_even)
    mix_block(z_odd, PROMPT_SLOTS * tt + 1, False, xc_ref.at[rows:2 * rows], x1_ref.at[rows:2 * rows])
    pooln_ref[0] = ext_ref[hist + rows - POOL_CACHE:hist + rows, :]

    @pl.when(tt == steps_per_seq - 1)
    def _():
        snew_ref[0] = s_ref[...]


def _prompt_layer(x, g1, w_in, w_o, lbp, wpool, pscale, hgn, batch, seq, layer):
    rows, slots = PROMPT_ROWS, PROMPT_SLOTS
    nt = seq // rows
    assert seq % (rows * slots) == 0
    nblk = batch * nt
    steps_per_seq = nt // slots
    const2 = lambda i: (0, 0)
    resident = lambda shape: pl.BlockSpec(shape, const2, pipeline_mode=pl.Buffered(1))
    return pl.pallas_call(
        functools.partial(_prompt_layer_body, layer=layer, nt=nt),
        grid=(nblk // slots,),
        in_specs=[
            pl.BlockSpec((slots * rows, D_MODEL), lambda i: (i, 0)),
            pl.BlockSpec((rows, D_MODEL), lambda i: (jnp.minimum(slots * i + slots, nblk - 1), 0)),
            pl.BlockSpec((1, D_MODEL), const2),
            resident(w_in.shape),
            resident(w_o.shape),
            pl.BlockSpec(lbp.shape, const2),
            pl.BlockSpec(wpool.shape, lambda i: (0, 0, 0)),
            pl.BlockSpec((1, POOL_WIDTH), const2),
            pl.BlockSpec((1, HG_WIDTH), const2),
        ],
        out_specs=[
            pl.BlockSpec((slots * rows, D_MODEL), lambda i: (i, 0)),
            pl.BlockSpec((1, POOL_CACHE, POOL_WIDTH), lambda i: (i // steps_per_seq, 0, 0)),
            pl.BlockSpec((1, HEADS, HEAD_DIM, HEAD_DIM), lambda i: (i // steps_per_seq, 0, 0, 0)),
        ],
        out_shape=[
            jax.ShapeDtypeStruct((batch * seq, D_MODEL), F32),
            jax.ShapeDtypeStruct((batch, POOL_CACHE, POOL_WIDTH), F32),
            jax.ShapeDtypeStruct((batch, HEADS, HEAD_DIM, HEAD_DIM), F32),
        ],
        scratch_shapes=[
            pltpu.VMEM((rows, IN_COLS), F32),
            pltpu.VMEM((rows, IN_COLS), F32),
            pltpu.VMEM((rows, D_MODEL), BF16),
            pltpu.VMEM((HIST_ROWS + rows, POOL_WIDTH), F32),
            pltpu.VMEM((HEADS, HEAD_DIM, HEAD_DIM), F32),
            pltpu.VMEM((rows, POOL_WIDTH), BF16),
            pltpu.VMEM((HEADS, HEAD_DIM), F32),
            *_level_scratch(rows, PROMPT_LEVELS),
        ],
        compiler_params=_cparams(("arbitrary",), 58),
        name="prompt_layer",
    )(x, x, g1, w_in, w_o, lbp, wpool, pscale, hgn)


def _mixer_sample_body(z_ref, cache_ref, sin_ref, lbp_ref, wpool_ref, pscale_ref, hgn_ref,
                       mix_ref, pooln_ref, snew_ref,
                       ext_ref, nat_ref, seg_ref, qk_ref, lev_ref, sc_ref, *, layer, nseq, steps, pos0):
    rows = nseq * steps
    levels = steps.bit_length() - 1
    hist = HIST_ROWS

    @pl.when(pl.program_id(0) == 0)
    def _():
        sc_ref[...] = jnp.zeros_like(sc_ref)

    ext_ref[:, 0:hist - POOL_CACHE, :] = jnp.zeros((nseq, hist - POOL_CACHE, POOL_WIDTH), F32)
    ext_ref[:, hist - POOL_CACHE:hist, :] = cache_ref[0]
    ext_ref[:, hist:hist + steps, :] = z_ref[:, 0:POOL_WIDTH].reshape(nseq, steps, POOL_WIDTH)
    pos = pos0 + lax.broadcasted_iota(jnp.int32, (1, steps, 1), 1)
    for gi, w in enumerate(POOL_WINDOWS):
        c0 = gi * POOL_GROUP_DIM
        e = ext_ref[:, :, c0:c0 + POOL_GROUP_DIM]
        cnt = jnp.minimum(pos + 1, w).astype(F32)
        mean = (_window_sums(e, w, 1, hist, steps) / cnt).reshape(rows, POOL_GROUP_DIM)
        u = e[:, hist:hist + steps].reshape(rows, POOL_GROUP_DIM)
        _pool_project(mean, 1.0, u, gi, wpool_ref, pscale_ref, mix_ref)
    pooln_ref[...] = ext_ref[:, hist + steps - POOL_CACHE:hist + steps, :]

    lb = _lower_bound(lbp_ref[...], layer)
    for c in range(HEADS):
        cols = slice(c * HEAD_DIM, (c + 1) * HEAD_DIM)
        _head_gates(z_ref[:, Z_Q + c * HEAD_DIM:Z_Q + (c + 1) * HEAD_DIM],
                    z_ref[:, Z_F + c * HEAD_DIM:Z_F + (c + 1) * HEAD_DIM], lb[:, cols], c, rows, nat_ref, qk_ref)
    _segment_products(rows, levels, nat_ref, seg_ref)
    prefix = _token_levels(range(rows), rows, levels, nat_ref, seg_ref, lev_ref)
    decay = jnp.stack(prefix[steps - 1::steps])
    decay_t = jnp.swapaxes(decay, 1, 2)
    masks, eye = _level_masks(rows, levels)
    for j in range(HEADS):
        c0 = j * HEAD_DIM
        scores = _intra_scores(j, rows, levels, masks, eye, qk_ref, lev_ref, sc_ref)
        v32 = z_ref[:, Z_I + c0:Z_I + c0 + HEAD_DIM]
        v = v32.astype(BF16)
        v3 = v32.reshape(nseq, steps, HEAD_DIM).astype(BF16)
        qt3 = _head_rows(lev_ref, levels, j, rows).reshape(nseq, steps, HEAD_DIM).astype(BF16)
        kt3 = _head_rows(lev_ref, levels + 1, j, rows).reshape(nseq, steps, HEAD_DIM).astype(BF16)
        s_in = sin_ref[:, j]
        o_state = lax.dot_general(qt3, s_in.astype(BF16), (((2,), (1,)), ((0,), (0,))),
                                  preferred_element_type=F32)
        o = jnp.dot(scores.astype(BF16), v, preferred_element_type=F32) + o_state.reshape(rows, HEAD_DIM)
        upd = lax.dot_general(kt3, v3, (((1,), (1,)), ((0,), (0,))), preferred_element_type=F32)
        snew_ref[:, j] = decay_t[:, :, j:j + 1] * s_in + upd
        _head_output(o, j, z_ref[:, Z_G + c0:Z_G + c0 + HEAD_DIM], hgn_ref, mix_ref)


def _mixer_sample(z, cache, state, lbp, wpool, pscale, hgn, nseq_total, steps, pos0, layer, nseq):
    rows = nseq * steps
    levels = steps.bit_length() - 1
    assert 1 << levels == steps and nseq_total % nseq == 0
    hist = HIST_ROWS
    const2 = lambda i: (0, 0)
    return pl.pallas_call(
        functools.partial(_mixer_sample_body, layer=layer, nseq=nseq, steps=steps, pos0=pos0),
        grid=(nseq_total // nseq,),
        in_specs=[
            pl.BlockSpec((rows, IN_COLS), lambda i: (i, 0)),
            pl.BlockSpec((1, nseq, POOL_CACHE, POOL_WIDTH), lambda i: (layer, i, 0, 0)),
            pl.BlockSpec((nseq, HEADS, HEAD_DIM, HEAD_DIM), lambda i: (i, 0, 0, 0)),
            pl.BlockSpec(lbp.shape, const2),
            pl.BlockSpec(wpool.shape, lambda i: (0, 0, 0)),
            pl.BlockSpec((1, POOL_WIDTH), const2),
            pl.BlockSpec((1, HG_WIDTH), const2),
        ],
        out_specs=[
            pl.BlockSpec((rows, D_MODEL), lambda i: (i, 0)),
            pl.BlockSpec((nseq, POOL_CACHE, POOL_WIDTH), lambda i: (i, 0, 0)),
            pl.BlockSpec((nseq, HEADS, HEAD_DIM, HEAD_DIM), lambda i: (i, 0, 0, 0)),
        ],
        out_shape=[
            jax.ShapeDtypeStruct((nseq_total * steps, D_MODEL), BF16),
            jax.ShapeDtypeStruct((nseq_total, POOL_CACHE, POOL_WIDTH), F32),
            jax.ShapeDtypeStruct((nseq_total, HEADS, HEAD_DIM, HEAD_DIM), F32),
        ],
        scratch_shapes=[
            pltpu.VMEM((nseq, hist + steps, POOL_WIDTH), F32),
            *_level_scratch(rows, levels),
        ],
        compiler_params=_cparams(("arbitrary",), 48),
        name="mixer_sample",
    )(z, cache, state, lbp, wpool, pscale, hgn)


def _out_proj_body(x_ref, mix_ref, w_ref, o_ref, wb_ref):
    w = w_ref[...].astype(BF16)
    wb_ref[...] = w
    o_ref[...] = x_ref[...] + jnp.dot(mix_ref[...], w, preferred_element_type=F32)


def _out_proj(x, mix, w, tm, tn):
    m = x.shape[0]
    return pl.pallas_call(
        _out_proj_body,
        grid=(m // tm, D_MODEL // tn),
        in_specs=[
            pl.BlockSpec((tm, tn), lambda i, j: (i, j)),
            pl.BlockSpec((tm, D_MODEL), lambda i, j: (i, 0)),
            pl.BlockSpec((D_MODEL, tn), lambda i, j: (0, j)),
        ],
        out_specs=[pl.BlockSpec((tm, tn), lambda i, j: (i, j)), pl.BlockSpec((D_MODEL, tn), lambda i, j: (0, j))],
        out_shape=[jax.ShapeDtypeStruct((m, D_MODEL), F32), jax.ShapeDtypeStruct(w.shape, BF16)],
        compiler_params=_cparams(("arbitrary", "arbitrary"), 48),
        name="out_proj",
    )(x, mix, w)


def _ffn_body(x_ref, g2_ref, wg_ref, wu_ref, wd_ref, gf_ref, y_ref, *rest):
    *wb_refs, h_ref = rest
    j = pl.program_id(1)
    last = pl.num_programs(1) - 1
    chunks = _row_chunks(x_ref.shape[0])

    def down(h, wg, wu, wd):
        gate = jnp.dot(h, wg, preferred_element_type=F32)
        up = jnp.dot(h, wu, preferred_element_type=F32)
        act = (gate * _sigmoid(gate) * up).astype(BF16)
        return jnp.dot(act, wd, preferred_element_type=F32)

    def weights():
        if not wb_refs:
            return wg_ref[...], wu_ref[...], wd_ref[...]
        w = tuple(ref[...].astype(BF16) for ref in (wg_ref, wu_ref, wd_ref))
        for wb_ref, wb in zip(wb_refs, w):
            wb_ref[...] = wb
        return w

    @pl.when(j == 0)
    def _():
        w = weights()
        for rs in chunks:
            h = _rms(x_ref[rs, :], g2_ref[...]).astype(BF16)
            h_ref[rs, :] = h
            y_ref[rs, :] = down(h, *w)

    @pl.when((j > 0) & (j < last))
    def _():
        y_ref[...] += down(h_ref[...], *weights())

    @pl.when(j == last)
    def _():
        w = weights()
        for rs in chunks:
            y_ref[rs, :] = _rms(x_ref[rs, :] + y_ref[rs, :] + down(h_ref[rs, :], *w), gf_ref[...])


def _ffn(x, g2, wg, wu, wd, gf, tm, tf):
    m = x.shape[0]
    d_ff = wg.shape[1]
    emit = wg.dtype != BF16
    w_specs = [
        pl.BlockSpec((D_MODEL, tf), lambda i, j: (0, j)),
        pl.BlockSpec((D_MODEL, tf), lambda i, j: (0, j)),
        pl.BlockSpec((tf, D_MODEL), lambda i, j: (j, 0)),
    ]
    x_mode = dict(pipeline_mode=pl.Buffered(1)) if m == tm else {}
    return pl.pallas_call(
        _ffn_body,
        grid=(m // tm, d_ff // tf),
        in_specs=[
            pl.BlockSpec((tm, D_MODEL), lambda i, j: (i, 0), **x_mode),
            pl.BlockSpec((1, D_MODEL), lambda i, j: (0, 0)),
            *w_specs,
            pl.BlockSpec((1, D_MODEL), lambda i, j: (0, 0)),
        ],
        out_specs=[pl.BlockSpec((tm, D_MODEL), lambda i, j: (i, 0))] + (w_specs if emit else []),
        out_shape=[jax.ShapeDtypeStruct((m, D_MODEL), F32)]
        + ([jax.ShapeDtypeStruct(a.shape, BF16) for a in (wg, wu, wd)] if emit else []),
        scratch_shapes=[pltpu.VMEM((tm, D_MODEL), BF16)],
        compiler_params=_cparams(("arbitrary", "arbitrary"), 58),
        name="ffn",
    )(x, g2, wg, wu, wd, gf)


def kernel(x_prompt, x_sample, cache_pool, state_hgrn, lb_param, norm1, w_in, w_pool, pool_scale,
           hg_norm, w_o, norm2, w_gate, w_up, w_down, norm_f):
    depth = w_in.shape[0]
    assert depth == 1, "single-layer trunk"
    layer = 0
    batch, seq, _ = x_prompt.shape
    dec_batch, dec_seq, _ = x_sample.shape

    row = lambda a: a.reshape(1, -1).astype(F32)
    g1, g2, gf = row(norm1[layer]), row(norm2[layer]), row(norm_f)
    lbp = lb_param.astype(F32)
    wpool = w_pool[layer].astype(BF16)
    pscale = row(pool_scale[layer])
    hgn = row(hg_norm[layer])

    m_s = dec_batch * dec_seq
    xs = x_sample.reshape(m_s, D_MODEL)
    z_s, w_in_b = _in_proj(xs, g1, w_in[layer], tm=m_s, tn=512)
    mix_s, pool_s, s_s = _mixer_sample(z_s, cache_pool, state_hgrn[layer], lbp, wpool, pscale, hgn,
                                       dec_batch, dec_seq, PAST_LEN, layer, nseq=8)
    x1_s, w_o_b = _out_proj(xs, mix_s, w_o[layer], tm=m_s, tn=512)
    y_s, w_gate_b, w_up_b, w_down_b = _ffn(x1_s, g2, w_gate[layer], w_up[layer], w_down[layer], gf, tm=m_s, tf=256)

    x1_p, pool_p, s_p = _prompt_layer(x_prompt.reshape(batch * seq, D_MODEL), g1, w_in_b, w_o_b,
                                      lbp, wpool, pscale, hgn, batch, seq, layer)
    (y_p,) = _ffn(x1_p, g2, w_gate_b, w_up_b, w_down_b, gf, tm=1024, tf=512)

    return (y_p.reshape(batch, seq, D_MODEL), y_s.reshape(dec_batch, dec_seq, D_MODEL),
            pool_p[None], s_p[None], pool_s[None], s_s[None])
```

```python
import functools

import jax
import jax.numpy as jnp
from jax import lax
from jax.experimental import pallas as pl
from jax.experimental.pallas import tpu as pltpu

D_MODEL = 2048
POOL_WIDTH = 1024
POOL_WINDOWS = (2, 4, 8, 16)
POOL_GROUP_DIM = POOL_WIDTH // len(POOL_WINDOWS)
POOL_CACHE = max(POOL_WINDOWS) - 1
HG_WIDTH = 1024
HEAD_DIM = 128
HEADS = HG_WIDTH // HEAD_DIM
IN_COLS = POOL_WIDTH + 4 * HG_WIDTH
Z_Q, Z_F, Z_I, Z_G = (POOL_WIDTH + n * HG_WIDTH for n in range(4))
EPS = 1e-6
SUBLANES = 8
assert HEADS == SUBLANES
HIST_ROWS = 2 * SUBLANES
PAST_LEN = 16384

F32 = jnp.float32
BF16 = jnp.bfloat16

MIB = 1024 * 1024


def _cparams(sem, vmem_mib):
    return pltpu.CompilerParams(dimension_semantics=sem, vmem_limit_bytes=vmem_mib * MIB)


def _rms(x, g):
    return x * lax.rsqrt(jnp.mean(x * x, axis=-1, keepdims=True) + EPS) * g


def _sigmoid(x):
    return 1.0 / (1.0 + jnp.exp(-x))


ROW_CHUNK = 256


def _row_chunks(n):
    return [slice(r, r + ROW_CHUNK) for r in range(0, n, ROW_CHUNK)]


def _in_proj_body(x_ref, g_ref, w_ref, z_ref, wb_ref, h_ref):
    j = pl.program_id(1)
    w = w_ref[...].astype(BF16)
    wb_ref[...] = w

    @pl.when(j == 0)
    def _():
        for rs in _row_chunks(x_ref.shape[0]):
            h = _rms(x_ref[rs, :], g_ref[...]).astype(BF16)
            h_ref[rs, :] = h
            z_ref[rs, :] = jnp.dot(h, w, preferred_element_type=F32)

    @pl.when(j > 0)
    def _():
        z_ref[...] = jnp.dot(h_ref[...], w, preferred_element_type=F32)


def _in_proj(x, g, w, tm, tn):
    m = x.shape[0]
    return pl.pallas_call(
        _in_proj_body,
        grid=(m // tm, IN_COLS // tn),
        in_specs=[
            pl.BlockSpec((tm, D_MODEL), lambda i, j: (i, 0)),
            pl.BlockSpec((1, D_MODEL), lambda i, j: (0, 0)),
            pl.BlockSpec((D_MODEL, tn), lambda i, j: (0, j)),
        ],
        out_specs=[pl.BlockSpec((tm, tn), lambda i, j: (i, j)), pl.BlockSpec((D_MODEL, tn), lambda i, j: (0, j))],
        out_shape=[jax.ShapeDtypeStruct((m, IN_COLS), F32), jax.ShapeDtypeStruct(w.shape, BF16)],
        scratch_shapes=[pltpu.VMEM((tm, D_MODEL), BF16)],
        compiler_params=_cparams(("arbitrary", "arbitrary"), 48),
        name="in_proj",
    )(x, g, w)


def _lower_bound(lbp, layer):
    e = jnp.exp(lbp - jnp.max(lbp, axis=0, keepdims=True))
    return jnp.sum(e[: layer + 1], axis=0, keepdims=True) / jnp.sum(e, axis=0, keepdims=True)


def _pool_project(acc, cnt, u, gi, wpool_ref, pscale_ref, mix_ref):
    c0 = gi * POOL_GROUP_DIM
    pooled = acc / cnt - u
    out = jnp.dot(pooled.astype(BF16), wpool_ref[gi], preferred_element_type=F32)
    mix_ref[:, c0:c0 + POOL_GROUP_DIM] = (out * pscale_ref[:, c0:c0 + POOL_GROUP_DIM]).astype(BF16)


def _token_tile(nat_ref, a, t):
    return nat_ref[a, t // SUBLANES, pl.ds(t % SUBLANES, HEADS, stride=SUBLANES), :]


def _head_rows(lev_ref, lvl, j, rows):
    return lev_ref[lvl, pl.ds(j, rows, stride=HEADS), :]


def _head_gates(zq, zf, lb, c, rows, nat_ref, qk_ref):
    fg = lb + (1.0 - lb) * _sigmoid(zf)
    qs = zq * _sigmoid(zq)
    kk = 1.0 - fg
    qk_ref[0, :, c * HEAD_DIM:(c + 1) * HEAD_DIM] = qs.astype(BF16)
    qk_ref[1, :, c * HEAD_DIM:(c + 1) * HEAD_DIM] = kk.astype(BF16)
    for a, x in enumerate((qs, kk, fg)):
        nat_ref[a, :, c * SUBLANES:(c + 1) * SUBLANES, :] = x.reshape(rows // SUBLANES, SUBLANES, HEAD_DIM)


def _segment_offsets(rows, levels):
    off = [0]
    for lvl in range(levels):
        off.append(off[-1] + (rows >> lvl))
    return off


def _segment_products(rows, levels, nat_ref, seg_ref):
    off = _segment_offsets(rows, levels)
    for t in range(rows):
        seg_ref[t] = _token_tile(nat_ref, 2, t)
    for lvl in range(1, levels):
        for m in range(rows >> lvl):
            seg_ref[off[lvl] + m] = seg_ref[off[lvl - 1] + 2 * m] * seg_ref[off[lvl - 1] + 2 * m + 1]


def _token_levels(tokens, rows, levels, nat_ref, seg_ref, lev_ref):
    off = _segment_offsets(rows, levels)
    prefix = []
    for t in tokens:
        tok = slice(t * HEADS, (t + 1) * HEADS)
        q_t, k_t = _token_tile(nat_ref, 0, t), _token_tile(nat_ref, 1, t)
        p_t, r_t = seg_ref[t], None
        for lvl in range(levels):
            m = t >> lvl
            sibling = seg_ref[off[lvl] + (m ^ 1)]
            if m & 1:
                lev_ref[lvl, tok, :] = q_t * p_t
                p_t = p_t * sibling
            else:
                lev_ref[lvl, tok, :] = k_t if r_t is None else k_t * r_t
                r_t = sibling if r_t is None else r_t * sibling
        lev_ref[levels, tok, :] = q_t * p_t
        lev_ref[levels + 1, tok, :] = k_t if r_t is None else k_t * r_t
        prefix.append(p_t)
    return prefix


def _level_scratch(rows, levels):
    return [
        pltpu.VMEM((3, rows // SUBLANES, HEADS * SUBLANES, HEAD_DIM), F32),
        pltpu.VMEM((2 * rows, HEADS, HEAD_DIM), F32),
        pltpu.VMEM((2, rows, HG_WIDTH), BF16),
        pltpu.VMEM((levels + 2, rows * HEADS, HEAD_DIM), F32),
        pltpu.VMEM((HEADS, rows, rows), F32),
    ]


def _level_masks(rows, levels):
    ti = lax.broadcasted_iota(jnp.int32, (rows, rows), 0)
    si = lax.broadcasted_iota(jnp.int32, (rows, rows), 1)
    x = ti ^ si
    masks = []
    for lvl in range(levels):
        h = 1 << lvl
        if h < SUBLANES:
            masks.append(((x >> lvl) == 1) & (((ti >> lvl) & 1) == 1))
        else:
            half = lax.broadcasted_iota(jnp.int32, (h, rows), 1) >> lvl
            masks.append([half == 2 * b for b in range(rows // (2 * h))])
    return masks, ti == si


def _nt(a, b):
    return lax.dot_general(a, b, (((1,), (1,)), ((), ())), preferred_element_type=F32)


def _score_level(j, lvl, rows, masks, eye, qk_ref, lev_ref, sc_ref):
    c0 = j * HEAD_DIM
    sc_ref = sc_ref.at[j]
    if lvl < 0:
        pltpu.store(sc_ref, _nt(qk_ref[0, :, c0:c0 + HEAD_DIM], qk_ref[1, :, c0:c0 + HEAD_DIM]), mask=eye)
        return
    h = 1 << lvl
    x32 = _head_rows(lev_ref, lvl, j, rows)
    x = x32.astype(BF16)
    if h < SUBLANES:
        pltpu.store(sc_ref, _nt(x, x), mask=masks[lvl])
        return
    blocks = rows // (2 * h)
    upper = x32.reshape(blocks, 2 * h, HEAD_DIM)[:, h:].reshape(rows // 2, HEAD_DIM).astype(BF16)
    s = _nt(upper, x)
    for b in range(blocks):
        pltpu.store(sc_ref.at[b * 2 * h + h:(b + 1) * 2 * h, :], s[b * h:(b + 1) * h], mask=masks[lvl][b])


def _intra_scores(j, rows, levels, masks, eye, qk_ref, lev_ref, sc_ref):
    for lvl in range(-1, levels):
        _score_level(j, lvl, rows, masks, eye, qk_ref, lev_ref, sc_ref)
    return sc_ref[j]


def _window_sums(e, w, axis, hist, rows):
    s, d = e, 1
    while d < min(w, SUBLANES):
        s = s + pltpu.roll(s, d, axis)
        d *= 2
    take = lambda a, lo: lax.slice_in_dim(a, lo, lo + rows, axis=axis)
    out = take(s, hist)
    if w > SUBLANES:
        assert w == 2 * SUBLANES
        out = out + take(s, hist - SUBLANES)
    return out


def _head_output(o, j, zg, hgn_ref, mix_ref):
    c0 = j * HEAD_DIM
    o = _rms(o, hgn_ref[:, c0:c0 + HEAD_DIM]) * (zg * _sigmoid(zg))
    mix_ref[:, POOL_WIDTH + c0:POOL_WIDTH + c0 + HEAD_DIM] = o.astype(BF16)


PROMPT_ROWS = 128
PROMPT_LEVELS = 7


PROMPT_SLOTS = 2


def _prompt_prepare(z_blk, t_blk, starts_seq, slot, lb, wpool_ref, pscale_ref,
                    ext_ref, pool_ref, decay_ref, nat_ref, seg_ref, qk_ref, lev_ref):
    rows, levels, hist = PROMPT_ROWS, PROMPT_LEVELS, HIST_ROWS
    nat, seg, qk, lev = nat_ref[slot], seg_ref[slot], qk_ref[slot], lev_ref[slot]

    def pool_group(gi):
        def run():
            if gi == 0:
                ext_ref[0:hist, :] = jnp.where(starts_seq, 0.0, ext_ref[rows:rows + hist, :])
                ext_ref[hist:hist + rows, :] = z_blk[:, 0:POOL_WIDTH]
            w, c0 = POOL_WINDOWS[gi], gi * POOL_GROUP_DIM
            e = ext_ref[:, c0:c0 + POOL_GROUP_DIM]
            pos = t_blk * rows + lax.broadcasted_iota(jnp.int32, (rows, 1), 0)
            cnt = jnp.minimum(pos + 1, w).astype(F32)
            _pool_project(_window_sums(e, w, 0, hist, rows), cnt, e[hist:hist + rows], gi,
                          wpool_ref, pscale_ref, pool_ref[slot])
        return run

    def gates(c):
        cols = slice(c * HEAD_DIM, (c + 1) * HEAD_DIM)
        return lambda: _head_gates(z_blk[:, Z_Q + c * HEAD_DIM:Z_Q + (c + 1) * HEAD_DIM],
                                   z_blk[:, Z_F + c * HEAD_DIM:Z_F + (c + 1) * HEAD_DIM], lb[:, cols], c, rows, nat, qk)

    def tokens(k, n):
        def run():
            prefix = _token_levels(range(k * n, (k + 1) * n), rows, levels, nat, seg, lev)
            if (k + 1) * n == rows:
                decay_ref[slot][...] = prefix[-1]
        return run

    a, b = [pool_group(g) for g in range(len(POOL_WINDOWS))], [gates(c) for c in range(HEADS)]
    t = [tokens(k, rows // HEADS) for k in range(HEADS)]
    segments = lambda: _segment_products(rows, levels, nat, seg)
    return [[a[0], a[1], b[0]], [a[2], a[3], b[1]], b[2:5], b[5:8], [segments, t[0]], t[1:3], t[3:6], t[6:8]]


def _prompt_heads(slot, z_blk, mix_blk, hgn_ref, s_ref, pool_ref, decay_ref, qk_ref, lev_ref, sc_ref,
                  masks, eye):
    rows, levels = PROMPT_ROWS, PROMPT_LEVELS
    qk, lev, sc = qk_ref[slot], lev_ref[slot], sc_ref[slot]
    value = lambda j: z_blk[:, Z_I + j * HEAD_DIM:Z_I + (j + 1) * HEAD_DIM].astype(BF16)

    def pool_part():
        mix_blk[:, 0:POOL_WIDTH] = pool_ref[slot][...]

    def output(j):
        c0 = j * HEAD_DIM
        qt = _head_rows(lev, levels, j, rows).astype(BF16)
        o = (jnp.dot(sc[j].astype(BF16), value(j), preferred_element_type=F32)
             + jnp.dot(qt, s_ref[j].astype(BF16), preferred_element_type=F32))
        _head_output(o, j, z_blk[:, Z_G + c0:Z_G + c0 + HEAD_DIM], hgn_ref, mix_blk)

    def state(j):
        decay_t = jnp.transpose(decay_ref[slot][...])
        kt = _head_rows(lev, levels + 1, j, rows).astype(BF16)
        s_ref[j] = decay_t[:, j:j + 1] * s_ref[j] + lax.dot_general(
            kt, value(j), (((0,), (0,)), ((), ())), preferred_element_type=F32)

    heads = range(HEADS)
    scores = [functools.partial(_score_level, j, lvl, rows, masks, eye, qk, lev, sc)
              for lvl in range(-1, levels) for j in heads]
    return ([pool_part] + scores + [functools.partial(output, j) for j in heads]
            + [functools.partial(state, j) for j in heads])


def _prompt_layer_body(xc_ref, xn_ref, g1_ref, win_ref, wo_ref, lbp_ref, wpool_ref, pscale_ref, hgn_ref,
                       x1_ref, pooln_ref, snew_ref,
                       z_even, z_odd, mix_ref, ext_ref, s_ref, pool_ref, decay_ref, nat_ref, seg_ref, qk_ref,
                       lev_ref, sc_ref, *, layer, nt):
    rows, hist = PROMPT_ROWS, HIST_ROWS
    i = pl.program_id(0)
    steps_per_seq = nt // PROMPT_SLOTS
    tt = i % steps_per_seq
    lb = _lower_bound(lbp_ref[...], layer)
    one = lambda ref: [ref]
    prepare = functools.partial(
        _prompt_prepare, lb=lb, wpool_ref=wpool_ref, pscale_ref=pscale_ref, ext_ref=ext_ref,
        pool_ref=one(pool_ref), decay_ref=one(decay_ref), nat_ref=one(nat_ref), seg_ref=one(seg_ref),
        qk_ref=one(qk_ref), lev_ref=one(lev_ref))
    masks, eye = _level_masks(rows, PROMPT_LEVELS)
    heads = functools.partial(
        _prompt_heads, hgn_ref=hgn_ref, s_ref=s_ref, pool_ref=one(pool_ref), decay_ref=one(decay_ref),
        qk_ref=one(qk_ref), lev_ref=one(lev_ref), sc_ref=one(sc_ref), masks=masks, eye=eye)

    def in_proj(x_rows, z_ref):
        h = _rms(x_rows[...], g1_ref[...]).astype(BF16)
        z_ref[...] = jnp.dot(h, win_ref[...], preferred_element_type=F32)

    def mix_block(z_ref, t_blk, starts_seq, x_rows, out_rows):
        for batch in prepare(z_ref, t_blk, starts_seq, 0):
            for item in batch:
                item()
        for head in heads(0, z_ref, mix_ref):
            head()
        out_rows[...] = x_rows[...] + jnp.dot(mix_ref[...], wo_ref[...], preferred_element_type=F32)

    @pl.when(i == 0)
    def _():
        sc_ref[...] = jnp.zeros_like(sc_ref)
        ext_ref[...] = jnp.zeros_like(ext_ref)
        in_proj(xc_ref.at[0:rows], z_even)

    @pl.when(tt == 0)
    def _():
        s_ref[...] = jnp.zeros_like(s_ref)

    region = pl.when(i < pl.num_programs(0))
    region(functools.partial(in_proj, xc_ref.at[rows:2 * rows], z_odd))
    region(functools.partial(mix_block, z_even, PROMPT_SLOTS * tt, tt == 0, xc_ref.at[0:rows], x1_ref.at[0:rows]))
    region(functools.partial(in_proj, xn_ref, z_even))
    region(functools.partial(mix_block, z_odd, PROMPT_SLOTS * tt + 1, False, xc_ref.at[rows:2 * rows],
                             x1_ref.at[rows:2 * rows]))
    pooln_ref[0] = ext_ref[hist + rows - POOL_CACHE:hist + rows, :]

    @pl.when(tt == steps_per_seq - 1)
    def _():
        snew_ref[0] = s_ref[...]


def _prompt_layer(x, g1, w_in, w_o, lbp, wpool, pscale, hgn, batch, seq, layer):
    rows, slots = PROMPT_ROWS, PROMPT_SLOTS
    nt = seq // rows
    assert seq % (rows * slots) == 0
    nblk = batch * nt
    steps_per_seq = nt // slots
    const2 = lambda i: (0, 0)
    resident = lambda shape: pl.BlockSpec(shape, const2, pipeline_mode=pl.Buffered(1))
    return pl.pallas_call(
        functools.partial(_prompt_layer_body, layer=layer, nt=nt),
        grid=(nblk // slots,),
        in_specs=[
            pl.BlockSpec((slots * rows, D_MODEL), lambda i: (i, 0)),
            pl.BlockSpec((rows, D_MODEL), lambda i: (jnp.minimum(slots * i + slots, nblk - 1), 0)),
            pl.BlockSpec((1, D_MODEL), const2),
            resident(w_in.shape),
            resident(w_o.shape),
            pl.BlockSpec(lbp.shape, const2),
            pl.BlockSpec(wpool.shape, lambda i: (0, 0, 0)),
            pl.BlockSpec((1, POOL_WIDTH), const2),
            pl.BlockSpec((1, HG_WIDTH), const2),
        ],
        out_specs=[
            pl.BlockSpec((slots * rows, D_MODEL), lambda i: (i, 0)),
            pl.BlockSpec((1, POOL_CACHE, POOL_WIDTH), lambda i: (i // steps_per_seq, 0, 0)),
            pl.BlockSpec((1, HEADS, HEAD_DIM, HEAD_DIM), lambda i: (i // steps_per_seq, 0, 0, 0)),
        ],
        out_shape=[
            jax.ShapeDtypeStruct((batch * seq, D_MODEL), F32),
            jax.ShapeDtypeStruct((batch, POOL_CACHE, POOL_WIDTH), F32),
            jax.ShapeDtypeStruct((batch, HEADS, HEAD_DIM, HEAD_DIM), F32),
        ],
        scratch_shapes=[
            pltpu.VMEM((rows, IN_COLS), F32),
            pltpu.VMEM((rows, IN_COLS), F32),
            pltpu.VMEM((rows, D_MODEL), BF16),
            pltpu.VMEM((HIST_ROWS + rows, POOL_WIDTH), F32),
            pltpu.VMEM((HEADS, HEAD_DIM, HEAD_DIM), F32),
            pltpu.VMEM((rows, POOL_WIDTH), BF16),
            pltpu.VMEM((HEADS, HEAD_DIM), F32),
            *_level_scratch(rows, PROMPT_LEVELS),
        ],
        compiler_params=_cparams(("arbitrary",), 58),
        name="prompt_layer",
    )(x, x, g1, w_in, w_o, lbp, wpool, pscale, hgn)


def _mixer_sample_body(z_ref, cache_ref, sin_ref, lbp_ref, wpool_ref, pscale_ref, hgn_ref,
                       mix_ref, pooln_ref, snew_ref,
                       ext_ref, nat_ref, seg_ref, qk_ref, lev_ref, sc_ref, *, layer, nseq, steps, pos0):
    rows = nseq * steps
    levels = steps.bit_length() - 1
    hist = HIST_ROWS

    @pl.when(pl.program_id(0) == 0)
    def _():
        sc_ref[...] = jnp.zeros_like(sc_ref)

    ext_ref[:, 0:hist - POOL_CACHE, :] = jnp.zeros((nseq, hist - POOL_CACHE, POOL_WIDTH), F32)
    ext_ref[:, hist - POOL_CACHE:hist, :] = cache_ref[0]
    ext_ref[:, hist:hist + steps, :] = z_ref[:, 0:POOL_WIDTH].reshape(nseq, steps, POOL_WIDTH)
    pos = pos0 + lax.broadcasted_iota(jnp.int32, (1, steps, 1), 1)
    for gi, w in enumerate(POOL_WINDOWS):
        c0 = gi * POOL_GROUP_DIM
        e = ext_ref[:, :, c0:c0 + POOL_GROUP_DIM]
        cnt = jnp.minimum(pos + 1, w).astype(F32)
        mean = (_window_sums(e, w, 1, hist, steps) / cnt).reshape(rows, POOL_GROUP_DIM)
        u = e[:, hist:hist + steps].reshape(rows, POOL_GROUP_DIM)
        _pool_project(mean, 1.0, u, gi, wpool_ref, pscale_ref, mix_ref)
    pooln_ref[...] = ext_ref[:, hist + steps - POOL_CACHE:hist + steps, :]

    lb = _lower_bound(lbp_ref[...], layer)
    for c in range(HEADS):
        cols = slice(c * HEAD_DIM, (c + 1) * HEAD_DIM)
        _head_gates(z_ref[:, Z_Q + c * HEAD_DIM:Z_Q + (c + 1) * HEAD_DIM],
                    z_ref[:, Z_F + c * HEAD_DIM:Z_F + (c + 1) * HEAD_DIM], lb[:, cols], c, rows, nat_ref, qk_ref)
    _segment_products(rows, levels, nat_ref, seg_ref)
    prefix = _token_levels(range(rows), rows, levels, nat_ref, seg_ref, lev_ref)
    decay = jnp.stack(prefix[steps - 1::steps])
    decay_t = jnp.swapaxes(decay, 1, 2)
    masks, eye = _level_masks(rows, levels)
    for j in range(HEADS):
        c0 = j * HEAD_DIM
        scores = _intra_scores(j, rows, levels, masks, eye, qk_ref, lev_ref, sc_ref)
        v32 = z_ref[:, Z_I + c0:Z_I + c0 + HEAD_DIM]
        v = v32.astype(BF16)
        v3 = v32.reshape(nseq, steps, HEAD_DIM).astype(BF16)
        qt3 = _head_rows(lev_ref, levels, j, rows).reshape(nseq, steps, HEAD_DIM).astype(BF16)
        kt3 = _head_rows(lev_ref, levels + 1, j, rows).reshape(nseq, steps, HEAD_DIM).astype(BF16)
        s_in = sin_ref[:, j]
        o_state = lax.dot_general(qt3, s_in.astype(BF16), (((2,), (1,)), ((0,), (0,))),
                                  preferred_element_type=F32)
        o = jnp.dot(scores.astype(BF16), v, preferred_element_type=F32) + o_state.reshape(rows, HEAD_DIM)
        upd = lax.dot_general(kt3, v3, (((1,), (1,)), ((0,), (0,))), preferred_element_type=F32)
        snew_ref[:, j] = decay_t[:, :, j:j + 1] * s_in + upd
        _head_output(o, j, z_ref[:, Z_G + c0:Z_G + c0 + HEAD_DIM], hgn_ref, mix_ref)


def _mixer_sample(z, cache, state, lbp, wpool, pscale, hgn, nseq_total, steps, pos0, layer, nseq):
    rows = nseq * steps
    levels = steps.bit_length() - 1
    assert 1 << levels == steps and nseq_total % nseq == 0
    hist = HIST_ROWS
    const2 = lambda i: (0, 0)
    return pl.pallas_call(
        functools.partial(_mixer_sample_body, layer=layer, nseq=nseq, steps=steps, pos0=pos0),
        grid=(nseq_total // nseq,),
        in_specs=[
            pl.BlockSpec((rows, IN_COLS), lambda i: (i, 0)),
            pl.BlockSpec((1, nseq, POOL_CACHE, POOL_WIDTH), lambda i: (layer, i, 0, 0)),
            pl.BlockSpec((nseq, HEADS, HEAD_DIM, HEAD_DIM), lambda i: (i, 0, 0, 0)),
            pl.BlockSpec(lbp.shape, const2),
            pl.BlockSpec(wpool.shape, lambda i: (0, 0, 0)),
            pl.BlockSpec((1, POOL_WIDTH), const2),
            pl.BlockSpec((1, HG_WIDTH), const2),
        ],
        out_specs=[
            pl.BlockSpec((rows, D_MODEL), lambda i: (i, 0)),
            pl.BlockSpec((nseq, POOL_CACHE, POOL_WIDTH), lambda i: (i, 0, 0)),
            pl.BlockSpec((nseq, HEADS, HEAD_DIM, HEAD_DIM), lambda i: (i, 0, 0, 0)),
        ],
        out_shape=[
            jax.ShapeDtypeStruct((nseq_total * steps, D_MODEL), BF16),
            jax.ShapeDtypeStruct((nseq_total, POOL_CACHE, POOL_WIDTH), F32),
            jax.ShapeDtypeStruct((nseq_total, HEADS, HEAD_DIM, HEAD_DIM), F32),
        ],
        scratch_shapes=[
            pltpu.VMEM((nseq, hist + steps, POOL_WIDTH), F32),
            *_level_scratch(rows, levels),
        ],
        compiler_params=_cparams(("arbitrary",), 48),
        name="mixer_sample",
    )(z, cache, state, lbp, wpool, pscale, hgn)


def _out_proj_body(x_ref, mix_ref, w_ref, o_ref, wb_ref):
    w = w_ref[...].astype(BF16)
    wb_ref[...] = w
    o_ref[...] = x_ref[...] + jnp.dot(mix_ref[...], w, preferred_element_type=F32)


def _out_proj(x, mix, w, tm, tn):
    m = x.shape[0]
    return pl.pallas_call(
        _out_proj_body,
        grid=(m // tm, D_MODEL // tn),
        in_specs=[
            pl.BlockSpec((tm, tn), lambda i, j: (i, j)),
            pl.BlockSpec((tm, D_MODEL), lambda i, j: (i, 0)),
            pl.BlockSpec((D_MODEL, tn), lambda i, j: (0, j)),
        ],
        out_specs=[pl.BlockSpec((tm, tn), lambda i, j: (i, j)), pl.BlockSpec((D_MODEL, tn), lambda i, j: (0, j))],
        out_shape=[jax.ShapeDtypeStruct((m, D_MODEL), F32), jax.ShapeDtypeStruct(w.shape, BF16)],
        compiler_params=_cparams(("arbitrary", "arbitrary"), 48),
        name="out_proj",
    )(x, mix, w)


def _ffn_body(x_ref, g2_ref, wg_ref, wu_ref, wd_ref, gf_ref, y_ref, *rest):
    *wb_refs, h_ref = rest
    j = pl.program_id(1)
    last = pl.num_programs(1) - 1
    chunks = _row_chunks(x_ref.shape[0])

    def down(h, wg, wu, wd):
        gate = jnp.dot(h, wg, preferred_element_type=F32)
        up = jnp.dot(h, wu, preferred_element_type=F32)
        act = (gate * _sigmoid(gate) * up).astype(BF16)
        return jnp.dot(act, wd, preferred_element_type=F32)

    def weights():
        if not wb_refs:
            return wg_ref[...], wu_ref[...], wd_ref[...]
        w = tuple(ref[...].astype(BF16) for ref in (wg_ref, wu_ref, wd_ref))
        for wb_ref, wb in zip(wb_refs, w):
            wb_ref[...] = wb
        return w

    @pl.when(j == 0)
    def _():
        w = weights()
        for rs in chunks:
            h = _rms(x_ref[rs, :], g2_ref[...]).astype(BF16)
            h_ref[rs, :] = h
            y_ref[rs, :] = down(h, *w)

    @pl.when((j > 0) & (j < last))
    def _():
        y_ref[...] += down(h_ref[...], *weights())

    @pl.when(j == last)
    def _():
        w = weights()
        for rs in chunks:
            y_ref[rs, :] = _rms(x_ref[rs, :] + y_ref[rs, :] + down(h_ref[rs, :], *w), gf_ref[...])


def _ffn(x, g2, wg, wu, wd, gf, tm, tf):
    m = x.shape[0]
    d_ff = wg.shape[1]
    emit = wg.dtype != BF16
    w_specs = [
        pl.BlockSpec((D_MODEL, tf), lambda i, j: (0, j)),
        pl.BlockSpec((D_MODEL, tf), lambda i, j: (0, j)),
        pl.BlockSpec((tf, D_MODEL), lambda i, j: (j, 0)),
    ]
    x_mode = dict(pipeline_mode=pl.Buffered(1)) if m == tm else {}
    return pl.pallas_call(
        _ffn_body,
        grid=(m // tm, d_ff // tf),
        in_specs=[
            pl.BlockSpec((tm, D_MODEL), lambda i, j: (i, 0), **x_mode),
            pl.BlockSpec((1, D_MODEL), lambda i, j: (0, 0)),
            *w_specs,
            pl.BlockSpec((1, D_MODEL), lambda i, j: (0, 0)),
        ],
        out_specs=[pl.BlockSpec((tm, D_MODEL), lambda i, j: (i, 0))] + (w_specs if emit else []),
        out_shape=[jax.ShapeDtypeStruct((m, D_MODEL), F32)]
        + ([jax.ShapeDtypeStruct(a.shape, BF16) for a in (wg, wu, wd)] if emit else []),
        scratch_shapes=[pltpu.VMEM((tm, D_MODEL), BF16)],
        compiler_params=_cparams(("arbitrary", "arbitrary"), 58),
        name="ffn",
    )(x, g2, wg, wu, wd, gf)


def kernel(x_prompt, x_sample, cache_pool, state_hgrn, lb_param, norm1, w_in, w_pool, pool_scale,
           hg_norm, w_o, norm2, w_gate, w_up, w_down, norm_f):
    depth = w_in.shape[0]
    assert depth == 1, "single-layer trunk"
    layer = 0
    batch, seq, _ = x_prompt.shape
    dec_batch, dec_seq, _ = x_sample.shape

    row = lambda a: a.reshape(1, -1).astype(F32)
    g1, g2, gf = row(norm1[layer]), row(norm2[layer]), row(norm_f)
    lbp = lb_param.astype(F32)
    wpool = w_pool[layer].astype(BF16)
    pscale = row(pool_scale[layer])
    hgn = row(hg_norm[layer])

    m_s = dec_batch * dec_seq
    xs = x_sample.reshape(m_s, D_MODEL)
    z_s, w_in_b = _in_proj(xs, g1, w_in[layer], tm=m_s, tn=512)
    mix_s, pool_s, s_s = _mixer_sample(z_s, cache_pool, state_hgrn[layer], lbp, wpool, pscale, hgn,
                                       dec_batch, dec_seq, PAST_LEN, layer, nseq=8)
    x1_s, w_o_b = _out_proj(xs, mix_s, w_o[layer], tm=m_s, tn=512)
    y_s, w_gate_b, w_up_b, w_down_b = _ffn(x1_s, g2, w_gate[layer], w_up[layer], w_down[layer], gf, tm=m_s, tf=256)

    x1_p, pool_p, s_p = _prompt_layer(x_prompt.reshape(batch * seq, D_MODEL), g1, w_in_b, w_o_b,
                                      lbp, wpool, pscale, hgn, batch, seq, layer)
    (y_p,) = _ffn(x1_p, g2, w_gate_b, w_up_b, w_down_b, gf, tm=1024, tf=512)

    return (y_p.reshape(batch, seq, D_MODEL), y_s.reshape(dec_batch, dec_seq, D_MODEL),
            pool_p[None], s_p[None], pool_s[None], s_s[None])
```

```python
import functools

import jax
import jax.numpy as jnp
from jax import lax
from jax.experimental import pallas as pl
from jax.experimental.pallas import tpu as pltpu

D_MODEL = 2048
POOL_WIDTH = 1024
POOL_WINDOWS = (2, 4, 8, 16)
POOL_GROUP_DIM = POOL_WIDTH // len(POOL_WINDOWS)
POOL_CACHE = max(POOL_WINDOWS) - 1
HG_WIDTH = 1024
HEAD_DIM = 128
HEADS = HG_WIDTH // HEAD_DIM
IN_COLS = POOL_WIDTH + 4 * HG_WIDTH
Z_Q, Z_F, Z_I, Z_G = (POOL_WIDTH + n * HG_WIDTH for n in range(4))
EPS = 1e-6
SUBLANES = 8
assert HEADS == SUBLANES
HIST_ROWS = 2 * SUBLANES
PAST_LEN = 16384

F32 = jnp.float32
BF16 = jnp.bfloat16

MIB = 1024 * 1024
SAMPLE_PROJ_COLS = 512
SAMPLE_FFN_COLS = 256
SAMPLE_SEQS = 8
PROMPT_FFN_ROWS = 1024
PROMPT_FFN_COLS = 512
VMEM_MIB = dict(in_proj=48, out_proj=48, mixer_sample=48, ffn=58, prompt_layer=60)


def _cparams(name, sem):
    return pltpu.CompilerParams(dimension_semantics=sem, vmem_limit_bytes=VMEM_MIB[name] * MIB)


def _rms(x, g):
    return x * lax.rsqrt(jnp.mean(x * x, axis=-1, keepdims=True) + EPS) * g


def _sigmoid(x):
    return 1.0 / (1.0 + jnp.exp(-x))


ROW_CHUNK = 256


def _row_chunks(n):
    return [slice(r, r + ROW_CHUNK) for r in range(0, n, ROW_CHUNK)]


def _in_proj_body(x_ref, g_ref, w_ref, z_ref, wb_ref, h_ref):
    j = pl.program_id(1)
    w = w_ref[...].astype(BF16)
    wb_ref[...] = w

    @pl.when(j == 0)
    def _():
        for rs in _row_chunks(x_ref.shape[0]):
            h = _rms(x_ref[rs, :], g_ref[...]).astype(BF16)
            h_ref[rs, :] = h
            z_ref[rs, :] = jnp.dot(h, w, preferred_element_type=F32)

    @pl.when(j > 0)
    def _():
        z_ref[...] = jnp.dot(h_ref[...], w, preferred_element_type=F32)


def _in_proj(x, g, w, tm, tn):
    m = x.shape[0]
    return pl.pallas_call(
        _in_proj_body,
        grid=(m // tm, IN_COLS // tn),
        in_specs=[
            pl.BlockSpec((tm, D_MODEL), lambda i, j: (i, 0)),
            pl.BlockSpec((1, D_MODEL), lambda i, j: (0, 0)),
            pl.BlockSpec((D_MODEL, tn), lambda i, j: (0, j)),
        ],
        out_specs=[pl.BlockSpec((tm, tn), lambda i, j: (i, j)), pl.BlockSpec((D_MODEL, tn), lambda i, j: (0, j))],
        out_shape=[jax.ShapeDtypeStruct((m, IN_COLS), F32), jax.ShapeDtypeStruct(w.shape, BF16)],
        scratch_shapes=[pltpu.VMEM((tm, D_MODEL), BF16)],
        compiler_params=_cparams("in_proj", ("arbitrary", "arbitrary")),
        name="in_proj",
    )(x, g, w)


def _lower_bound(lbp, layer):
    e = jnp.exp(lbp - jnp.max(lbp, axis=0, keepdims=True))
    return jnp.sum(e[: layer + 1], axis=0, keepdims=True) / jnp.sum(e, axis=0, keepdims=True)


def _pool_project(acc, cnt, u, gi, wpool_ref, pscale_ref, mix_ref):
    c0 = gi * POOL_GROUP_DIM
    pooled = acc / cnt - u
    out = jnp.dot(pooled.astype(BF16), wpool_ref[gi], preferred_element_type=F32)
    mix_ref[:, c0:c0 + POOL_GROUP_DIM] = (out * pscale_ref[:, c0:c0 + POOL_GROUP_DIM]).astype(BF16)


def _token_tile(nat_ref, a, t):
    return nat_ref[a, t // SUBLANES, pl.ds(t % SUBLANES, HEADS, stride=SUBLANES), :]


def _head_rows(lev_ref, lvl, j, rows):
    return lev_ref[lvl, pl.ds(j, rows, stride=HEADS), :]


def _head_gates(zq, zf, lb, c, rows, nat_ref, qk_ref):
    fg = lb + (1.0 - lb) * _sigmoid(zf)
    qs = zq * _sigmoid(zq)
    kk = 1.0 - fg
    qk_ref[0, :, c * HEAD_DIM:(c + 1) * HEAD_DIM] = qs.astype(BF16)
    qk_ref[1, :, c * HEAD_DIM:(c + 1) * HEAD_DIM] = kk.astype(BF16)
    for a, x in enumerate((qs, kk, fg)):
        nat_ref[a, :, c * SUBLANES:(c + 1) * SUBLANES, :] = x.reshape(rows // SUBLANES, SUBLANES, HEAD_DIM)


def _segment_offsets(rows, levels):
    off = [0]
    for lvl in range(levels):
        off.append(off[-1] + (rows >> lvl))
    return off


def _segment_products(rows, levels, nat_ref, seg_ref):
    off = _segment_offsets(rows, levels)
    for t in range(rows):
        seg_ref[t] = _token_tile(nat_ref, 2, t)
    for lvl in range(1, levels):
        for m in range(rows >> lvl):
            seg_ref[off[lvl] + m] = seg_ref[off[lvl - 1] + 2 * m] * seg_ref[off[lvl - 1] + 2 * m + 1]


def _token_levels(tokens, rows, levels, nat_ref, seg_ref, lev_ref):
    off = _segment_offsets(rows, levels)
    prefix = []
    for t in tokens:
        tok = slice(t * HEADS, (t + 1) * HEADS)
        q_t, k_t = _token_tile(nat_ref, 0, t), _token_tile(nat_ref, 1, t)
        p_t, r_t = seg_ref[t], None
        for lvl in range(levels):
            m = t >> lvl
            sibling = seg_ref[off[lvl] + (m ^ 1)]
            if m & 1:
                lev_ref[lvl, tok, :] = q_t * p_t
                p_t = p_t * sibling
            else:
                lev_ref[lvl, tok, :] = k_t if r_t is None else k_t * r_t
                r_t = sibling if r_t is None else r_t * sibling
        lev_ref[levels, tok, :] = q_t * p_t
        lev_ref[levels + 1, tok, :] = k_t if r_t is None else k_t * r_t
        prefix.append(p_t)
    return prefix


def _level_scratch(rows, levels):
    return [
        pltpu.VMEM((3, rows // SUBLANES, HEADS * SUBLANES, HEAD_DIM), F32),
        pltpu.VMEM((2 * rows, HEADS, HEAD_DIM), F32),
        pltpu.VMEM((2, rows, HG_WIDTH), BF16),
        pltpu.VMEM((levels + 2, rows * HEADS, HEAD_DIM), F32),
        pltpu.VMEM((HEADS, rows, rows), F32),
    ]


def _level_masks(rows, levels):
    ti = lax.broadcasted_iota(jnp.int32, (rows, rows), 0)
    si = lax.broadcasted_iota(jnp.int32, (rows, rows), 1)
    x = ti ^ si
    masks = []
    for lvl in range(levels):
        h = 1 << lvl
        if h < SUBLANES:
            masks.append(((x >> lvl) == 1) & (((ti >> lvl) & 1) == 1))
        else:
            half = lax.broadcasted_iota(jnp.int32, (h, rows), 1) >> lvl
            masks.append([half == 2 * b for b in range(rows // (2 * h))])
    return masks, ti == si


def _nt(a, b):
    return lax.dot_general(a, b, (((1,), (1,)), ((), ())), preferred_element_type=F32)


def _score_level(j, lvl, rows, masks, eye, qk_ref, lev_ref, sc_ref):
    c0 = j * HEAD_DIM
    sc_ref = sc_ref.at[j]
    if lvl < 0:
        pltpu.store(sc_ref, _nt(qk_ref[0, :, c0:c0 + HEAD_DIM], qk_ref[1, :, c0:c0 + HEAD_DIM]), mask=eye)
        return
    h = 1 << lvl
    x32 = _head_rows(lev_ref, lvl, j, rows)
    x = x32.astype(BF16)
    if h < SUBLANES:
        pltpu.store(sc_ref, _nt(x, x), mask=masks[lvl])
        return
    blocks = rows // (2 * h)
    upper = x32.reshape(blocks, 2 * h, HEAD_DIM)[:, h:].reshape(rows // 2, HEAD_DIM).astype(BF16)
    s = _nt(upper, x)
    for b in range(blocks):
        pltpu.store(sc_ref.at[b * 2 * h + h:(b + 1) * 2 * h, :], s[b * h:(b + 1) * h], mask=masks[lvl][b])


def _intra_scores(j, rows, levels, masks, eye, qk_ref, lev_ref, sc_ref):
    for lvl in range(-1, levels):
        _score_level(j, lvl, rows, masks, eye, qk_ref, lev_ref, sc_ref)
    return sc_ref[j]


def _window_sums(e, w, axis, hist, rows):
    s, d = e, 1
    while d < min(w, SUBLANES):
        s = s + pltpu.roll(s, d, axis)
        d *= 2
    take = lambda a, lo: lax.slice_in_dim(a, lo, lo + rows, axis=axis)
    out = take(s, hist)
    if w > SUBLANES:
        assert w == 2 * SUBLANES
        out = out + take(s, hist - SUBLANES)
    return out


def _head_output(o, j, zg, hgn_ref, mix_ref):
    c0 = j * HEAD_DIM
    o = _rms(o, hgn_ref[:, c0:c0 + HEAD_DIM]) * (zg * _sigmoid(zg))
    mix_ref[:, POOL_WIDTH + c0:POOL_WIDTH + c0 + HEAD_DIM] = o.astype(BF16)


PROMPT_ROWS = 128
PROMPT_LEVELS = 7


PROMPT_SLOTS = 2


def _prompt_prepare(z_blk, t_blk, starts_seq, slot, lb, wpool_ref, pscale_ref,
                    ext_ref, pool_ref, decay_ref, nat_ref, seg_ref, qk_ref, lev_ref):
    rows, levels, hist = PROMPT_ROWS, PROMPT_LEVELS, HIST_ROWS
    nat, seg, qk, lev = nat_ref[slot], seg_ref[slot], qk_ref[slot], lev_ref[slot]

    def pool_group(gi):
        def run():
            if gi == 0:
                ext_ref[0:hist, :] = jnp.where(starts_seq, 0.0, ext_ref[rows:rows + hist, :])
                ext_ref[hist:hist + rows, :] = z_blk[:, 0:POOL_WIDTH]
            w, c0 = POOL_WINDOWS[gi], gi * POOL_GROUP_DIM
            e = ext_ref[:, c0:c0 + POOL_GROUP_DIM]
            pos = t_blk * rows + lax.broadcasted_iota(jnp.int32, (rows, 1), 0)
            cnt = jnp.minimum(pos + 1, w).astype(F32)
            _pool_project(_window_sums(e, w, 0, hist, rows), cnt, e[hist:hist + rows], gi,
                          wpool_ref, pscale_ref, pool_ref[slot])
        return run

    def gates(c):
        cols = slice(c * HEAD_DIM, (c + 1) * HEAD_DIM)
        return lambda: _head_gates(z_blk[:, Z_Q + c * HEAD_DIM:Z_Q + (c + 1) * HEAD_DIM],
                                   z_blk[:, Z_F + c * HEAD_DIM:Z_F + (c + 1) * HEAD_DIM], lb[:, cols], c, rows, nat, qk)

    def tokens(k, n):
        def run():
            prefix = _token_levels(range(k * n, (k + 1) * n), rows, levels, nat, seg, lev)
            if (k + 1) * n == rows:
                decay_ref[slot][...] = prefix[-1]
        return run

    a, b = [pool_group(g) for g in range(len(POOL_WINDOWS))], [gates(c) for c in range(HEADS)]
    t = [tokens(k, rows // HEADS) for k in range(HEADS)]
    segments = lambda: _segment_products(rows, levels, nat, seg)
    return [[a[0], a[1], b[0]], [a[2], a[3], b[1]], b[2:5], b[5:8], [segments, t[0]], t[1:3], t[3:6], t[6:8]]


def _prompt_heads(slot, z_blk, mix_blk, hgn_ref, s_ref, pool_ref, decay_ref, qk_ref, lev_ref, sc_ref,
                  masks, eye):
    rows, levels = PROMPT_ROWS, PROMPT_LEVELS
    qk, lev, sc = qk_ref[slot], lev_ref[slot], sc_ref[slot]
    value = lambda j: z_blk[:, Z_I + j * HEAD_DIM:Z_I + (j + 1) * HEAD_DIM].astype(BF16)

    def pool_part():
        mix_blk[:, 0:POOL_WIDTH] = pool_ref[slot][...]

    def output(j):
        c0 = j * HEAD_DIM
        qt = _head_rows(lev, levels, j, rows).astype(BF16)
        o = (jnp.dot(sc[j].astype(BF16), value(j), preferred_element_type=F32)
             + jnp.dot(qt, s_ref[j].astype(BF16), preferred_element_type=F32))
        _head_output(o, j, z_blk[:, Z_G + c0:Z_G + c0 + HEAD_DIM], hgn_ref, mix_blk)

    def state(j):
        decay_t = jnp.transpose(decay_ref[slot][...])
        kt = _head_rows(lev, levels + 1, j, rows).astype(BF16)
        s_ref[j] = decay_t[:, j:j + 1] * s_ref[j] + lax.dot_general(
            kt, value(j), (((0,), (0,)), ((), ())), preferred_element_type=F32)

    heads = range(HEADS)
    scores = [functools.partial(_score_level, j, lvl, rows, masks, eye, qk, lev, sc)
              for lvl in range(-1, levels) for j in heads]
    return ([pool_part] + scores + [functools.partial(output, j) for j in heads]
            + [functools.partial(state, j) for j in heads])


def _prompt_layer_body(xc_ref, xn_ref, g1_ref, win_ref, wo_ref, lbp_ref, wpool_ref, pscale_ref, hgn_ref,
                       x1_ref, pooln_ref, snew_ref,
                       z_ref, mix_ref, ext_ref, s_ref, pool_ref, decay_ref, nat_ref, seg_ref, qk_ref,
                       lev_ref, sc_ref, *, layer, nt):
    rows, hist = PROMPT_ROWS, HIST_ROWS
    i = pl.program_id(0)
    steps_per_seq = nt // PROMPT_SLOTS
    tt = i % steps_per_seq
    lb = _lower_bound(lbp_ref[...], layer)
    one = lambda ref: [ref]
    prepare = functools.partial(
        _prompt_prepare, lb=lb, wpool_ref=wpool_ref, pscale_ref=pscale_ref, ext_ref=ext_ref,
        pool_ref=one(pool_ref), decay_ref=one(decay_ref), nat_ref=one(nat_ref), seg_ref=one(seg_ref),
        qk_ref=one(qk_ref), lev_ref=one(lev_ref))
    masks, eye = _level_masks(rows, PROMPT_LEVELS)
    heads = functools.partial(
        _prompt_heads, hgn_ref=hgn_ref, s_ref=s_ref, pool_ref=one(pool_ref), decay_ref=one(decay_ref),
        qk_ref=one(qk_ref), lev_ref=one(lev_ref), sc_ref=one(sc_ref), masks=masks, eye=eye)

    def in_proj(x_ref):
        h = _rms(x_ref[...], g1_ref[...]).astype(BF16)
        z_ref[...] = jnp.dot(h, win_ref[...], preferred_element_type=F32)

    def mix_block(blk, t_blk, starts_seq):
        z_blk, mix_blk = z_ref.at[blk * rows:(blk + 1) * rows], mix_ref.at[blk * rows:(blk + 1) * rows]
        for batch in prepare(z_blk, t_blk, starts_seq, 0):
            for item in batch:
                item()
        for head in heads(0, z_blk, mix_blk):
            head()

    def out_proj():
        x1_ref[...] = xc_ref[...] + jnp.dot(mix_ref[...], wo_ref[...], preferred_element_type=F32)

    @pl.when(i == 0)
    def _():
        sc_ref[...] = jnp.zeros_like(sc_ref)
        ext_ref[...] = jnp.zeros_like(ext_ref)
        in_proj(xc_ref)

    @pl.when(tt == 0)
    def _():
        s_ref[...] = jnp.zeros_like(s_ref)

    region = pl.when(i < pl.num_programs(0))
    region(functools.partial(mix_block, 0, PROMPT_SLOTS * tt, tt == 0))
    region(functools.partial(mix_block, 1, PROMPT_SLOTS * tt + 1, False))
    region(out_proj)
    region(functools.partial(in_proj, xn_ref))
    pooln_ref[0] = ext_ref[hist + rows - POOL_CACHE:hist + rows, :]

    @pl.when(tt == steps_per_seq - 1)
    def _():
        snew_ref[0] = s_ref[...]


def _prompt_layer(x, g1, w_in, w_o, lbp, wpool, pscale, hgn, batch, seq, layer):
    rows, slots = PROMPT_ROWS, PROMPT_SLOTS
    nt = seq // rows
    assert seq % (rows * slots) == 0
    nblk = batch * nt
    steps_per_seq = nt // slots
    const2 = lambda i: (0, 0)
    resident = lambda shape: pl.BlockSpec(shape, const2, pipeline_mode=pl.Buffered(1))
    return pl.pallas_call(
        functools.partial(_prompt_layer_body, layer=layer, nt=nt),
        grid=(nblk // slots,),
        in_specs=[
            pl.BlockSpec((slots * rows, D_MODEL), lambda i: (i, 0)),
            pl.BlockSpec((slots * rows, D_MODEL), lambda i: (jnp.minimum(i + 1, nblk // slots - 1), 0)),
            pl.BlockSpec((1, D_MODEL), const2),
            resident(w_in.shape),
            resident(w_o.shape),
            pl.BlockSpec(lbp.shape, const2),
            pl.BlockSpec(wpool.shape, lambda i: (0, 0, 0)),
            pl.BlockSpec((1, POOL_WIDTH), const2),
            pl.BlockSpec((1, HG_WIDTH), const2),
        ],
        out_specs=[
            pl.BlockSpec((slots * rows, D_MODEL), lambda i: (i, 0)),
            pl.BlockSpec((1, POOL_CACHE, POOL_WIDTH), lambda i: (i // steps_per_seq, 0, 0)),
            pl.BlockSpec((1, HEADS, HEAD_DIM, HEAD_DIM), lambda i: (i // steps_per_seq, 0, 0, 0)),
        ],
        out_shape=[
            jax.ShapeDtypeStruct((batch * seq, D_MODEL), F32),
            jax.ShapeDtypeStruct((batch, POOL_CACHE, POOL_WIDTH), F32),
            jax.ShapeDtypeStruct((batch, HEADS, HEAD_DIM, HEAD_DIM), F32),
        ],
        scratch_shapes=[
            pltpu.VMEM((slots * rows, IN_COLS), F32),
            pltpu.VMEM((slots * rows, D_MODEL), BF16),
            pltpu.VMEM((HIST_ROWS + rows, POOL_WIDTH), F32),
            pltpu.VMEM((HEADS, HEAD_DIM, HEAD_DIM), F32),
            pltpu.VMEM((rows, POOL_WIDTH), BF16),
            pltpu.VMEM((HEADS, HEAD_DIM), F32),
            *_level_scratch(rows, PROMPT_LEVELS),
        ],
        compiler_params=_cparams("prompt_layer", ("arbitrary",)),
        name="prompt_layer",
    )(x, x, g1, w_in, w_o, lbp, wpool, pscale, hgn)


def _mixer_sample_body(z_ref, cache_ref, sin_ref, lbp_ref, wpool_ref, pscale_ref, hgn_ref,
                       mix_ref, pooln_ref, snew_ref,
                       ext_ref, nat_ref, seg_ref, qk_ref, lev_ref, sc_ref, *, layer, nseq, steps, pos0):
    rows = nseq * steps
    levels = steps.bit_length() - 1
    hist = HIST_ROWS

    @pl.when(pl.program_id(0) == 0)
    def _():
        sc_ref[...] = jnp.zeros_like(sc_ref)

    ext_ref[:, 0:hist - POOL_CACHE, :] = jnp.zeros((nseq, hist - POOL_CACHE, POOL_WIDTH), F32)
    ext_ref[:, hist - POOL_CACHE:hist, :] = cache_ref[0]
    ext_ref[:, hist:hist + steps, :] = z_ref[:, 0:POOL_WIDTH].reshape(nseq, steps, POOL_WIDTH)
    pos = pos0 + lax.broadcasted_iota(jnp.int32, (1, steps, 1), 1)
    for gi, w in enumerate(POOL_WINDOWS):
        c0 = gi * POOL_GROUP_DIM
        e = ext_ref[:, :, c0:c0 + POOL_GROUP_DIM]
        cnt = jnp.minimum(pos + 1, w).astype(F32)
        mean = (_window_sums(e, w, 1, hist, steps) / cnt).reshape(rows, POOL_GROUP_DIM)
        u = e[:, hist:hist + steps].reshape(rows, POOL_GROUP_DIM)
        _pool_project(mean, 1.0, u, gi, wpool_ref, pscale_ref, mix_ref)
    pooln_ref[...] = ext_ref[:, hist + steps - POOL_CACHE:hist + steps, :]

    lb = _lower_bound(lbp_ref[...], layer)
    for c in range(HEADS):
        cols = slice(c * HEAD_DIM, (c + 1) * HEAD_DIM)
        _head_gates(z_ref[:, Z_Q + c * HEAD_DIM:Z_Q + (c + 1) * HEAD_DIM],
                    z_ref[:, Z_F + c * HEAD_DIM:Z_F + (c + 1) * HEAD_DIM], lb[:, cols], c, rows, nat_ref, qk_ref)
    _segment_products(rows, levels, nat_ref, seg_ref)
    prefix = _token_levels(range(rows), rows, levels, nat_ref, seg_ref, lev_ref)
    decay = jnp.stack(prefix[steps - 1::steps])
    decay_t = jnp.swapaxes(decay, 1, 2)
    masks, eye = _level_masks(rows, levels)
    for j in range(HEADS):
        c0 = j * HEAD_DIM
        scores = _intra_scores(j, rows, levels, masks, eye, qk_ref, lev_ref, sc_ref)
        v32 = z_ref[:, Z_I + c0:Z_I + c0 + HEAD_DIM]
        v = v32.astype(BF16)
        v3 = v32.reshape(nseq, steps, HEAD_DIM).astype(BF16)
        qt3 = _head_rows(lev_ref, levels, j, rows).reshape(nseq, steps, HEAD_DIM).astype(BF16)
        kt3 = _head_rows(lev_ref, levels + 1, j, rows).reshape(nseq, steps, HEAD_DIM).astype(BF16)
        s_in = sin_ref[:, j]
        o_state = lax.dot_general(qt3, s_in.astype(BF16), (((2,), (1,)), ((0,), (0,))),
                                  preferred_element_type=F32)
        o = jnp.dot(scores.astype(BF16), v, preferred_element_type=F32) + o_state.reshape(rows, HEAD_DIM)
        upd = lax.dot_general(kt3, v3, (((1,), (1,)), ((0,), (0,))), preferred_element_type=F32)
        snew_ref[:, j] = decay_t[:, :, j:j + 1] * s_in + upd
        _head_output(o, j, z_ref[:, Z_G + c0:Z_G + c0 + HEAD_DIM], hgn_ref, mix_ref)


def _mixer_sample(z, cache, state, lbp, wpool, pscale, hgn, nseq_total, steps, pos0, layer, nseq):
    rows = nseq * steps
    levels = steps.bit_length() - 1
    assert 1 << levels == steps and nseq_total % nseq == 0
    hist = HIST_ROWS
    const2 = lambda i: (0, 0)
    return pl.pallas_call(
        functools.partial(_mixer_sample_body, layer=layer, nseq=nseq, steps=steps, pos0=pos0),
        grid=(nseq_total // nseq,),
        in_specs=[
            pl.BlockSpec((rows, IN_COLS), lambda i: (i, 0)),
            pl.BlockSpec((1, nseq, POOL_CACHE, POOL_WIDTH), lambda i: (layer, i, 0, 0)),
            pl.BlockSpec((nseq, HEADS, HEAD_DIM, HEAD_DIM), lambda i: (i, 0, 0, 0)),
            pl.BlockSpec(lbp.shape, const2),
            pl.BlockSpec(wpool.shape, lambda i: (0, 0, 0)),
            pl.BlockSpec((1, POOL_WIDTH), const2),
            pl.BlockSpec((1, HG_WIDTH), const2),
        ],
        out_specs=[
            pl.BlockSpec((rows, D_MODEL), lambda i: (i, 0)),
            pl.BlockSpec((nseq, POOL_CACHE, POOL_WIDTH), lambda i: (i, 0, 0)),
            pl.BlockSpec((nseq, HEADS, HEAD_DIM, HEAD_DIM), lambda i: (i, 0, 0, 0)),
        ],
        out_shape=[
            jax.ShapeDtypeStruct((nseq_total * steps, D_MODEL), BF16),
            jax.ShapeDtypeStruct((nseq_total, POOL_CACHE, POOL_WIDTH), F32),
            jax.ShapeDtypeStruct((nseq_total, HEADS, HEAD_DIM, HEAD_DIM), F32),
        ],
        scratch_shapes=[
            pltpu.VMEM((nseq, hist + steps, POOL_WIDTH), F32),
            *_level_scratch(rows, levels),
        ],
        compiler_params=_cparams("mixer_sample", ("arbitrary",)),
        name="mixer_sample",
    )(z, cache, state, lbp, wpool, pscale, hgn)


def _out_proj_body(x_ref, mix_ref, w_ref, o_ref, wb_ref):
    w = w_ref[...].astype(BF16)
    wb_ref[...] = w
    o_ref[...] = x_ref[...] + jnp.dot(mix_ref[...], w, preferred_element_type=F32)


def _out_proj(x, mix, w, tm, tn):
    m = x.shape[0]
    return pl.pallas_call(
        _out_proj_body,
        grid=(m // tm, D_MODEL // tn),
        in_specs=[
            pl.BlockSpec((tm, tn), lambda i, j: (i, j)),
            pl.BlockSpec((tm, D_MODEL), lambda i, j: (i, 0)),
            pl.BlockSpec((D_MODEL, tn), lambda i, j: (0, j)),
        ],
        out_specs=[pl.BlockSpec((tm, tn), lambda i, j: (i, j)), pl.BlockSpec((D_MODEL, tn), lambda i, j: (0, j))],
        out_shape=[jax.ShapeDtypeStruct((m, D_MODEL), F32), jax.ShapeDtypeStruct(w.shape, BF16)],
        compiler_params=_cparams("out_proj", ("arbitrary", "arbitrary")),
        name="out_proj",
    )(x, mix, w)


def _ffn_body(x_ref, g2_ref, wg_ref, wu_ref, wd_ref, gf_ref, y_ref, *rest):
    *wb_refs, h_ref = rest
    j = pl.program_id(1)
    last = pl.num_programs(1) - 1
    chunks = _row_chunks(x_ref.shape[0])

    def down(h, wg, wu, wd):
        gate = jnp.dot(h, wg, preferred_element_type=F32)
        up = jnp.dot(h, wu, preferred_element_type=F32)
        act = (gate * _sigmoid(gate) * up).astype(BF16)
        return jnp.dot(act, wd, preferred_element_type=F32)

    def weights():
        if not wb_refs:
            return wg_ref[...], wu_ref[...], wd_ref[...]
        w = tuple(ref[...].astype(BF16) for ref in (wg_ref, wu_ref, wd_ref))
        for wb_ref, wb in zip(wb_refs, w):
            wb_ref[...] = wb
        return w

    @pl.when(j == 0)
    def _():
        w = weights()
        for rs in chunks:
            h = _rms(x_ref[rs, :], g2_ref[...]).astype(BF16)
            h_ref[rs, :] = h
            y_ref[rs, :] = down(h, *w)

    @pl.when((j > 0) & (j < last))
    def _():
        y_ref[...] += down(h_ref[...], *weights())

    @pl.when(j == last)
    def _():
        w = weights()
        for rs in chunks:
            y_ref[rs, :] = _rms(x_ref[rs, :] + y_ref[rs, :] + down(h_ref[rs, :], *w), gf_ref[...])


def _ffn(x, g2, wg, wu, wd, gf, tm, tf):
    m = x.shape[0]
    d_ff = wg.shape[1]
    emit = wg.dtype != BF16
    w_specs = [
        pl.BlockSpec((D_MODEL, tf), lambda i, j: (0, j)),
        pl.BlockSpec((D_MODEL, tf), lambda i, j: (0, j)),
        pl.BlockSpec((tf, D_MODEL), lambda i, j: (j, 0)),
    ]
    x_mode = dict(pipeline_mode=pl.Buffered(1)) if m == tm else {}
    return pl.pallas_call(
        _ffn_body,
        grid=(m // tm, d_ff // tf),
        in_specs=[
            pl.BlockSpec((tm, D_MODEL), lambda i, j: (i, 0), **x_mode),
            pl.BlockSpec((1, D_MODEL), lambda i, j: (0, 0)),
            *w_specs,
            pl.BlockSpec((1, D_MODEL), lambda i, j: (0, 0)),
        ],
        out_specs=[pl.BlockSpec((tm, D_MODEL), lambda i, j: (i, 0))] + (w_specs if emit else []),
        out_shape=[jax.ShapeDtypeStruct((m, D_MODEL), F32)]
        + ([jax.ShapeDtypeStruct(a.shape, BF16) for a in (wg, wu, wd)] if emit else []),
        scratch_shapes=[pltpu.VMEM((tm, D_MODEL), BF16)],
        compiler_params=_cparams("ffn", ("arbitrary", "arbitrary")),
        name="ffn",
    )(x, g2, wg, wu, wd, gf)


def kernel(x_prompt, x_sample, cache_pool, state_hgrn, lb_param, norm1, w_in, w_pool, pool_scale,
           hg_norm, w_o, norm2, w_gate, w_up, w_down, norm_f):
    depth = w_in.shape[0]
    assert depth == 1, "single-layer trunk"
    layer = 0
    batch, seq, _ = x_prompt.shape
    dec_batch, dec_seq, _ = x_sample.shape

    row = lambda a: a.reshape(1, -1).astype(F32)
    g1, g2, gf = row(norm1[layer]), row(norm2[layer]), row(norm_f)
    lbp = lb_param.astype(F32)
    wpool = w_pool[layer].astype(BF16)
    pscale = row(pool_scale[layer])
    hgn = row(hg_norm[layer])

    m_s = dec_batch * dec_seq
    xs = x_sample.reshape(m_s, D_MODEL)
    z_s, w_in_b = _in_proj(xs, g1, w_in[layer], tm=m_s, tn=SAMPLE_PROJ_COLS)
    mix_s, pool_s, s_s = _mixer_sample(z_s, cache_pool, state_hgrn[layer], lbp, wpool, pscale, hgn,
                                       dec_batch, dec_seq, PAST_LEN, layer, nseq=SAMPLE_SEQS)
    x1_s, w_o_b = _out_proj(xs, mix_s, w_o[layer], tm=m_s, tn=SAMPLE_PROJ_COLS)
    y_s, w_gate_b, w_up_b, w_down_b = _ffn(x1_s, g2, w_gate[layer], w_up[layer], w_down[layer], gf,
                                           tm=m_s, tf=SAMPLE_FFN_COLS)

    x1_p, pool_p, s_p = _prompt_layer(x_prompt.reshape(batch * seq, D_MODEL), g1, w_in_b, w_o_b,
                                      lbp, wpool, pscale, hgn, batch, seq, layer)
    (y_p,) = _ffn(x1_p, g2, w_gate_b, w_up_b, w_down_b, gf, tm=PROMPT_FFN_ROWS, tf=PROMPT_FFN_COLS)

    return (y_p.reshape(batch, seq, D_MODEL), y_s.reshape(dec_batch, dec_seq, D_MODEL),
            pool_p[None], s_p[None], pool_s[None], s_s[None])
```

```python
import functools

import jax
import jax.numpy as jnp
from jax import lax
from jax.experimental import pallas as pl
from jax.experimental.pallas import tpu as pltpu

D_MODEL = 2048
POOL_WIDTH = 1024
POOL_WINDOWS = (2, 4, 8, 16)
POOL_GROUP_DIM = POOL_WIDTH // len(POOL_WINDOWS)
POOL_CACHE = max(POOL_WINDOWS) - 1
HG_WIDTH = 1024
HEAD_DIM = 128
HEADS = HG_WIDTH // HEAD_DIM
IN_COLS = POOL_WIDTH + 4 * HG_WIDTH
Z_Q, Z_F, Z_I, Z_G = (POOL_WIDTH + n * HG_WIDTH for n in range(4))
EPS = 1e-6
SUBLANES = 8
assert HEADS == SUBLANES
HIST_ROWS = 2 * SUBLANES
PAST_LEN = 16384

F32 = jnp.float32
BF16 = jnp.bfloat16

MIB = 1024 * 1024
SAMPLE_PROJ_COLS = 512
SAMPLE_FFN_COLS = 256
SAMPLE_SEQS = 8
PROMPT_FFN_ROWS = 1024
PROMPT_FFN_COLS = 512
VMEM_MIB = dict(in_proj=48, out_proj=48, mixer_sample=48, ffn=58, prompt_layer=60)


def _cparams(name, sem):
    return pltpu.CompilerParams(dimension_semantics=sem, vmem_limit_bytes=VMEM_MIB[name] * MIB)


def _rms(x, g):
    return x * lax.rsqrt(jnp.mean(x * x, axis=-1, keepdims=True) + EPS) * g


def _sigmoid(x):
    return 1.0 / (1.0 + jnp.exp(-x))


ROW_CHUNK = 256


def _row_chunks(n):
    return [slice(r, r + ROW_CHUNK) for r in range(0, n, ROW_CHUNK)]


def _in_proj_body(x_ref, g_ref, w_ref, z_ref, wb_ref, h_ref):
    j = pl.program_id(1)
    w = w_ref[...].astype(BF16)
    wb_ref[...] = w

    @pl.when(j == 0)
    def _():
        for rs in _row_chunks(x_ref.shape[0]):
            h = _rms(x_ref[rs, :], g_ref[...]).astype(BF16)
            h_ref[rs, :] = h
            z_ref[rs, :] = jnp.dot(h, w, preferred_element_type=F32)

    @pl.when(j > 0)
    def _():
        z_ref[...] = jnp.dot(h_ref[...], w, preferred_element_type=F32)


def _in_proj(x, g, w, tm, tn):
    m = x.shape[0]
    return pl.pallas_call(
        _in_proj_body,
        grid=(m // tm, IN_COLS // tn),
        in_specs=[
            pl.BlockSpec((tm, D_MODEL), lambda i, j: (i, 0)),
            pl.BlockSpec((1, D_MODEL), lambda i, j: (0, 0)),
            pl.BlockSpec((D_MODEL, tn), lambda i, j: (0, j)),
        ],
        out_specs=[pl.BlockSpec((tm, tn), lambda i, j: (i, j)), pl.BlockSpec((D_MODEL, tn), lambda i, j: (0, j))],
        out_shape=[jax.ShapeDtypeStruct((m, IN_COLS), F32), jax.ShapeDtypeStruct(w.shape, BF16)],
        scratch_shapes=[pltpu.VMEM((tm, D_MODEL), BF16)],
        compiler_params=_cparams("in_proj", ("arbitrary", "arbitrary")),
        name="in_proj",
    )(x, g, w)


def _lower_bound(lbp, layer):
    e = jnp.exp(lbp - jnp.max(lbp, axis=0, keepdims=True))
    return jnp.sum(e[: layer + 1], axis=0, keepdims=True) / jnp.sum(e, axis=0, keepdims=True)


def _pool_project(acc, cnt, u, gi, wpool_ref, pscale_ref, mix_ref):
    c0 = gi * POOL_GROUP_DIM
    pooled = acc / cnt - u
    out = jnp.dot(pooled.astype(BF16), wpool_ref[gi], preferred_element_type=F32)
    mix_ref[:, c0:c0 + POOL_GROUP_DIM] = (out * pscale_ref[:, c0:c0 + POOL_GROUP_DIM]).astype(BF16)


def _token_tile(nat_ref, a, t):
    return nat_ref[a, t // SUBLANES, pl.ds(t % SUBLANES, HEADS, stride=SUBLANES), :]


def _head_rows(lev_ref, lvl, j, rows):
    return lev_ref[lvl, pl.ds(j, rows, stride=HEADS), :]


def _head_gates(zq, zf, lb, c, rows, nat_ref, qk_ref):
    fg = lb + (1.0 - lb) * _sigmoid(zf)
    qs = zq * _sigmoid(zq)
    kk = 1.0 - fg
    qk_ref[0, :, c * HEAD_DIM:(c + 1) * HEAD_DIM] = qs.astype(BF16)
    qk_ref[1, :, c * HEAD_DIM:(c + 1) * HEAD_DIM] = kk.astype(BF16)
    for a, x in enumerate((qs, kk, fg)):
        nat_ref[a, :, c * SUBLANES:(c + 1) * SUBLANES, :] = x.reshape(rows // SUBLANES, SUBLANES, HEAD_DIM)


def _segment_offsets(rows, levels):
    off = [0]
    for lvl in range(levels):
        off.append(off[-1] + (rows >> lvl))
    return off


def _segment_products(rows, levels, nat_ref, seg_ref):
    off = _segment_offsets(rows, levels)
    for t in range(rows):
        seg_ref[t] = _token_tile(nat_ref, 2, t)
    for lvl in range(1, levels):
        for m in range(rows >> lvl):
            seg_ref[off[lvl] + m] = seg_ref[off[lvl - 1] + 2 * m] * seg_ref[off[lvl - 1] + 2 * m + 1]


def _token_levels(tokens, rows, levels, nat_ref, seg_ref, lev_ref):
    off = _segment_offsets(rows, levels)
    prefix = []
    for t in tokens:
        tok = slice(t * HEADS, (t + 1) * HEADS)
        q_t, k_t = _token_tile(nat_ref, 0, t), _token_tile(nat_ref, 1, t)
        p_t, r_t = seg_ref[t], None
        for lvl in range(levels):
            m = t >> lvl
            sibling = seg_ref[off[lvl] + (m ^ 1)]
            if m & 1:
                lev_ref[lvl, tok, :] = q_t * p_t
                p_t = p_t * sibling
            else:
                lev_ref[lvl, tok, :] = k_t if r_t is None else k_t * r_t
                r_t = sibling if r_t is None else r_t * sibling
        lev_ref[levels, tok, :] = q_t * p_t
        lev_ref[levels + 1, tok, :] = k_t if r_t is None else k_t * r_t
        prefix.append(p_t)
    return prefix


def _level_scratch(rows, levels):
    return [
        pltpu.VMEM((3, rows // SUBLANES, HEADS * SUBLANES, HEAD_DIM), F32),
        pltpu.VMEM((2 * rows, HEADS, HEAD_DIM), F32),
        pltpu.VMEM((2, rows, HG_WIDTH), BF16),
        pltpu.VMEM((levels + 2, rows * HEADS, HEAD_DIM), F32),
        pltpu.VMEM((HEADS, rows, rows), F32),
    ]


def _level_masks(rows, levels):
    ti = lax.broadcasted_iota(jnp.int32, (rows, rows), 0)
    si = lax.broadcasted_iota(jnp.int32, (rows, rows), 1)
    x = ti ^ si
    masks = []
    for lvl in range(levels):
        h = 1 << lvl
        if h < SUBLANES:
            masks.append(((x >> lvl) == 1) & (((ti >> lvl) & 1) == 1))
        else:
            half = lax.broadcasted_iota(jnp.int32, (h, rows), 1) >> lvl
            masks.append([half == 2 * b for b in range(rows // (2 * h))])
    return masks, ti == si


def _nt(a, b):
    return lax.dot_general(a, b, (((1,), (1,)), ((), ())), preferred_element_type=F32)


def _score_level(j, lvl, rows, masks, eye, qk_ref, lev_ref, sc_ref):
    c0 = j * HEAD_DIM
    sc_ref = sc_ref.at[j]
    if lvl < 0:
        pltpu.store(sc_ref, _nt(qk_ref[0, :, c0:c0 + HEAD_DIM], qk_ref[1, :, c0:c0 + HEAD_DIM]), mask=eye)
        return
    h = 1 << lvl
    x32 = _head_rows(lev_ref, lvl, j, rows)
    x = x32.astype(BF16)
    if h < SUBLANES:
        pltpu.store(sc_ref, _nt(x, x), mask=masks[lvl])
        return
    blocks = rows // (2 * h)
    upper = x32.reshape(blocks, 2 * h, HEAD_DIM)[:, h:].reshape(rows // 2, HEAD_DIM).astype(BF16)
    s = _nt(upper, x)
    for b in range(blocks):
        pltpu.store(sc_ref.at[b * 2 * h + h:(b + 1) * 2 * h, :], s[b * h:(b + 1) * h], mask=masks[lvl][b])


def _intra_scores(j, rows, levels, masks, eye, qk_ref, lev_ref, sc_ref):
    for lvl in range(-1, levels):
        _score_level(j, lvl, rows, masks, eye, qk_ref, lev_ref, sc_ref)
    return sc_ref[j]


def _window_sums(e, w, axis, hist, rows):
    s, d = e, 1
    while d < min(w, SUBLANES):
        s = s + pltpu.roll(s, d, axis)
        d *= 2
    take = lambda a, lo: lax.slice_in_dim(a, lo, lo + rows, axis=axis)
    out = take(s, hist)
    if w > SUBLANES:
        assert w == 2 * SUBLANES
        out = out + take(s, hist - SUBLANES)
    return out


def _head_output(o, j, zg, hgn_ref, mix_ref):
    c0 = j * HEAD_DIM
    o = _rms(o, hgn_ref[:, c0:c0 + HEAD_DIM]) * (zg * _sigmoid(zg))
    mix_ref[:, POOL_WIDTH + c0:POOL_WIDTH + c0 + HEAD_DIM] = o.astype(BF16)


PROMPT_ROWS = 128
PROMPT_LEVELS = 7


PROMPT_SLOTS = 2


def _prompt_prepare(z_blk, t_blk, starts_seq, slot, lb, wpool_ref, pscale_ref,
                    ext_ref, pool_ref, decay_ref, nat_ref, seg_ref, qk_ref, lev_ref):
    rows, levels, hist = PROMPT_ROWS, PROMPT_LEVELS, HIST_ROWS
    nat, seg, qk, lev = nat_ref[slot], seg_ref[slot], qk_ref[slot], lev_ref[slot]

    def pool_group(gi):
        def run():
            if gi == 0:
                ext_ref[0:hist, :] = jnp.where(starts_seq, 0.0, ext_ref[rows:rows + hist, :])
                ext_ref[hist:hist + rows, :] = z_blk[:, 0:POOL_WIDTH]
            w, c0 = POOL_WINDOWS[gi], gi * POOL_GROUP_DIM
            e = ext_ref[:, c0:c0 + POOL_GROUP_DIM]
            pos = t_blk * rows + lax.broadcasted_iota(jnp.int32, (rows, 1), 0)
            cnt = jnp.minimum(pos + 1, w).astype(F32)
            _pool_project(_window_sums(e, w, 0, hist, rows), cnt, e[hist:hist + rows], gi,
                          wpool_ref, pscale_ref, pool_ref[slot])
        return run

    def gates(c):
        cols = slice(c * HEAD_DIM, (c + 1) * HEAD_DIM)
        return lambda: _head_gates(z_blk[:, Z_Q + c * HEAD_DIM:Z_Q + (c + 1) * HEAD_DIM],
                                   z_blk[:, Z_F + c * HEAD_DIM:Z_F + (c + 1) * HEAD_DIM], lb[:, cols], c, rows, nat, qk)

    def tokens(k, n):
        def run():
            prefix = _token_levels(range(k * n, (k + 1) * n), rows, levels, nat, seg, lev)
            if (k + 1) * n == rows:
                decay_ref[slot][...] = prefix[-1]
        return run

    a, b = [pool_group(g) for g in range(len(POOL_WINDOWS))], [gates(c) for c in range(HEADS)]
    t = [tokens(k, rows // HEADS) for k in range(HEADS)]
    segments = lambda: _segment_products(rows, levels, nat, seg)
    return [[a[0], a[1], b[0]], [a[2], a[3], b[1]], b[2:5], b[5:8], [segments, t[0]], t[1:3], t[3:6], t[6:8]]


def _prompt_heads(slot, z_blk, mix_blk, hgn_ref, s_ref, pool_ref, decay_ref, qk_ref, lev_ref, sc_ref,
                  masks, eye):
    rows, levels = PROMPT_ROWS, PROMPT_LEVELS
    qk, lev, sc = qk_ref[slot], lev_ref[slot], sc_ref[slot]
    value = lambda j: z_blk[:, Z_I + j * HEAD_DIM:Z_I + (j + 1) * HEAD_DIM].astype(BF16)

    def pool_part():
        mix_blk[:, 0:POOL_WIDTH] = pool_ref[slot][...]

    def output(j):
        c0 = j * HEAD_DIM
        qt = _head_rows(lev, levels, j, rows).astype(BF16)
        o = (jnp.dot(sc[j].astype(BF16), value(j), preferred_element_type=F32)
             + jnp.dot(qt, s_ref[j].astype(BF16), preferred_element_type=F32))
        _head_output(o, j, z_blk[:, Z_G + c0:Z_G + c0 + HEAD_DIM], hgn_ref, mix_blk)

    def state(j):
        decay_t = jnp.transpose(decay_ref[slot][...])
        kt = _head_rows(lev, levels + 1, j, rows).astype(BF16)
        s_ref[j] = decay_t[:, j:j + 1] * s_ref[j] + lax.dot_general(
            kt, value(j), (((0,), (0,)), ((), ())), preferred_element_type=F32)

    heads = range(HEADS)
    scores = [functools.partial(_score_level, j, lvl, rows, masks, eye, qk, lev, sc)
              for lvl in range(-1, levels) for j in heads]
    return ([pool_part] + scores + [functools.partial(output, j) for j in heads]
            + [functools.partial(state, j) for j in heads])


def _prompt_layer_body(xc_ref, xn_ref, g1_ref, win_ref, wo_ref, lbp_ref, wpool_ref, pscale_ref, hgn_ref,
                       x1_ref, pooln_ref, snew_ref,
                       z_ref, mix_ref, ext_ref, s_ref, pool_ref, decay_ref, nat_ref, seg_ref, qk_ref,
                       lev_ref, sc_ref, *, layer, nt):
    rows, hist = PROMPT_ROWS, HIST_ROWS
    i = pl.program_id(0)
    steps_per_seq = nt // PROMPT_SLOTS
    tt = i % steps_per_seq
    lb = _lower_bound(lbp_ref[...], layer)
    one = lambda ref: [ref]
    prepare = functools.partial(
        _prompt_prepare, lb=lb, wpool_ref=wpool_ref, pscale_ref=pscale_ref, ext_ref=ext_ref,
        pool_ref=one(pool_ref), decay_ref=one(decay_ref), nat_ref=one(nat_ref), seg_ref=one(seg_ref),
        qk_ref=one(qk_ref), lev_ref=one(lev_ref))
    masks, eye = _level_masks(rows, PROMPT_LEVELS)
    heads = functools.partial(
        _prompt_heads, hgn_ref=hgn_ref, s_ref=s_ref, pool_ref=one(pool_ref), decay_ref=one(decay_ref),
        qk_ref=one(qk_ref), lev_ref=one(lev_ref), sc_ref=one(sc_ref), masks=masks, eye=eye)

    def in_proj(x_ref):
        h = _rms(x_ref[...], g1_ref[...]).astype(BF16)
        z_ref[...] = jnp.dot(h, win_ref[...], preferred_element_type=F32)

    def mix_block(blk, t_blk, starts_seq):
        z_blk, mix_blk = z_ref.at[blk * rows:(blk + 1) * rows], mix_ref.at[blk * rows:(blk + 1) * rows]
        for batch in prepare(z_blk, t_blk, starts_seq, 0):
            for item in batch:
                item()
        for head in heads(0, z_blk, mix_blk):
            head()

    def out_proj():
        x1_ref[...] = xc_ref[...] + jnp.dot(mix_ref[...], wo_ref[...], preferred_element_type=F32)

    @pl.when(i == 0)
    def _():
        sc_ref[...] = jnp.zeros_like(sc_ref)
        ext_ref[...] = jnp.zeros_like(ext_ref)
        in_proj(xc_ref)

    @pl.when(tt == 0)
    def _():
        s_ref[...] = jnp.zeros_like(s_ref)

    region = pl.when(i < pl.num_programs(0))
    region(functools.partial(mix_block, 0, PROMPT_SLOTS * tt, tt == 0))
    region(functools.partial(mix_block, 1, PROMPT_SLOTS * tt + 1, False))
    region(out_proj)
    region(functools.partial(in_proj, xn_ref))
    pooln_ref[0] = ext_ref[hist + rows - POOL_CACHE:hist + rows, :]

    @pl.when(tt == steps_per_seq - 1)
    def _():
        snew_ref[0] = s_ref[...]


def _prompt_layer(x, g1, w_in, w_o, lbp, wpool, pscale, hgn, batch, seq, layer):
    rows, slots = PROMPT_ROWS, PROMPT_SLOTS
    nt = seq // rows
    assert seq % (rows * slots) == 0
    nblk = batch * nt
    steps_per_seq = nt // slots
    const2 = lambda i: (0, 0)
    resident = lambda shape: pl.BlockSpec(shape, const2, pipeline_mode=pl.Buffered(1))
    return pl.pallas_call(
        functools.partial(_prompt_layer_body, layer=layer, nt=nt),
        grid=(nblk // slots,),
        in_specs=[
            pl.BlockSpec((slots * rows, D_MODEL), lambda i: (i, 0)),
            pl.BlockSpec((slots * rows, D_MODEL), lambda i: (jnp.minimum(i + 1, nblk // slots - 1), 0)),
            pl.BlockSpec((1, D_MODEL), const2),
            resident(w_in.shape),
            resident(w_o.shape),
            pl.BlockSpec(lbp.shape, const2),
            pl.BlockSpec(wpool.shape, lambda i: (0, 0, 0)),
            pl.BlockSpec((1, POOL_WIDTH), const2),
            pl.BlockSpec((1, HG_WIDTH), const2),
        ],
        out_specs=[
            pl.BlockSpec((slots * rows, D_MODEL), lambda i: (i, 0)),
            pl.BlockSpec((1, POOL_CACHE, POOL_WIDTH), lambda i: (i // steps_per_seq, 0, 0)),
            pl.BlockSpec((1, HEADS, HEAD_DIM, HEAD_DIM), lambda i: (i // steps_per_seq, 0, 0, 0)),
        ],
        out_shape=[
            jax.ShapeDtypeStruct((batch * seq, D_MODEL), F32),
            jax.ShapeDtypeStruct((batch, POOL_CACHE, POOL_WIDTH), F32),
            jax.ShapeDtypeStruct((batch, HEADS, HEAD_DIM, HEAD_DIM), F32),
        ],
        scratch_shapes=[
            pltpu.VMEM((slots * rows, IN_COLS), F32),
            pltpu.VMEM((slots * rows, D_MODEL), BF16),
            pltpu.VMEM((HIST_ROWS + rows, POOL_WIDTH), F32),
            pltpu.VMEM((HEADS, HEAD_DIM, HEAD_DIM), F32),
            pltpu.VMEM((rows, POOL_WIDTH), BF16),
            pltpu.VMEM((HEADS, HEAD_DIM), F32),
            *_level_scratch(rows, PROMPT_LEVELS),
        ],
        compiler_params=_cparams("prompt_layer", ("arbitrary",)),
        name="prompt_layer",
    )(x, x, g1, w_in, w_o, lbp, wpool, pscale, hgn)


def _mixer_sample_body(z_ref, cache_ref, sin_ref, lbp_ref, wpool_ref, pscale_ref, hgn_ref,
                       mix_ref, pooln_ref, snew_ref,
                       ext_ref, nat_ref, seg_ref, qk_ref, lev_ref, sc_ref, *, layer, nseq, steps, pos0):
    rows = nseq * steps
    levels = steps.bit_length() - 1
    hist = HIST_ROWS

    @pl.when(pl.program_id(0) == 0)
    def _():
        sc_ref[...] = jnp.zeros_like(sc_ref)

    u_nat = z_ref[:, 0:POOL_WIDTH]
    ext_ref[0:hist - POOL_CACHE] = jnp.zeros((hist - POOL_CACHE, nseq, POOL_WIDTH), F32)
    ext_ref[hist - POOL_CACHE:hist] = cache_ref[0]
    ext_ref[hist:hist + steps] = jnp.swapaxes(u_nat.reshape(nseq, steps, POOL_WIDTH), 0, 1)
    pos = pos0 + lax.broadcasted_iota(jnp.int32, (steps, 1, 1), 0)
    for gi, w in enumerate(POOL_WINDOWS):
        c0 = gi * POOL_GROUP_DIM
        acc = ext_ref[hist:hist + steps, :, c0:c0 + POOL_GROUP_DIM]
        for d in range(1, w):
            acc = acc + ext_ref[hist - d:hist - d + steps, :, c0:c0 + POOL_GROUP_DIM]
        mean = acc / jnp.minimum(pos + 1, w).astype(F32)
        mean = jnp.swapaxes(mean, 0, 1).reshape(rows, POOL_GROUP_DIM)
        _pool_project(mean, 1.0, u_nat[:, c0:c0 + POOL_GROUP_DIM], gi, wpool_ref, pscale_ref, mix_ref)
    pooln_ref[...] = ext_ref[hist + steps - POOL_CACHE:hist + steps]

    lb = _lower_bound(lbp_ref[...], layer)
    for c in range(HEADS):
        cols = slice(c * HEAD_DIM, (c + 1) * HEAD_DIM)
        _head_gates(z_ref[:, Z_Q + c * HEAD_DIM:Z_Q + (c + 1) * HEAD_DIM],
                    z_ref[:, Z_F + c * HEAD_DIM:Z_F + (c + 1) * HEAD_DIM], lb[:, cols], c, rows, nat_ref, qk_ref)
    _segment_products(rows, levels, nat_ref, seg_ref)
    prefix = _token_levels(range(rows), rows, levels, nat_ref, seg_ref, lev_ref)
    decay = jnp.stack(prefix[steps - 1::steps])
    decay_t = jnp.swapaxes(decay, 1, 2)
    masks, eye = _level_masks(rows, levels)
    for j in range(HEADS):
        c0 = j * HEAD_DIM
        scores = _intra_scores(j, rows, levels, masks, eye, qk_ref, lev_ref, sc_ref)
        v32 = z_ref[:, Z_I + c0:Z_I + c0 + HEAD_DIM]
        v = v32.astype(BF16)
        v3 = v32.reshape(nseq, steps, HEAD_DIM).astype(BF16)
        qt3 = _head_rows(lev_ref, levels, j, rows).reshape(nseq, steps, HEAD_DIM).astype(BF16)
        kt3 = _head_rows(lev_ref, levels + 1, j, rows).reshape(nseq, steps, HEAD_DIM).astype(BF16)
        s_in = sin_ref[:, j]
        o_state = lax.dot_general(qt3, s_in.astype(BF16), (((2,), (1,)), ((0,), (0,))),
                                  preferred_element_type=F32)
        o = jnp.dot(scores.astype(BF16), v, preferred_element_type=F32) + o_state.reshape(rows, HEAD_DIM)
        upd = lax.dot_general(kt3, v3, (((1,), (1,)), ((0,), (0,))), preferred_element_type=F32)
        snew_ref[:, j] = decay_t[:, :, j:j + 1] * s_in + upd
        _head_output(o, j, z_ref[:, Z_G + c0:Z_G + c0 + HEAD_DIM], hgn_ref, mix_ref)


def _mixer_sample(z, cache, state, lbp, wpool, pscale, hgn, nseq_total, steps, pos0, layer, nseq):
    rows = nseq * steps
    levels = steps.bit_length() - 1
    assert 1 << levels == steps and nseq_total % nseq == 0
    hist = HIST_ROWS
    const2 = lambda i: (0, 0)
    return pl.pallas_call(
        functools.partial(_mixer_sample_body, layer=layer, nseq=nseq, steps=steps, pos0=pos0),
        grid=(nseq_total // nseq,),
        in_specs=[
            pl.BlockSpec((rows, IN_COLS), lambda i: (i, 0)),
            pl.BlockSpec((1, POOL_CACHE, nseq, POOL_WIDTH), lambda i: (layer, 0, i, 0)),
            pl.BlockSpec((nseq, HEADS, HEAD_DIM, HEAD_DIM), lambda i: (i, 0, 0, 0)),
            pl.BlockSpec(lbp.shape, const2),
            pl.BlockSpec(wpool.shape, lambda i: (0, 0, 0)),
            pl.BlockSpec((1, POOL_WIDTH), const2),
            pl.BlockSpec((1, HG_WIDTH), const2),
        ],
        out_specs=[
            pl.BlockSpec((rows, D_MODEL), lambda i: (i, 0)),
            pl.BlockSpec((POOL_CACHE, nseq, POOL_WIDTH), lambda i: (0, i, 0)),
            pl.BlockSpec((nseq, HEADS, HEAD_DIM, HEAD_DIM), lambda i: (i, 0, 0, 0)),
        ],
        out_shape=[
            jax.ShapeDtypeStruct((nseq_total * steps, D_MODEL), BF16),
            jax.ShapeDtypeStruct((POOL_CACHE, nseq_total, POOL_WIDTH), F32),
            jax.ShapeDtypeStruct((nseq_total, HEADS, HEAD_DIM, HEAD_DIM), F32),
        ],
        scratch_shapes=[
            pltpu.VMEM((hist + steps, nseq, POOL_WIDTH), F32),
            *_level_scratch(rows, levels),
        ],
        compiler_params=_cparams("mixer_sample", ("arbitrary",)),
        name="mixer_sample",
    )(z, cache, state, lbp, wpool, pscale, hgn)


def _out_proj_body(x_ref, mix_ref, w_ref, o_ref, wb_ref):
    w = w_ref[...].astype(BF16)
    wb_ref[...] = w
    o_ref[...] = x_ref[...] + jnp.dot(mix_ref[...], w, preferred_element_type=F32)


def _out_proj(x, mix, w, tm, tn):
    m = x.shape[0]
    return pl.pallas_call(
        _out_proj_body,
        grid=(m // tm, D_MODEL // tn),
        in_specs=[
            pl.BlockSpec((tm, tn), lambda i, j: (i, j)),
            pl.BlockSpec((tm, D_MODEL), lambda i, j: (i, 0)),
            pl.BlockSpec((D_MODEL, tn), lambda i, j: (0, j)),
        ],
        out_specs=[pl.BlockSpec((tm, tn), lambda i, j: (i, j)), pl.BlockSpec((D_MODEL, tn), lambda i, j: (0, j))],
        out_shape=[jax.ShapeDtypeStruct((m, D_MODEL), F32), jax.ShapeDtypeStruct(w.shape, BF16)],
        compiler_params=_cparams("out_proj", ("arbitrary", "arbitrary")),
        name="out_proj",
    )(x, mix, w)


def _ffn_body(x_ref, g2_ref, wg_ref, wu_ref, wd_ref, gf_ref, y_ref, *rest):
    *wb_refs, h_ref = rest
    j = pl.program_id(1)
    last = pl.num_programs(1) - 1
    chunks = _row_chunks(x_ref.shape[0])

    def down(h, wg, wu, wd):
        gate = jnp.dot(h, wg, preferred_element_type=F32)
        up = jnp.dot(h, wu, preferred_element_type=F32)
        act = (gate * _sigmoid(gate) * up).astype(BF16)
        return jnp.dot(act, wd, preferred_element_type=F32)

    def weights():
        if not wb_refs:
            return wg_ref[...], wu_ref[...], wd_ref[...]
        w = tuple(ref[...].astype(BF16) for ref in (wg_ref, wu_ref, wd_ref))
        for wb_ref, wb in zip(wb_refs, w):
            wb_ref[...] = wb
        return w

    @pl.when(j == 0)
    def _():
        w = weights()
        for rs in chunks:
            h = _rms(x_ref[rs, :], g2_ref[...]).astype(BF16)
            h_ref[rs, :] = h
            y_ref[rs, :] = down(h, *w)

    @pl.when((j > 0) & (j < last))
    def _():
        y_ref[...] += down(h_ref[...], *weights())

    @pl.when(j == last)
    def _():
        w = weights()
        for rs in chunks:
            y_ref[rs, :] = _rms(x_ref[rs, :] + y_ref[rs, :] + down(h_ref[rs, :], *w), gf_ref[...])


def _ffn(x, g2, wg, wu, wd, gf, tm, tf):
    m = x.shape[0]
    d_ff = wg.shape[1]
    emit = wg.dtype != BF16
    w_specs = [
        pl.BlockSpec((D_MODEL, tf), lambda i, j: (0, j)),
        pl.BlockSpec((D_MODEL, tf), lambda i, j: (0, j)),
        pl.BlockSpec((tf, D_MODEL), lambda i, j: (j, 0)),
    ]
    x_mode = dict(pipeline_mode=pl.Buffered(1)) if m == tm else {}
    return pl.pallas_call(
        _ffn_body,
        grid=(m // tm, d_ff // tf),
        in_specs=[
            pl.BlockSpec((tm, D_MODEL), lambda i, j: (i, 0), **x_mode),
            pl.BlockSpec((1, D_MODEL), lambda i, j: (0, 0)),
            *w_specs,
            pl.BlockSpec((1, D_MODEL), lambda i, j: (0, 0)),
        ],
        out_specs=[pl.BlockSpec((tm, D_MODEL), lambda i, j: (i, 0))] + (w_specs if emit else []),
        out_shape=[jax.ShapeDtypeStruct((m, D_MODEL), F32)]
        + ([jax.ShapeDtypeStruct(a.shape, BF16) for a in (wg, wu, wd)] if emit else []),
        scratch_shapes=[pltpu.VMEM((tm, D_MODEL), BF16)],
        compiler_params=_cparams("ffn", ("arbitrary", "arbitrary")),
        name="ffn",
    )(x, g2, wg, wu, wd, gf)


def kernel(x_prompt, x_sample, cache_pool, state_hgrn, lb_param, norm1, w_in, w_pool, pool_scale,
           hg_norm, w_o, norm2, w_gate, w_up, w_down, norm_f):
    depth = w_in.shape[0]
    assert depth == 1, "single-layer trunk"
    layer = 0
    batch, seq, _ = x_prompt.shape
    dec_batch, dec_seq, _ = x_sample.shape

    row = lambda a: a.reshape(1, -1).astype(F32)
    g1, g2, gf = row(norm1[layer]), row(norm2[layer]), row(norm_f)
    lbp = lb_param.astype(F32)
    wpool = w_pool[layer].astype(BF16)
    pscale = row(pool_scale[layer])
    hgn = row(hg_norm[layer])

    m_s = dec_batch * dec_seq
    xs = x_sample.reshape(m_s, D_MODEL)
    z_s, w_in_b = _in_proj(xs, g1, w_in[layer], tm=m_s, tn=SAMPLE_PROJ_COLS)
    mix_s, pool_s, s_s = _mixer_sample(z_s, jnp.swapaxes(cache_pool, 1, 2), state_hgrn[layer], lbp, wpool, pscale,
                                       hgn, dec_batch, dec_seq, PAST_LEN, layer, nseq=SAMPLE_SEQS)
    pool_s = jnp.swapaxes(pool_s, 0, 1)
    x1_s, w_o_b = _out_proj(xs, mix_s, w_o[layer], tm=m_s, tn=SAMPLE_PROJ_COLS)
    y_s, w_gate_b, w_up_b, w_down_b = _ffn(x1_s, g2, w_gate[layer], w_up[layer], w_down[layer], gf,
                                           tm=m_s, tf=SAMPLE_FFN_COLS)

    x1_p, pool_p, s_p = _prompt_layer(x_prompt.reshape(batch * seq, D_MODEL), g1, w_in_b, w_o_b,
                                      lbp, wpool, pscale, hgn, batch, seq, layer)
    (y_p,) = _ffn(x1_p, g2, w_gate_b, w_up_b, w_down_b, gf, tm=PROMPT_FFN_ROWS, tf=PROMPT_FFN_COLS)

    return (y_p.reshape(batch, seq, D_MODEL), y_s.reshape(dec_batch, dec_seq, D_MODEL),
            pool_p[None], s_p[None], pool_s[None], s_s[None])
```

```python
import functools

import jax
import jax.numpy as jnp
from jax import lax
from jax.experimental import pallas as pl
from jax.experimental.pallas import tpu as pltpu

D_MODEL = 2048
POOL_WIDTH = 1024
POOL_WINDOWS = (2, 4, 8, 16)
POOL_GROUP_DIM = POOL_WIDTH // len(POOL_WINDOWS)
POOL_CACHE = max(POOL_WINDOWS) - 1
HG_WIDTH = 1024
HEAD_DIM = 128
HEADS = HG_WIDTH // HEAD_DIM
IN_COLS = POOL_WIDTH + 4 * HG_WIDTH
Z_Q, Z_F, Z_I, Z_G = (POOL_WIDTH + n * HG_WIDTH for n in range(4))
EPS = 1e-6
SUBLANES = 8
assert HEADS == SUBLANES
HIST_ROWS = 2 * SUBLANES
PAST_LEN = 16384

F32 = jnp.float32
BF16 = jnp.bfloat16

MIB = 1024 * 1024
SAMPLE_PROJ_COLS = 512
SAMPLE_FFN_COLS = 256
SAMPLE_SEQS = 8
PROMPT_FFN_ROWS = 1024
PROMPT_FFN_COLS = 512
VMEM_MIB = dict(in_proj=48, out_proj=48, mixer_sample=48, ffn=58, prompt_layer=60)


def _cparams(name, sem):
    return pltpu.CompilerParams(dimension_semantics=sem, vmem_limit_bytes=VMEM_MIB[name] * MIB)


def _rms(x, g):
    return x * lax.rsqrt(jnp.mean(x * x, axis=-1, keepdims=True) + EPS) * g


def _sigmoid(x):
    return 1.0 / (1.0 + jnp.exp(-x))


ROW_CHUNK = 256


def _row_chunks(n):
    return [slice(r, r + ROW_CHUNK) for r in range(0, n, ROW_CHUNK)]


def _in_proj_body(x_ref, g_ref, w_ref, z_ref, wb_ref, h_ref):
    j = pl.program_id(1)
    w = w_ref[...].astype(BF16)
    wb_ref[...] = w

    @pl.when(j == 0)
    def _():
        for rs in _row_chunks(x_ref.shape[0]):
            h = _rms(x_ref[rs, :], g_ref[...]).astype(BF16)
            h_ref[rs, :] = h
            z_ref[rs, :] = jnp.dot(h, w, preferred_element_type=F32)

    @pl.when(j > 0)
    def _():
        z_ref[...] = jnp.dot(h_ref[...], w, preferred_element_type=F32)


def _in_proj(x, g, w, tm, tn):
    m = x.shape[0]
    return pl.pallas_call(
        _in_proj_body,
        grid=(m // tm, IN_COLS // tn),
        in_specs=[
            pl.BlockSpec((tm, D_MODEL), lambda i, j: (i, 0)),
            pl.BlockSpec((1, D_MODEL), lambda i, j: (0, 0)),
            pl.BlockSpec((D_MODEL, tn), lambda i, j: (0, j)),
        ],
        out_specs=[pl.BlockSpec((tm, tn), lambda i, j: (i, j)), pl.BlockSpec((D_MODEL, tn), lambda i, j: (0, j))],
        out_shape=[jax.ShapeDtypeStruct((m, IN_COLS), F32), jax.ShapeDtypeStruct(w.shape, BF16)],
        scratch_shapes=[pltpu.VMEM((tm, D_MODEL), BF16)],
        compiler_params=_cparams("in_proj", ("arbitrary", "arbitrary")),
        name="in_proj",
    )(x, g, w)


def _lower_bound(lbp, layer):
    e = jnp.exp(lbp - jnp.max(lbp, axis=0, keepdims=True))
    return jnp.sum(e[: layer + 1], axis=0, keepdims=True) / jnp.sum(e, axis=0, keepdims=True)


def _pool_project(acc, cnt, u, gi, wpool_ref, pscale_ref, mix_ref):
    c0 = gi * POOL_GROUP_DIM
    pooled = acc / cnt - u
    out = jnp.dot(pooled.astype(BF16), wpool_ref[gi], preferred_element_type=F32)
    mix_ref[:, c0:c0 + POOL_GROUP_DIM] = (out * pscale_ref[:, c0:c0 + POOL_GROUP_DIM]).astype(BF16)


def _token_tile(nat_ref, a, t):
    return nat_ref[a, t // SUBLANES, pl.ds(t % SUBLANES, HEADS, stride=SUBLANES), :]


def _head_rows(lev_ref, lvl, j, rows):
    return lev_ref[lvl, pl.ds(j, rows, stride=HEADS), :]


def _head_gates(zq, zf, lb, c, rows, nat_ref, qk_ref):
    fg = lb + (1.0 - lb) * _sigmoid(zf)
    qs = zq * _sigmoid(zq)
    kk = 1.0 - fg
    qk_ref[0, :, c * HEAD_DIM:(c + 1) * HEAD_DIM] = qs.astype(BF16)
    qk_ref[1, :, c * HEAD_DIM:(c + 1) * HEAD_DIM] = kk.astype(BF16)
    for a, x in enumerate((qs, kk, fg)):
        nat_ref[a, :, c * SUBLANES:(c + 1) * SUBLANES, :] = x.reshape(rows // SUBLANES, SUBLANES, HEAD_DIM)


def _segment_offsets(rows, levels):
    off = [0]
    for lvl in range(levels):
        off.append(off[-1] + (rows >> lvl))
    return off


def _segment_products(rows, levels, nat_ref, seg_ref):
    off = _segment_offsets(rows, levels)
    for t in range(rows):
        seg_ref[t] = _token_tile(nat_ref, 2, t)
    for lvl in range(1, levels):
        for m in range(rows >> lvl):
            seg_ref[off[lvl] + m] = seg_ref[off[lvl - 1] + 2 * m] * seg_ref[off[lvl - 1] + 2 * m + 1]


def _token_levels(tokens, rows, levels, nat_ref, seg_ref, lev_ref):
    off = _segment_offsets(rows, levels)
    prefix = []
    for t in tokens:
        tok = slice(t * HEADS, (t + 1) * HEADS)
        q_t, k_t = _token_tile(nat_ref, 0, t), _token_tile(nat_ref, 1, t)
        p_t, r_t = seg_ref[t], None
        for lvl in range(levels):
            m = t >> lvl
            sibling = seg_ref[off[lvl] + (m ^ 1)]
            if m & 1:
                lev_ref[lvl, tok, :] = q_t * p_t
                p_t = p_t * sibling
            else:
                lev_ref[lvl, tok, :] = k_t if r_t is None else k_t * r_t
                r_t = sibling if r_t is None else r_t * sibling
        lev_ref[levels, tok, :] = q_t * p_t
        lev_ref[levels + 1, tok, :] = k_t if r_t is None else k_t * r_t
        prefix.append(p_t)
    return prefix


def _level_scratch(rows, levels):
    return [
        pltpu.VMEM((3, rows // SUBLANES, HEADS * SUBLANES, HEAD_DIM), F32),
        pltpu.VMEM((2 * rows, HEADS, HEAD_DIM), F32),
        pltpu.VMEM((2, rows, HG_WIDTH), BF16),
        pltpu.VMEM((levels + 2, rows * HEADS, HEAD_DIM), F32),
        pltpu.VMEM((HEADS, rows, rows), F32),
    ]


def _level_masks(rows, levels):
    ti = lax.broadcasted_iota(jnp.int32, (rows, rows), 0)
    si = lax.broadcasted_iota(jnp.int32, (rows, rows), 1)
    x = ti ^ si
    masks = []
    for lvl in range(levels):
        h = 1 << lvl
        if h < SUBLANES:
            masks.append(((x >> lvl) == 1) & (((ti >> lvl) & 1) == 1))
        else:
            half = lax.broadcasted_iota(jnp.int32, (h, rows), 1) >> lvl
            masks.append([half == 2 * b for b in range(rows // (2 * h))])
    return masks, ti == si


def _nt(a, b):
    return lax.dot_general(a, b, (((1,), (1,)), ((), ())), preferred_element_type=F32)


def _score_level(j, lvl, rows, masks, eye, qk_ref, lev_ref, sc_ref):
    c0 = j * HEAD_DIM
    sc_ref = sc_ref.at[j]
    if lvl < 0:
        pltpu.store(sc_ref, _nt(qk_ref[0, :, c0:c0 + HEAD_DIM], qk_ref[1, :, c0:c0 + HEAD_DIM]), mask=eye)
        return
    h = 1 << lvl
    x32 = _head_rows(lev_ref, lvl, j, rows)
    x = x32.astype(BF16)
    if h < SUBLANES:
        pltpu.store(sc_ref, _nt(x, x), mask=masks[lvl])
        return
    blocks = rows // (2 * h)
    upper = x32.reshape(blocks, 2 * h, HEAD_DIM)[:, h:].reshape(rows // 2, HEAD_DIM).astype(BF16)
    s = _nt(upper, x)
    for b in range(blocks):
        pltpu.store(sc_ref.at[b * 2 * h + h:(b + 1) * 2 * h, :], s[b * h:(b + 1) * h], mask=masks[lvl][b])


def _intra_scores(j, rows, levels, masks, eye, qk_ref, lev_ref, sc_ref):
    for lvl in range(-1, levels):
        _score_level(j, lvl, rows, masks, eye, qk_ref, lev_ref, sc_ref)
    return sc_ref[j]


def _window_sums(e, w, axis, hist, rows):
    s, d = e, 1
    while d < min(w, SUBLANES):
        s = s + pltpu.roll(s, d, axis)
        d *= 2
    take = lambda a, lo: lax.slice_in_dim(a, lo, lo + rows, axis=axis)
    out = take(s, hist)
    if w > SUBLANES:
        assert w == 2 * SUBLANES
        out = out + take(s, hist - SUBLANES)
    return out


def _head_output(o, j, zg, hgn_ref, mix_ref):
    c0 = j * HEAD_DIM
    o = _rms(o, hgn_ref[:, c0:c0 + HEAD_DIM]) * (zg * _sigmoid(zg))
    mix_ref[:, POOL_WIDTH + c0:POOL_WIDTH + c0 + HEAD_DIM] = o.astype(BF16)


PROMPT_ROWS = 128
PROMPT_LEVELS = 7


PROMPT_SLOTS = 2


def _prompt_prepare(z_blk, t_blk, starts_seq, slot, lb, wpool_ref, pscale_ref,
                    ext_ref, pool_ref, decay_ref, nat_ref, seg_ref, qk_ref, lev_ref):
    rows, levels, hist = PROMPT_ROWS, PROMPT_LEVELS, HIST_ROWS
    nat, seg, qk, lev = nat_ref[slot], seg_ref[slot], qk_ref[slot], lev_ref[slot]

    def pool_group(gi):
        def run():
            if gi == 0:
                ext_ref[0:hist, :] = jnp.where(starts_seq, 0.0, ext_ref[rows:rows + hist, :])
                ext_ref[hist:hist + rows, :] = z_blk[:, 0:POOL_WIDTH]
            w, c0 = POOL_WINDOWS[gi], gi * POOL_GROUP_DIM
            e = ext_ref[:, c0:c0 + POOL_GROUP_DIM]
            pos = t_blk * rows + lax.broadcasted_iota(jnp.int32, (rows, 1), 0)
            cnt = jnp.minimum(pos + 1, w).astype(F32)
            _pool_project(_window_sums(e, w, 0, hist, rows), cnt, e[hist:hist + rows], gi,
                          wpool_ref, pscale_ref, pool_ref[slot])
        return run

    def gates(c):
        cols = slice(c * HEAD_DIM, (c + 1) * HEAD_DIM)
        return lambda: _head_gates(z_blk[:, Z_Q + c * HEAD_DIM:Z_Q + (c + 1) * HEAD_DIM],
                                   z_blk[:, Z_F + c * HEAD_DIM:Z_F + (c + 1) * HEAD_DIM], lb[:, cols], c, rows, nat, qk)

    def tokens(k, n):
        def run():
            prefix = _token_levels(range(k * n, (k + 1) * n), rows, levels, nat, seg, lev)
            if (k + 1) * n == rows:
                decay_ref[slot][...] = prefix[-1]
        return run

    a, b = [pool_group(g) for g in range(len(POOL_WINDOWS))], [gates(c) for c in range(HEADS)]
    t = [tokens(k, rows // HEADS) for k in range(HEADS)]
    segments = lambda: _segment_products(rows, levels, nat, seg)
    return [[a[0], a[1], b[0]], [a[2], a[3], b[1]], b[2:5], b[5:8], [segments, t[0]], t[1:3], t[3:6], t[6:8]]


def _prompt_heads(slot, z_blk, mix_blk, hgn_ref, s_ref, pool_ref, decay_ref, qk_ref, lev_ref, sc_ref,
                  masks, eye):
    rows, levels = PROMPT_ROWS, PROMPT_LEVELS
    qk, lev, sc = qk_ref[slot], lev_ref[slot], sc_ref[slot]
    value = lambda j: z_blk[:, Z_I + j * HEAD_DIM:Z_I + (j + 1) * HEAD_DIM].astype(BF16)

    def pool_part():
        mix_blk[:, 0:POOL_WIDTH] = pool_ref[slot][...]

    def output(j):
        c0 = j * HEAD_DIM
        qt = _head_rows(lev, levels, j, rows).astype(BF16)
        o = (jnp.dot(sc[j].astype(BF16), value(j), preferred_element_type=F32)
             + jnp.dot(qt, s_ref[j].astype(BF16), preferred_element_type=F32))
        _head_output(o, j, z_blk[:, Z_G + c0:Z_G + c0 + HEAD_DIM], hgn_ref, mix_blk)

    def state(j):
        decay_t = jnp.transpose(decay_ref[slot][...])
        kt = _head_rows(lev, levels + 1, j, rows).astype(BF16)
        s_ref[j] = decay_t[:, j:j + 1] * s_ref[j] + lax.dot_general(
            kt, value(j), (((0,), (0,)), ((), ())), preferred_element_type=F32)

    items = [pool_part]
    for j in range(HEADS):
        items += [functools.partial(_score_level, j, lvl, rows, masks, eye, qk, lev, sc)
                  for lvl in range(-1, levels)]
        items += [functools.partial(output, j), functools.partial(state, j)]
    return items


def _prompt_layer_body(xc_ref, xn_ref, g1_ref, win_ref, wo_ref, lbp_ref, wpool_ref, pscale_ref, hgn_ref,
                       x1_ref, pooln_ref, snew_ref,
                       z_ref, mix_ref, ext_ref, s_ref, pool_ref, decay_ref, nat_ref, seg_ref, qk_ref,
                       lev_ref, sc_ref, *, layer, nt):
    rows, hist = PROMPT_ROWS, HIST_ROWS
    i = pl.program_id(0)
    steps_per_seq = nt // PROMPT_SLOTS
    tt = i % steps_per_seq
    lb = _lower_bound(lbp_ref[...], layer)
    one = lambda ref: [ref]
    prepare = functools.partial(
        _prompt_prepare, lb=lb, wpool_ref=wpool_ref, pscale_ref=pscale_ref, ext_ref=ext_ref,
        pool_ref=one(pool_ref), decay_ref=one(decay_ref), nat_ref=one(nat_ref), seg_ref=one(seg_ref),
        qk_ref=one(qk_ref), lev_ref=one(lev_ref))
    masks, eye = _level_masks(rows, PROMPT_LEVELS)
    heads = functools.partial(
        _prompt_heads, hgn_ref=hgn_ref, s_ref=s_ref, pool_ref=one(pool_ref), decay_ref=one(decay_ref),
        qk_ref=one(qk_ref), lev_ref=one(lev_ref), sc_ref=one(sc_ref), masks=masks, eye=eye)

    def in_proj(x_ref):
        h = _rms(x_ref[...], g1_ref[...]).astype(BF16)
        z_ref[...] = jnp.dot(h, win_ref[...], preferred_element_type=F32)

    def mix_block(blk, t_blk, starts_seq):
        z_blk, mix_blk = z_ref.at[blk * rows:(blk + 1) * rows], mix_ref.at[blk * rows:(blk + 1) * rows]
        for batch in prepare(z_blk, t_blk, starts_seq, 0):
            for item in batch:
                item()
        for head in heads(0, z_blk, mix_blk):
            head()

    def out_proj():
        x1_ref[...] = xc_ref[...] + jnp.dot(mix_ref[...], wo_ref[...], preferred_element_type=F32)

    @pl.when(i == 0)
    def _():
        sc_ref[...] = jnp.zeros_like(sc_ref)
        ext_ref[...] = jnp.zeros_like(ext_ref)
        in_proj(xc_ref)

    @pl.when(tt == 0)
    def _():
        s_ref[...] = jnp.zeros_like(s_ref)

    region = pl.when(i < pl.num_programs(0))
    region(functools.partial(mix_block, 0, PROMPT_SLOTS * tt, tt == 0))
    region(functools.partial(mix_block, 1, PROMPT_SLOTS * tt + 1, False))
    region(out_proj)
    region(functools.partial(in_proj, xn_ref))
    pooln_ref[0] = ext_ref[hist + rows - POOL_CACHE:hist + rows, :]

    @pl.when(tt == steps_per_seq - 1)
    def _():
        snew_ref[0] = s_ref[...]


def _prompt_layer(x, g1, w_in, w_o, lbp, wpool, pscale, hgn, batch, seq, layer):
    rows, slots = PROMPT_ROWS, PROMPT_SLOTS
    nt = seq // rows
    assert seq % (rows * slots) == 0
    nblk = batch * nt
    steps_per_seq = nt // slots
    const2 = lambda i: (0, 0)
    resident = lambda shape: pl.BlockSpec(shape, const2, pipeline_mode=pl.Buffered(1))
    return pl.pallas_call(
        functools.partial(_prompt_layer_body, layer=layer, nt=nt),
        grid=(nblk // slots,),
        in_specs=[
            pl.BlockSpec((slots * rows, D_MODEL), lambda i: (i, 0)),
            pl.BlockSpec((slots * rows, D_MODEL), lambda i: (jnp.minimum(i + 1, nblk // slots - 1), 0)),
            pl.BlockSpec((1, D_MODEL), const2),
            resident(w_in.shape),
            resident(w_o.shape),
            pl.BlockSpec(lbp.shape, const2),
            pl.BlockSpec(wpool.shape, lambda i: (0, 0, 0)),
            pl.BlockSpec((1, POOL_WIDTH), const2),
            pl.BlockSpec((1, HG_WIDTH), const2),
        ],
        out_specs=[
            pl.BlockSpec((slots * rows, D_MODEL), lambda i: (i, 0)),
            pl.BlockSpec((1, POOL_CACHE, POOL_WIDTH), lambda i: (i // steps_per_seq, 0, 0)),
            pl.BlockSpec((1, HEADS, HEAD_DIM, HEAD_DIM), lambda i: (i // steps_per_seq, 0, 0, 0)),
        ],
        out_shape=[
            jax.ShapeDtypeStruct((batch * seq, D_MODEL), F32),
            jax.ShapeDtypeStruct((batch, POOL_CACHE, POOL_WIDTH), F32),
            jax.ShapeDtypeStruct((batch, HEADS, HEAD_DIM, HEAD_DIM), F32),
        ],
        scratch_shapes=[
            pltpu.VMEM((slots * rows, IN_COLS), F32),
            pltpu.VMEM((slots * rows, D_MODEL), BF16),
            pltpu.VMEM((HIST_ROWS + rows, POOL_WIDTH), F32),
            pltpu.VMEM((HEADS, HEAD_DIM, HEAD_DIM), F32),
            pltpu.VMEM((rows, POOL_WIDTH), BF16),
            pltpu.VMEM((HEADS, HEAD_DIM), F32),
            *_level_scratch(rows, PROMPT_LEVELS),
        ],
        compiler_params=_cparams("prompt_layer", ("arbitrary",)),
        name="prompt_layer",
    )(x, x, g1, w_in, w_o, lbp, wpool, pscale, hgn)


def _mixer_sample_body(z_ref, cache_ref, sin_ref, lbp_ref, wpool_ref, pscale_ref, hgn_ref,
                       mix_ref, pooln_ref, snew_ref,
                       ext_ref, nat_ref, seg_ref, qk_ref, lev_ref, sc_ref, *, layer, nseq, steps, pos0):
    rows = nseq * steps
    levels = steps.bit_length() - 1
    hist = HIST_ROWS

    @pl.when(pl.program_id(0) == 0)
    def _():
        sc_ref[...] = jnp.zeros_like(sc_ref)

    u_nat = z_ref[:, 0:POOL_WIDTH]
    ext_ref[0:hist - POOL_CACHE] = jnp.zeros((hist - POOL_CACHE, nseq, POOL_WIDTH), F32)
    ext_ref[hist - POOL_CACHE:hist] = cache_ref[0]
    ext_ref[hist:hist + steps] = jnp.swapaxes(u_nat.reshape(nseq, steps, POOL_WIDTH), 0, 1)
    pos = pos0 + lax.broadcasted_iota(jnp.int32, (steps, 1, 1), 0)
    for gi, w in enumerate(POOL_WINDOWS):
        c0 = gi * POOL_GROUP_DIM
        acc = ext_ref[hist:hist + steps, :, c0:c0 + POOL_GROUP_DIM]
        for d in range(1, w):
            acc = acc + ext_ref[hist - d:hist - d + steps, :, c0:c0 + POOL_GROUP_DIM]
        mean = acc / jnp.minimum(pos + 1, w).astype(F32)
        mean = jnp.swapaxes(mean, 0, 1).reshape(rows, POOL_GROUP_DIM)
        _pool_project(mean, 1.0, u_nat[:, c0:c0 + POOL_GROUP_DIM], gi, wpool_ref, pscale_ref, mix_ref)
    pooln_ref[...] = ext_ref[hist + steps - POOL_CACHE:hist + steps]

    lb = _lower_bound(lbp_ref[...], layer)
    for c in range(HEADS):
        cols = slice(c * HEAD_DIM, (c + 1) * HEAD_DIM)
        _head_gates(z_ref[:, Z_Q + c * HEAD_DIM:Z_Q + (c + 1) * HEAD_DIM],
                    z_ref[:, Z_F + c * HEAD_DIM:Z_F + (c + 1) * HEAD_DIM], lb[:, cols], c, rows, nat_ref, qk_ref)
    _segment_products(rows, levels, nat_ref, seg_ref)
    prefix = _token_levels(range(rows), rows, levels, nat_ref, seg_ref, lev_ref)
    decay = jnp.stack(prefix[steps - 1::steps])
    decay_t = jnp.swapaxes(decay, 1, 2)
    masks, eye = _level_masks(rows, levels)
    for j in range(HEADS):
        c0 = j * HEAD_DIM
        scores = _intra_scores(j, rows, levels, masks, eye, qk_ref, lev_ref, sc_ref)
        v32 = z_ref[:, Z_I + c0:Z_I + c0 + HEAD_DIM]
        v = v32.astype(BF16)
        v3 = v32.reshape(nseq, steps, HEAD_DIM).astype(BF16)
        qt3 = _head_rows(lev_ref, levels, j, rows).reshape(nseq, steps, HEAD_DIM).astype(BF16)
        kt3 = _head_rows(lev_ref, levels + 1, j, rows).reshape(nseq, steps, HEAD_DIM).astype(BF16)
        s_in = sin_ref[:, j]
        o_state = lax.dot_general(qt3, s_in.astype(BF16), (((2,), (1,)), ((0,), (0,))),
                                  preferred_element_type=F32)
        o = jnp.dot(scores.astype(BF16), v, preferred_element_type=F32) + o_state.reshape(rows, HEAD_DIM)
        upd = lax.dot_general(kt3, v3, (((1,), (1,)), ((0,), (0,))), preferred_element_type=F32)
        snew_ref[:, j] = decay_t[:, :, j:j + 1] * s_in + upd
        _head_output(o, j, z_ref[:, Z_G + c0:Z_G + c0 + HEAD_DIM], hgn_ref, mix_ref)


def _mixer_sample(z, cache, state, lbp, wpool, pscale, hgn, nseq_total, steps, pos0, layer, nseq):
    rows = nseq * steps
    levels = steps.bit_length() - 1
    assert 1 << levels == steps and nseq_total % nseq == 0
    hist = HIST_ROWS
    const2 = lambda i: (0, 0)
    return pl.pallas_call(
        functools.partial(_mixer_sample_body, layer=layer, nseq=nseq, steps=steps, pos0=pos0),
        grid=(nseq_total // nseq,),
        in_specs=[
            pl.BlockSpec((rows, IN_COLS), lambda i: (i, 0)),
            pl.BlockSpec((1, POOL_CACHE, nseq, POOL_WIDTH), lambda i: (layer, 0, i, 0)),
            pl.BlockSpec((nseq, HEADS, HEAD_DIM, HEAD_DIM), lambda i: (i, 0, 0, 0)),
            pl.BlockSpec(lbp.shape, const2),
            pl.BlockSpec(wpool.shape, lambda i: (0, 0, 0)),
            pl.BlockSpec((1, POOL_WIDTH), const2),
            pl.BlockSpec((1, HG_WIDTH), const2),
        ],
        out_specs=[
            pl.BlockSpec((rows, D_MODEL), lambda i: (i, 0)),
            pl.BlockSpec((POOL_CACHE, nseq, POOL_WIDTH), lambda i: (0, i, 0)),
            pl.BlockSpec((nseq, HEADS, HEAD_DIM, HEAD_DIM), lambda i: (i, 0, 0, 0)),
        ],
        out_shape=[
            jax.ShapeDtypeStruct((nseq_total * steps, D_MODEL), BF16),
            jax.ShapeDtypeStruct((POOL_CACHE, nseq_total, POOL_WIDTH), F32),
            jax.ShapeDtypeStruct((nseq_total, HEADS, HEAD_DIM, HEAD_DIM), F32),
        ],
        scratch_shapes=[
            pltpu.VMEM((hist + steps, nseq, POOL_WIDTH), F32),
            *_level_scratch(rows, levels),
        ],
        compiler_params=_cparams("mixer_sample", ("arbitrary",)),
        name="mixer_sample",
    )(z, cache, state, lbp, wpool, pscale, hgn)


def _out_proj_body(x_ref, mix_ref, w_ref, o_ref, wb_ref):
    w = w_ref[...].astype(BF16)
    wb_ref[...] = w
    o_ref[...] = x_ref[...] + jnp.dot(mix_ref[...], w, preferred_element_type=F32)


def _out_proj(x, mix, w, tm, tn):
    m = x.shape[0]
    return pl.pallas_call(
        _out_proj_body,
        grid=(m // tm, D_MODEL // tn),
        in_specs=[
            pl.BlockSpec((tm, tn), lambda i, j: (i, j)),
            pl.BlockSpec((tm, D_MODEL), lambda i, j: (i, 0)),
            pl.BlockSpec((D_MODEL, tn), lambda i, j: (0, j)),
        ],
        out_specs=[pl.BlockSpec((tm, tn), lambda i, j: (i, j)), pl.BlockSpec((D_MODEL, tn), lambda i, j: (0, j))],
        out_shape=[jax.ShapeDtypeStruct((m, D_MODEL), F32), jax.ShapeDtypeStruct(w.shape, BF16)],
        compiler_params=_cparams("out_proj", ("arbitrary", "arbitrary")),
        name="out_proj",
    )(x, mix, w)


def _ffn_body(x_ref, g2_ref, wg_ref, wu_ref, wd_ref, gf_ref, y_ref, *rest):
    *wb_refs, h_ref = rest
    j = pl.program_id(1)
    last = pl.num_programs(1) - 1
    chunks = _row_chunks(x_ref.shape[0])

    def down(h, wg, wu, wd):
        gate = jnp.dot(h, wg, preferred_element_type=F32)
        up = jnp.dot(h, wu, preferred_element_type=F32)
        act = (gate * _sigmoid(gate) * up).astype(BF16)
        return jnp.dot(act, wd, preferred_element_type=F32)

    def weights():
        if not wb_refs:
            return wg_ref[...], wu_ref[...], wd_ref[...]
        w = tuple(ref[...].astype(BF16) for ref in (wg_ref, wu_ref, wd_ref))
        for wb_ref, wb in zip(wb_refs, w):
            wb_ref[...] = wb
        return w

    @pl.when(j == 0)
    def _():
        w = weights()
        for rs in chunks:
            h = _rms(x_ref[rs, :], g2_ref[...]).astype(BF16)
            h_ref[rs, :] = h
            y_ref[rs, :] = down(h, *w)

    @pl.when((j > 0) & (j < last))
    def _():
        y_ref[...] += down(h_ref[...], *weights())

    @pl.when(j == last)
    def _():
        w = weights()
        for rs in chunks:
            y_ref[rs, :] = _rms(x_ref[rs, :] + y_ref[rs, :] + down(h_ref[rs, :], *w), gf_ref[...])


def _ffn(x, g2, wg, wu, wd, gf, tm, tf):
    m = x.shape[0]
    d_ff = wg.shape[1]
    emit = wg.dtype != BF16
    w_specs = [
        pl.BlockSpec((D_MODEL, tf), lambda i, j: (0, j)),
        pl.BlockSpec((D_MODEL, tf), lambda i, j: (0, j)),
        pl.BlockSpec((tf, D_MODEL), lambda i, j: (j, 0)),
    ]
    x_mode = dict(pipeline_mode=pl.Buffered(1)) if m == tm else {}
    return pl.pallas_call(
        _ffn_body,
        grid=(m // tm, d_ff // tf),
        in_specs=[
            pl.BlockSpec((tm, D_MODEL), lambda i, j: (i, 0), **x_mode),
            pl.BlockSpec((1, D_MODEL), lambda i, j: (0, 0)),
            *w_specs,
            pl.BlockSpec((1, D_MODEL), lambda i, j: (0, 0)),
        ],
        out_specs=[pl.BlockSpec((tm, D_MODEL), lambda i, j: (i, 0))] + (w_specs if emit else []),
        out_shape=[jax.ShapeDtypeStruct((m, D_MODEL), F32)]
        + ([jax.ShapeDtypeStruct(a.shape, BF16) for a in (wg, wu, wd)] if emit else []),
        scratch_shapes=[pltpu.VMEM((tm, D_MODEL), BF16)],
        compiler_params=_cparams("ffn", ("arbitrary", "arbitrary")),
        name="ffn",
    )(x, g2, wg, wu, wd, gf)


def kernel(x_prompt, x_sample, cache_pool, state_hgrn, lb_param, norm1, w_in, w_pool, pool_scale,
           hg_norm, w_o, norm2, w_gate, w_up, w_down, norm_f):
    depth = w_in.shape[0]
    assert depth == 1, "single-layer trunk"
    layer = 0
    batch, seq, _ = x_prompt.shape
    dec_batch, dec_seq, _ = x_sample.shape

    row = lambda a: a.reshape(1, -1).astype(F32)
    g1, g2, gf = row(norm1[layer]), row(norm2[layer]), row(norm_f)
    lbp = lb_param.astype(F32)
    wpool = w_pool[layer].astype(BF16)
    pscale = row(pool_scale[layer])
    hgn = row(hg_norm[layer])

    m_s = dec_batch * dec_seq
    xs = x_sample.reshape(m_s, D_MODEL)
    z_s, w_in_b = _in_proj(xs, g1, w_in[layer], tm=m_s, tn=SAMPLE_PROJ_COLS)
    mix_s, pool_s, s_s = _mixer_sample(z_s, jnp.swapaxes(cache_pool, 1, 2), state_hgrn[layer], lbp, wpool, pscale,
                                       hgn, dec_batch, dec_seq, PAST_LEN, layer, nseq=SAMPLE_SEQS)
    pool_s = jnp.swapaxes(pool_s, 0, 1)
    x1_s, w_o_b = _out_proj(xs, mix_s, w_o[layer], tm=m_s, tn=SAMPLE_PROJ_COLS)
    y_s, w_gate_b, w_up_b, w_down_b = _ffn(x1_s, g2, w_gate[layer], w_up[layer], w_down[layer], gf,
                                           tm=m_s, tf=SAMPLE_FFN_COLS)

    x1_p, pool_p, s_p = _prompt_layer(x_prompt.reshape(batch * seq, D_MODEL), g1, w_in_b, w_o_b,
                                      lbp, wpool, pscale, hgn, batch, seq, layer)
    (y_p,) = _ffn(x1_p, g2, w_gate_b, w_up_b, w_down_b, gf, tm=PROMPT_FFN_ROWS, tf=PROMPT_FFN_COLS)

    return (y_p.reshape(batch, seq, D_MODEL), y_s.reshape(dec_batch, dec_seq, D_MODEL),
            pool_p[None], s_p[None], pool_s[None], s_s[None])
```

```python
import functools

import jax
import jax.numpy as jnp
from jax import lax
from jax.experimental import pallas as pl
from jax.experimental.pallas import tpu as pltpu

D_MODEL = 2048
POOL_WIDTH = 1024
POOL_WINDOWS = (2, 4, 8, 16)
POOL_GROUP_DIM = POOL_WIDTH // len(POOL_WINDOWS)
POOL_CACHE = max(POOL_WINDOWS) - 1
HG_WIDTH = 1024
HEAD_DIM = 128
HEADS = HG_WIDTH // HEAD_DIM
IN_COLS = POOL_WIDTH + 4 * HG_WIDTH
Z_Q, Z_F, Z_I, Z_G = (POOL_WIDTH + n * HG_WIDTH for n in range(4))
EPS = 1e-6
SUBLANES = 8
assert HEADS == SUBLANES
HIST_ROWS = 2 * SUBLANES
PAST_LEN = 16384

F32 = jnp.float32
BF16 = jnp.bfloat16

MIB = 1024 * 1024
SAMPLE_PROJ_COLS = 512
SAMPLE_FFN_COLS = 256
SAMPLE_SEQS = 8
PROMPT_FFN_ROWS = 1024
PROMPT_FFN_COLS = 512
VMEM_MIB = dict(in_proj=48, out_proj=48, mixer_sample=48, ffn=58, prompt_layer=60)


def _cparams(name, sem):
    return pltpu.CompilerParams(dimension_semantics=sem, vmem_limit_bytes=VMEM_MIB[name] * MIB)


def _rms(x, g):
    return x * lax.rsqrt(jnp.mean(x * x, axis=-1, keepdims=True) + EPS) * g


def _sigmoid(x):
    return 1.0 / (1.0 + jnp.exp(-x))


ROW_CHUNK = 256


def _row_chunks(n):
    return [slice(r, r + ROW_CHUNK) for r in range(0, n, ROW_CHUNK)]


def _in_proj_body(x_ref, g_ref, w_ref, z_ref, wb_ref, h_ref):
    j = pl.program_id(1)
    w = w_ref[...].astype(BF16)
    wb_ref[...] = w

    @pl.when(j == 0)
    def _():
        for rs in _row_chunks(x_ref.shape[0]):
            h = _rms(x_ref[rs, :], g_ref[...]).astype(BF16)
            h_ref[rs, :] = h
            z_ref[rs, :] = jnp.dot(h, w, preferred_element_type=F32)

    @pl.when(j > 0)
    def _():
        z_ref[...] = jnp.dot(h_ref[...], w, preferred_element_type=F32)


def _in_proj(x, g, w, tm, tn):
    m = x.shape[0]
    return pl.pallas_call(
        _in_proj_body,
        grid=(m // tm, IN_COLS // tn),
        in_specs=[
            pl.BlockSpec((tm, D_MODEL), lambda i, j: (i, 0)),
            pl.BlockSpec((1, D_MODEL), lambda i, j: (0, 0)),
            pl.BlockSpec((D_MODEL, tn), lambda i, j: (0, j)),
        ],
        out_specs=[pl.BlockSpec((tm, tn), lambda i, j: (i, j)), pl.BlockSpec((D_MODEL, tn), lambda i, j: (0, j))],
        out_shape=[jax.ShapeDtypeStruct((m, IN_COLS), F32), jax.ShapeDtypeStruct(w.shape, BF16)],
        scratch_shapes=[pltpu.VMEM((tm, D_MODEL), BF16)],
        compiler_params=_cparams("in_proj", ("arbitrary", "arbitrary")),
        name="in_proj",
    )(x, g, w)


def _lower_bound(lbp, layer):
    e = jnp.exp(lbp - jnp.max(lbp, axis=0, keepdims=True))
    return jnp.sum(e[: layer + 1], axis=0, keepdims=True) / jnp.sum(e, axis=0, keepdims=True)


def _pool_project(acc, cnt, u, gi, wpool_ref, pscale_ref, mix_ref):
    c0 = gi * POOL_GROUP_DIM
    pooled = acc / cnt - u
    out = jnp.dot(pooled.astype(BF16), wpool_ref[gi], preferred_element_type=F32)
    mix_ref[:, c0:c0 + POOL_GROUP_DIM] = (out * pscale_ref[:, c0:c0 + POOL_GROUP_DIM]).astype(BF16)


def _token_tile(nat_ref, a, t):
    return nat_ref[a, t // SUBLANES, pl.ds(t % SUBLANES, HEADS, stride=SUBLANES), :]


def _head_rows(lev_ref, lvl, j, rows):
    return lev_ref[lvl, pl.ds(j, rows, stride=HEADS), :]


def _head_gates(zq, zf, lb, c, rows, nat_ref, qk_ref):
    fg = lb + (1.0 - lb) * _sigmoid(zf)
    qs = zq * _sigmoid(zq)
    kk = 1.0 - fg
    qk_ref[0, :, c * HEAD_DIM:(c + 1) * HEAD_DIM] = qs.astype(BF16)
    qk_ref[1, :, c * HEAD_DIM:(c + 1) * HEAD_DIM] = kk.astype(BF16)
    for a, x in enumerate((qs, kk, fg)):
        nat_ref[a, :, c * SUBLANES:(c + 1) * SUBLANES, :] = x.reshape(rows // SUBLANES, SUBLANES, HEAD_DIM)


def _segment_offsets(rows, levels):
    off = [0]
    for lvl in range(levels):
        off.append(off[-1] + (rows >> lvl))
    return off


def _segment_products(rows, levels, nat_ref, seg_ref):
    off = _segment_offsets(rows, levels)
    for t in range(rows):
        seg_ref[t] = _token_tile(nat_ref, 2, t)
    for lvl in range(1, levels):
        for m in range(rows >> lvl):
            seg_ref[off[lvl] + m] = seg_ref[off[lvl - 1] + 2 * m] * seg_ref[off[lvl - 1] + 2 * m + 1]


def _token_levels(tokens, rows, levels, nat_ref, seg_ref, lev_ref):
    off = _segment_offsets(rows, levels)
    prefix = []
    for t in tokens:
        tok = slice(t * HEADS, (t + 1) * HEADS)
        q_t, k_t = _token_tile(nat_ref, 0, t), _token_tile(nat_ref, 1, t)
        p_t, r_t = seg_ref[t], None
        for lvl in range(levels):
            m = t >> lvl
            sibling = seg_ref[off[lvl] + (m ^ 1)]
            if m & 1:
                lev_ref[lvl, tok, :] = q_t * p_t
                p_t = p_t * sibling
            else:
                lev_ref[lvl, tok, :] = k_t if r_t is None else k_t * r_t
                r_t = sibling if r_t is None else r_t * sibling
        lev_ref[levels, tok, :] = q_t * p_t
        lev_ref[levels + 1, tok, :] = k_t if r_t is None else k_t * r_t
        prefix.append(p_t)
    return prefix


def _level_scratch(rows, levels):
    return [
        pltpu.VMEM((3, rows // SUBLANES, HEADS * SUBLANES, HEAD_DIM), F32),
        pltpu.VMEM((2 * rows, HEADS, HEAD_DIM), F32),
        pltpu.VMEM((2, rows, HG_WIDTH), BF16),
        pltpu.VMEM((levels + 2, rows * HEADS, HEAD_DIM), F32),
        pltpu.VMEM((HEADS, rows, rows), F32),
    ]


def _level_masks(rows, levels):
    ti = lax.broadcasted_iota(jnp.int32, (rows, rows), 0)
    si = lax.broadcasted_iota(jnp.int32, (rows, rows), 1)
    x = ti ^ si
    masks = []
    for lvl in range(levels):
        h = 1 << lvl
        if h < SUBLANES:
            masks.append(((x >> lvl) == 1) & (((ti >> lvl) & 1) == 1))
        else:
            half = lax.broadcasted_iota(jnp.int32, (h, rows), 1) >> lvl
            masks.append([half == 2 * b for b in range(rows // (2 * h))])
    return masks, ti == si


def _nt(a, b):
    return lax.dot_general(a, b, (((1,), (1,)), ((), ())), preferred_element_type=F32)


def _score_level(j, lvl, rows, masks, eye, qk_ref, lev_ref, sc_ref):
    c0 = j * HEAD_DIM
    sc_ref = sc_ref.at[j]
    if lvl < 0:
        pltpu.store(sc_ref, _nt(qk_ref[0, :, c0:c0 + HEAD_DIM], qk_ref[1, :, c0:c0 + HEAD_DIM]), mask=eye)
        return
    h = 1 << lvl
    x32 = _head_rows(lev_ref, lvl, j, rows)
    x = x32.astype(BF16)
    if h < SUBLANES:
        pltpu.store(sc_ref, _nt(x, x), mask=masks[lvl])
        return
    blocks = rows // (2 * h)
    upper = x32.reshape(blocks, 2 * h, HEAD_DIM)[:, h:].reshape(rows // 2, HEAD_DIM).astype(BF16)
    s = _nt(upper, x)
    for b in range(blocks):
        pltpu.store(sc_ref.at[b * 2 * h + h:(b + 1) * 2 * h, :], s[b * h:(b + 1) * h], mask=masks[lvl][b])


def _intra_scores(j, rows, levels, masks, eye, qk_ref, lev_ref, sc_ref):
    for lvl in range(-1, levels):
        _score_level(j, lvl, rows, masks, eye, qk_ref, lev_ref, sc_ref)
    return sc_ref[j]


def _window_sums(e, w, axis, hist, rows):
    s, d = e, 1
    while d < min(w, SUBLANES):
        s = s + pltpu.roll(s, d, axis)
        d *= 2
    take = lambda a, lo: lax.slice_in_dim(a, lo, lo + rows, axis=axis)
    out = take(s, hist)
    if w > SUBLANES:
        assert w == 2 * SUBLANES
        out = out + take(s, hist - SUBLANES)
    return out


def _head_output(o, j, zg, hgn_ref, mix_ref):
    c0 = j * HEAD_DIM
    o = _rms(o, hgn_ref[:, c0:c0 + HEAD_DIM]) * (zg * _sigmoid(zg))
    mix_ref[:, POOL_WIDTH + c0:POOL_WIDTH + c0 + HEAD_DIM] = o.astype(BF16)


PROMPT_ROWS = 128
PROMPT_LEVELS = 7


PROMPT_SLOTS = 2


def _prompt_prepare(z_blk, t_blk, starts_seq, slot, lb, wpool_ref, pscale_ref,
                    ext_ref, pool_ref, decay_ref, nat_ref, seg_ref, qk_ref, lev_ref):
    rows, levels, hist = PROMPT_ROWS, PROMPT_LEVELS, HIST_ROWS
    nat, seg, qk, lev = nat_ref[slot], seg_ref[slot], qk_ref[slot], lev_ref[slot]

    def pool_group(gi):
        def run():
            if gi == 0:
                ext_ref[0:hist, :] = jnp.where(starts_seq, 0.0, ext_ref[rows:rows + hist, :])
                ext_ref[hist:hist + rows, :] = z_blk[:, 0:POOL_WIDTH]
            w, c0 = POOL_WINDOWS[gi], gi * POOL_GROUP_DIM
            e = ext_ref[:, c0:c0 + POOL_GROUP_DIM]
            pos = t_blk * rows + lax.broadcasted_iota(jnp.int32, (rows, 1), 0)
            cnt = jnp.minimum(pos + 1, w).astype(F32)
            _pool_project(_window_sums(e, w, 0, hist, rows), cnt, e[hist:hist + rows], gi,
                          wpool_ref, pscale_ref, pool_ref[slot])
        return run

    def gates(c):
        cols = slice(c * HEAD_DIM, (c + 1) * HEAD_DIM)
        return lambda: _head_gates(z_blk[:, Z_Q + c * HEAD_DIM:Z_Q + (c + 1) * HEAD_DIM],
                                   z_blk[:, Z_F + c * HEAD_DIM:Z_F + (c + 1) * HEAD_DIM], lb[:, cols], c, rows, nat, qk)

    def tokens(k, n):
        def run():
            prefix = _token_levels(range(k * n, (k + 1) * n), rows, levels, nat, seg, lev)
            if (k + 1) * n == rows:
                decay_ref[slot][...] = prefix[-1]
        return run

    a, b = [pool_group(g) for g in range(len(POOL_WINDOWS))], [gates(c) for c in range(HEADS)]
    t = [tokens(k, rows // HEADS) for k in range(HEADS)]
    segments = lambda: _segment_products(rows, levels, nat, seg)
    return [[a[0], a[1], b[0]], [a[2], a[3], b[1]], b[2:5], b[5:8], [segments, t[0]], t[1:3], t[3:6], t[6:8]]


def _prompt_heads(slot, z_blk, mix_blk, hgn_ref, s_ref, pool_ref, decay_ref, qk_ref, lev_ref, sc_ref,
                  masks, eye):
    rows, levels = PROMPT_ROWS, PROMPT_LEVELS
    qk, lev, sc = qk_ref[slot], lev_ref[slot], sc_ref[slot]
    value = lambda j: z_blk[:, Z_I + j * HEAD_DIM:Z_I + (j + 1) * HEAD_DIM].astype(BF16)

    def pool_part():
        mix_blk[:, 0:POOL_WIDTH] = pool_ref[slot][...]

    def output(j):
        c0 = j * HEAD_DIM
        qt = _head_rows(lev, levels, j, rows).astype(BF16)
        o = (jnp.dot(sc[j].astype(BF16), value(j), preferred_element_type=F32)
             + jnp.dot(qt, s_ref[j].astype(BF16), preferred_element_type=F32))
        _head_output(o, j, z_blk[:, Z_G + c0:Z_G + c0 + HEAD_DIM], hgn_ref, mix_blk)

    def state(j):
        decay_t = jnp.transpose(decay_ref[slot][...])
        kt = _head_rows(lev, levels + 1, j, rows).astype(BF16)
        s_ref[j] = decay_t[:, j:j + 1] * s_ref[j] + lax.dot_general(
            kt, value(j), (((0,), (0,)), ((), ())), preferred_element_type=F32)

    heads = range(HEADS)
    scores = [functools.partial(_score_level, j, lvl, rows, masks, eye, qk, lev, sc)
              for lvl in range(-1, levels) for j in heads]
    return ([pool_part] + scores + [functools.partial(output, j) for j in heads]
            + [functools.partial(state, j) for j in heads])


def _prompt_layer_body(xc_ref, xn_ref, g1_ref, win_ref, wo_ref, lbp_ref, wpool_ref, pscale_ref, hgn_ref,
                       x1_ref, pooln_ref, snew_ref,
                       z_ref, mix_ref, ext_ref, s_ref, pool_ref, decay_ref, nat_ref, seg_ref, qk_ref,
                       lev_ref, sc_ref, *, layer, nt):
    rows, hist = PROMPT_ROWS, HIST_ROWS
    i = pl.program_id(0)
    steps_per_seq = nt // PROMPT_SLOTS
    tt = i % steps_per_seq
    lb = _lower_bound(lbp_ref[...], layer)
    one = lambda ref: [ref]
    prepare = functools.partial(
        _prompt_prepare, lb=lb, wpool_ref=wpool_ref, pscale_ref=pscale_ref, ext_ref=ext_ref,
        pool_ref=one(pool_ref), decay_ref=one(decay_ref), nat_ref=one(nat_ref), seg_ref=one(seg_ref),
        qk_ref=one(qk_ref), lev_ref=one(lev_ref))
    masks, eye = _level_masks(rows, PROMPT_LEVELS)
    heads = functools.partial(
        _prompt_heads, hgn_ref=hgn_ref, s_ref=s_ref, pool_ref=one(pool_ref), decay_ref=one(decay_ref),
        qk_ref=one(qk_ref), lev_ref=one(lev_ref), sc_ref=one(sc_ref), masks=masks, eye=eye)

    def in_proj(x_ref):
        h = _rms(x_ref[...], g1_ref[...]).astype(BF16)
        z_ref[...] = jnp.dot(h, win_ref[...], preferred_element_type=F32)

    def mix_block(blk, t_blk, starts_seq):
        z_blk, mix_blk = z_ref.at[blk * rows:(blk + 1) * rows], mix_ref.at[blk * rows:(blk + 1) * rows]
        for batch in prepare(z_blk, t_blk, starts_seq, 0):
            for item in batch:
                item()
        for head in heads(0, z_blk, mix_blk):
            head()

    def out_proj():
        x1_ref[...] = xc_ref[...] + jnp.dot(mix_ref[...], wo_ref[...], preferred_element_type=F32)

    @pl.when(i == 0)
    def _():
        sc_ref[...] = jnp.zeros_like(sc_ref)
        ext_ref[...] = jnp.zeros_like(ext_ref)
        in_proj(xc_ref)

    @pl.when(tt == 0)
    def _():
        s_ref[...] = jnp.zeros_like(s_ref)

    steps = pl.num_programs(0)
    pl.when(i < steps)(functools.partial(mix_block, 0, PROMPT_SLOTS * tt, tt == 0))
    pl.when(tt < steps)(functools.partial(mix_block, 1, PROMPT_SLOTS * tt + 1, False))
    pl.when(i >= 0)(out_proj)
    pl.when(tt >= 0)(functools.partial(in_proj, xn_ref))
    pooln_ref[0] = ext_ref[hist + rows - POOL_CACHE:hist + rows, :]

    @pl.when(tt == steps_per_seq - 1)
    def _():
        snew_ref[0] = s_ref[...]


def _prompt_layer(x, g1, w_in, w_o, lbp, wpool, pscale, hgn, batch, seq, layer):
    rows, slots = PROMPT_ROWS, PROMPT_SLOTS
    nt = seq // rows
    assert seq % (rows * slots) == 0
    nblk = batch * nt
    steps_per_seq = nt // slots
    const2 = lambda i: (0, 0)
    resident = lambda shape: pl.BlockSpec(shape, const2, pipeline_mode=pl.Buffered(1))
    return pl.pallas_call(
        functools.partial(_prompt_layer_body, layer=layer, nt=nt),
        grid=(nblk // slots,),
        in_specs=[
            pl.BlockSpec((slots * rows, D_MODEL), lambda i: (i, 0)),
            pl.BlockSpec((slots * rows, D_MODEL), lambda i: (jnp.minimum(i + 1, nblk // slots - 1), 0)),
            pl.BlockSpec((1, D_MODEL), const2),
            resident(w_in.shape),
            resident(w_o.shape),
            pl.BlockSpec(lbp.shape, const2),
            pl.BlockSpec(wpool.shape, lambda i: (0, 0, 0)),
            pl.BlockSpec((1, POOL_WIDTH), const2),
            pl.BlockSpec((1, HG_WIDTH), const2),
        ],
        out_specs=[
            pl.BlockSpec((slots * rows, D_MODEL), lambda i: (i, 0)),
            pl.BlockSpec((1, POOL_CACHE, POOL_WIDTH), lambda i: (i // steps_per_seq, 0, 0)),
            pl.BlockSpec((1, HEADS, HEAD_DIM, HEAD_DIM), lambda i: (i // steps_per_seq, 0, 0, 0)),
        ],
        out_shape=[
            jax.ShapeDtypeStruct((batch * seq, D_MODEL), F32),
            jax.ShapeDtypeStruct((batch, POOL_CACHE, POOL_WIDTH), F32),
            jax.ShapeDtypeStruct((batch, HEADS, HEAD_DIM, HEAD_DIM), F32),
        ],
        scratch_shapes=[
            pltpu.VMEM((slots * rows, IN_COLS), F32),
            pltpu.VMEM((slots * rows, D_MODEL), BF16),
            pltpu.VMEM((HIST_ROWS + rows, POOL_WIDTH), F32),
            pltpu.VMEM((HEADS, HEAD_DIM, HEAD_DIM), F32),
            pltpu.VMEM((rows, POOL_WIDTH), BF16),
            pltpu.VMEM((HEADS, HEAD_DIM), F32),
            *_level_scratch(rows, PROMPT_LEVELS),
        ],
        compiler_params=_cparams("prompt_layer", ("arbitrary",)),
        name="prompt_layer",
    )(x, x, g1, w_in, w_o, lbp, wpool, pscale, hgn)


def _mixer_sample_body(z_ref, cache_ref, sin_ref, lbp_ref, wpool_ref, pscale_ref, hgn_ref,
                       mix_ref, pooln_ref, snew_ref,
                       ext_ref, nat_ref, seg_ref, qk_ref, lev_ref, sc_ref, *, layer, nseq, steps, pos0):
    rows = nseq * steps
    levels = steps.bit_length() - 1
    hist = HIST_ROWS

    @pl.when(pl.program_id(0) == 0)
    def _():
        sc_ref[...] = jnp.zeros_like(sc_ref)

    u_nat = z_ref[:, 0:POOL_WIDTH]
    ext_ref[0:hist - POOL_CACHE] = jnp.zeros((hist - POOL_CACHE, nseq, POOL_WIDTH), F32)
    ext_ref[hist - POOL_CACHE:hist] = cache_ref[0]
    ext_ref[hist:hist + steps] = jnp.swapaxes(u_nat.reshape(nseq, steps, POOL_WIDTH), 0, 1)
    pos = pos0 + lax.broadcasted_iota(jnp.int32, (steps, 1, 1), 0)
    for gi, w in enumerate(POOL_WINDOWS):
        c0 = gi * POOL_GROUP_DIM
        acc = ext_ref[hist:hist + steps, :, c0:c0 + POOL_GROUP_DIM]
        for d in range(1, w):
            acc = acc + ext_ref[hist - d:hist - d + steps, :, c0:c0 + POOL_GROUP_DIM]
        mean = acc / jnp.minimum(pos + 1, w).astype(F32)
        mean = jnp.swapaxes(mean, 0, 1).reshape(rows, POOL_GROUP_DIM)
        _pool_project(mean, 1.0, u_nat[:, c0:c0 + POOL_GROUP_DIM], gi, wpool_ref, pscale_ref, mix_ref)
    pooln_ref[...] = ext_ref[hist + steps - POOL_CACHE:hist + steps]

    lb = _lower_bound(lbp_ref[...], layer)
    for c in range(HEADS):
        cols = slice(c * HEAD_DIM, (c + 1) * HEAD_DIM)
        _head_gates(z_ref[:, Z_Q + c * HEAD_DIM:Z_Q + (c + 1) * HEAD_DIM],
                    z_ref[:, Z_F + c * HEAD_DIM:Z_F + (c + 1) * HEAD_DIM], lb[:, cols], c, rows, nat_ref, qk_ref)
    _segment_products(rows, levels, nat_ref, seg_ref)
    prefix = _token_levels(range(rows), rows, levels, nat_ref, seg_ref, lev_ref)
    decay = jnp.stack(prefix[steps - 1::steps])
    decay_t = jnp.swapaxes(decay, 1, 2)
    masks, eye = _level_masks(rows, levels)
    for j in range(HEADS):
        c0 = j * HEAD_DIM
        scores = _intra_scores(j, rows, levels, masks, eye, qk_ref, lev_ref, sc_ref)
        v32 = z_ref[:, Z_I + c0:Z_I + c0 + HEAD_DIM]
        v = v32.astype(BF16)
        v3 = v32.reshape(nseq, steps, HEAD_DIM).astype(BF16)
        qt3 = _head_rows(lev_ref, levels, j, rows).reshape(nseq, steps, HEAD_DIM).astype(BF16)
        kt3 = _head_rows(lev_ref, levels + 1, j, rows).reshape(nseq, steps, HEAD_DIM).astype(BF16)
        s_in = sin_ref[:, j]
        o_state = lax.dot_general(qt3, s_in.astype(BF16), (((2,), (1,)), ((0,), (0,))),
                                  preferred_element_type=F32)
        o = jnp.dot(scores.astype(BF16), v, preferred_element_type=F32) + o_state.reshape(rows, HEAD_DIM)
        upd = lax.dot_general(kt3, v3, (((1,), (1,)), ((0,), (0,))), preferred_element_type=F32)
        snew_ref[:, j] = decay_t[:, :, j:j + 1] * s_in + upd
        _head_output(o, j, z_ref[:, Z_G + c0:Z_G + c0 + HEAD_DIM], hgn_ref, mix_ref)


def _mixer_sample(z, cache, state, lbp, wpool, pscale, hgn, nseq_total, steps, pos0, layer, nseq):
    rows = nseq * steps
    levels = steps.bit_length() - 1
    assert 1 << levels == steps and nseq_total % nseq == 0
    hist = HIST_ROWS
    const2 = lambda i: (0, 0)
    return pl.pallas_call(
        functools.partial(_mixer_sample_body, layer=layer, nseq=nseq, steps=steps, pos0=pos0),
        grid=(nseq_total // nseq,),
        in_specs=[
            pl.BlockSpec((rows, IN_COLS), lambda i: (i, 0)),
            pl.BlockSpec((1, POOL_CACHE, nseq, POOL_WIDTH), lambda i: (layer, 0, i, 0)),
            pl.BlockSpec((nseq, HEADS, HEAD_DIM, HEAD_DIM), lambda i: (i, 0, 0, 0)),
            pl.BlockSpec(lbp.shape, const2),
            pl.BlockSpec(wpool.shape, lambda i: (0, 0, 0)),
            pl.BlockSpec((1, POOL_WIDTH), const2),
            pl.BlockSpec((1, HG_WIDTH), const2),
        ],
        out_specs=[
            pl.BlockSpec((rows, D_MODEL), lambda i: (i, 0)),
            pl.BlockSpec((POOL_CACHE, nseq, POOL_WIDTH), lambda i: (0, i, 0)),
            pl.BlockSpec((nseq, HEADS, HEAD_DIM, HEAD_DIM), lambda i: (i, 0, 0, 0)),
        ],
        out_shape=[
            jax.ShapeDtypeStruct((nseq_total * steps, D_MODEL), BF16),
            jax.ShapeDtypeStruct((POOL_CACHE, nseq_total, POOL_WIDTH), F32),
            jax.ShapeDtypeStruct((nseq_total, HEADS, HEAD_DIM, HEAD_DIM), F32),
        ],
        scratch_shapes=[
            pltpu.VMEM((hist + steps, nseq, POOL_WIDTH), F32),
            *_level_scratch(rows, levels),
        ],
        compiler_params=_cparams("mixer_sample", ("arbitrary",)),
        name="mixer_sample",
    )(z, cache, state, lbp, wpool, pscale, hgn)


def _out_proj_body(x_ref, mix_ref, w_ref, o_ref, wb_ref):
    w = w_ref[...].astype(BF16)
    wb_ref[...] = w
    o_ref[...] = x_ref[...] + jnp.dot(mix_ref[...], w, preferred_element_type=F32)


def _out_proj(x, mix, w, tm, tn):
    m = x.shape[0]
    return pl.pallas_call(
        _out_proj_body,
        grid=(m // tm, D_MODEL // tn),
        in_specs=[
            pl.BlockSpec((tm, tn), lambda i, j: (i, j)),
            pl.BlockSpec((tm, D_MODEL), lambda i, j: (i, 0)),
            pl.BlockSpec((D_MODEL, tn), lambda i, j: (0, j)),
        ],
        out_specs=[pl.BlockSpec((tm, tn), lambda i, j: (i, j)), pl.BlockSpec((D_MODEL, tn), lambda i, j: (0, j))],
        out_shape=[jax.ShapeDtypeStruct((m, D_MODEL), F32), jax.ShapeDtypeStruct(w.shape, BF16)],
        compiler_params=_cparams("out_proj", ("arbitrary", "arbitrary")),
        name="out_proj",
    )(x, mix, w)


def _ffn_body(x_ref, g2_ref, wg_ref, wu_ref, wd_ref, gf_ref, y_ref, *rest):
    *wb_refs, h_ref = rest
    j = pl.program_id(1)
    last = pl.num_programs(1) - 1
    chunks = _row_chunks(x_ref.shape[0])

    def down(h, wg, wu, wd):
        gate = jnp.dot(h, wg, preferred_element_type=F32)
        up = jnp.dot(h, wu, preferred_element_type=F32)
        act = (gate * _sigmoid(gate) * up).astype(BF16)
        return jnp.dot(act, wd, preferred_element_type=F32)

    def weights():
        if not wb_refs:
            return wg_ref[...], wu_ref[...], wd_ref[...]
        w = tuple(ref[...].astype(BF16) for ref in (wg_ref, wu_ref, wd_ref))
        for wb_ref, wb in zip(wb_refs, w):
            wb_ref[...] = wb
        return w

    @pl.when(j == 0)
    def _():
        w = weights()
        for rs in chunks:
            h = _rms(x_ref[rs, :], g2_ref[...]).astype(BF16)
            h_ref[rs, :] = h
            y_ref[rs, :] = down(h, *w)

    @pl.when((j > 0) & (j < last))
    def _():
        y_ref[...] += down(h_ref[...], *weights())

    @pl.when(j == last)
    def _():
        w = weights()
        for rs in chunks:
            y_ref[rs, :] = _rms(x_ref[rs, :] + y_ref[rs, :] + down(h_ref[rs, :], *w), gf_ref[...])


def _ffn(x, g2, wg, wu, wd, gf, tm, tf):
    m = x.shape[0]
    d_ff = wg.shape[1]
    emit = wg.dtype != BF16
    w_specs = [
        pl.BlockSpec((D_MODEL, tf), lambda i, j: (0, j)),
        pl.BlockSpec((D_MODEL, tf), lambda i, j: (0, j)),
        pl.BlockSpec((tf, D_MODEL), lambda i, j: (j, 0)),
    ]
    x_mode = dict(pipeline_mode=pl.Buffered(1)) if m == tm else {}
    return pl.pallas_call(
        _ffn_body,
        grid=(m // tm, d_ff // tf),
        in_specs=[
            pl.BlockSpec((tm, D_MODEL), lambda i, j: (i, 0), **x_mode),
            pl.BlockSpec((1, D_MODEL), lambda i, j: (0, 0)),
            *w_specs,
            pl.BlockSpec((1, D_MODEL), lambda i, j: (0, 0)),
        ],
        out_specs=[pl.BlockSpec((tm, D_MODEL), lambda i, j: (i, 0))] + (w_specs if emit else []),
        out_shape=[jax.ShapeDtypeStruct((m, D_MODEL), F32)]
        + ([jax.ShapeDtypeStruct(a.shape, BF16) for a in (wg, wu, wd)] if emit else []),
        scratch_shapes=[pltpu.VMEM((tm, D_MODEL), BF16)],
        compiler_params=_cparams("ffn", ("arbitrary", "arbitrary")),
        name="ffn",
    )(x, g2, wg, wu, wd, gf)


def kernel(x_prompt, x_sample, cache_pool, state_hgrn, lb_param, norm1, w_in, w_pool, pool_scale,
           hg_norm, w_o, norm2, w_gate, w_up, w_down, norm_f):
    depth = w_in.shape[0]
    assert depth == 1, "single-layer trunk"
    layer = 0
    batch, seq, _ = x_prompt.shape
    dec_batch, dec_seq, _ = x_sample.shape

    row = lambda a: a.reshape(1, -1).astype(F32)
    g1, g2, gf = row(norm1[layer]), row(norm2[layer]), row(norm_f)
    lbp = lb_param.astype(F32)
    wpool = w_pool[layer].astype(BF16)
    pscale = row(pool_scale[layer])
    hgn = row(hg_norm[layer])

    m_s = dec_batch * dec_seq
    xs = x_sample.reshape(m_s, D_MODEL)
    z_s, w_in_b = _in_proj(xs, g1, w_in[layer], tm=m_s, tn=SAMPLE_PROJ_COLS)
    mix_s, pool_s, s_s = _mixer_sample(z_s, jnp.swapaxes(cache_pool, 1, 2), state_hgrn[layer], lbp, wpool, pscale,
                                       hgn, dec_batch, dec_seq, PAST_LEN, layer, nseq=SAMPLE_SEQS)
    pool_s = jnp.swapaxes(pool_s, 0, 1)
    x1_s, w_o_b = _out_proj(xs, mix_s, w_o[layer], tm=m_s, tn=SAMPLE_PROJ_COLS)
    y_s, w_gate_b, w_up_b, w_down_b = _ffn(x1_s, g2, w_gate[layer], w_up[layer], w_down[layer], gf,
                                           tm=m_s, tf=SAMPLE_FFN_COLS)

    x1_p, pool_p, s_p = _prompt_layer(x_prompt.reshape(batch * seq, D_MODEL), g1, w_in_b, w_o_b,
                                      lbp, wpool, pscale, hgn, batch, seq, layer)
    (y_p,) = _ffn(x1_p, g2, w_gate_b, w_up_b, w_down_b, gf, tm=PROMPT_FFN_ROWS, tf=PROMPT_FFN_COLS)

    return (y_p.reshape(batch, seq, D_MODEL), y_s.reshape(dec_batch, dec_seq, D_MODEL),
            pool_p[None], s_p[None], pool_s[None], s_s[None])
```

```python
import functools

import jax
import jax.numpy as jnp
from jax import lax
from jax.experimental import pallas as pl
from jax.experimental.pallas import tpu as pltpu

D_MODEL = 2048
POOL_WIDTH = 1024
POOL_WINDOWS = (2, 4, 8, 16)
POOL_GROUP_DIM = POOL_WIDTH // len(POOL_WINDOWS)
POOL_CACHE = max(POOL_WINDOWS) - 1
HG_WIDTH = 1024
HEAD_DIM = 128
HEADS = HG_WIDTH // HEAD_DIM
IN_COLS = POOL_WIDTH + 4 * HG_WIDTH
Z_Q, Z_F, Z_I, Z_G = (POOL_WIDTH + n * HG_WIDTH for n in range(4))
EPS = 1e-6
MXU_TILE = 256
SUBLANES = 8
assert HEADS == SUBLANES
HIST_ROWS = 2 * SUBLANES
PAST_LEN = 16384

F32 = jnp.float32
BF16 = jnp.bfloat16

MIB = 1024 * 1024
SAMPLE_PROJ_COLS = 512
SAMPLE_FFN_COLS = 256
SAMPLE_SEQS = 8
PROMPT_FFN_ROWS = 1024
PROMPT_FFN_COLS = 512
VMEM_MIB = dict(in_proj=48, out_proj=48, mixer_sample=48, ffn=58, prompt_layer=60)


def _cparams(name, sem):
    return pltpu.CompilerParams(dimension_semantics=sem, vmem_limit_bytes=VMEM_MIB[name] * MIB)


def _rms(x, g):
    return x * lax.rsqrt(jnp.mean(x * x, axis=-1, keepdims=True) + EPS) * g


def _sigmoid(x):
    return 1.0 / (1.0 + jnp.exp(-x))


ROW_CHUNK = 256


def _row_chunks(n):
    return [slice(r, r + ROW_CHUNK) for r in range(0, n, ROW_CHUNK)]


def _in_proj_body(x_ref, g_ref, w_ref, z_ref, wb_ref, h_ref):
    j = pl.program_id(1)
    w = w_ref[...].astype(BF16)
    wb_ref[...] = w

    @pl.when(j == 0)
    def _():
        for rs in _row_chunks(x_ref.shape[0]):
            h = _rms(x_ref[rs, :], g_ref[...]).astype(BF16)
            h_ref[rs, :] = h
            z_ref[rs, :] = jnp.dot(h, w, preferred_element_type=F32)

    @pl.when(j > 0)
    def _():
        z_ref[...] = jnp.dot(h_ref[...], w, preferred_element_type=F32)


def _in_proj(x, g, w, tm, tn):
    m = x.shape[0]
    return pl.pallas_call(
        _in_proj_body,
        grid=(m // tm, IN_COLS // tn),
        in_specs=[
            pl.BlockSpec((tm, D_MODEL), lambda i, j: (i, 0)),
            pl.BlockSpec((1, D_MODEL), lambda i, j: (0, 0)),
            pl.BlockSpec((D_MODEL, tn), lambda i, j: (0, j)),
        ],
        out_specs=[pl.BlockSpec((tm, tn), lambda i, j: (i, j)), pl.BlockSpec((D_MODEL, tn), lambda i, j: (0, j))],
        out_shape=[jax.ShapeDtypeStruct((m, IN_COLS), F32), jax.ShapeDtypeStruct(w.shape, BF16)],
        scratch_shapes=[pltpu.VMEM((tm, D_MODEL), BF16)],
        compiler_params=_cparams("in_proj", ("arbitrary", "arbitrary")),
        name="in_proj",
    )(x, g, w)


def _lower_bound(lbp, layer):
    e = jnp.exp(lbp - jnp.max(lbp, axis=0, keepdims=True))
    return jnp.sum(e[: layer + 1], axis=0, keepdims=True) / jnp.sum(e, axis=0, keepdims=True)


def _pool_project(acc, cnt, u, gi, wpool_ref, pscale_ref, mix_ref):
    c0 = gi * POOL_GROUP_DIM
    pooled = acc / cnt - u
    out = jnp.dot(pooled.astype(BF16), wpool_ref[gi], preferred_element_type=F32)
    mix_ref[:, c0:c0 + POOL_GROUP_DIM] = (out * pscale_ref[:, c0:c0 + POOL_GROUP_DIM]).astype(BF16)


def _token_tile(nat_ref, a, t):
    return nat_ref[a, t // SUBLANES, pl.ds(t % SUBLANES, HEADS, stride=SUBLANES), :]


def _head_rows(lev_ref, lvl, j, rows):
    return lev_ref[lvl, pl.ds(j, rows, stride=HEADS), :]


def _head_gates(zq, zf, lb, c, rows, nat_ref, qk_ref):
    fg = lb + (1.0 - lb) * _sigmoid(zf)
    qs = zq * _sigmoid(zq)
    kk = 1.0 - fg
    qk_ref[0, :, c * HEAD_DIM:(c + 1) * HEAD_DIM] = qs.astype(BF16)
    qk_ref[1, :, c * HEAD_DIM:(c + 1) * HEAD_DIM] = kk.astype(BF16)
    for a, x in enumerate((qs, kk, fg)):
        nat_ref[a, :, c * SUBLANES:(c + 1) * SUBLANES, :] = x.reshape(rows // SUBLANES, SUBLANES, HEAD_DIM)


def _segment_offsets(rows, levels):
    off = [0]
    for lvl in range(levels):
        off.append(off[-1] + (rows >> lvl))
    return off


def _segment_products(rows, levels, nat_ref, seg_ref):
    off = _segment_offsets(rows, levels)
    for t in range(rows):
        seg_ref[t] = _token_tile(nat_ref, 2, t)
    for lvl in range(1, levels):
        for m in range(rows >> lvl):
            seg_ref[off[lvl] + m] = seg_ref[off[lvl - 1] + 2 * m] * seg_ref[off[lvl - 1] + 2 * m + 1]


def _token_levels(tokens, rows, levels, nat_ref, seg_ref, lev_ref):
    off = _segment_offsets(rows, levels)
    prefix = []
    for t in tokens:
        tok = slice(t * HEADS, (t + 1) * HEADS)
        q_t, k_t = _token_tile(nat_ref, 0, t), _token_tile(nat_ref, 1, t)
        p_t, r_t = seg_ref[t], None
        for lvl in range(levels):
            m = t >> lvl
            sibling = seg_ref[off[lvl] + (m ^ 1)]
            if m & 1:
                lev_ref[lvl, tok, :] = q_t * p_t
                p_t = p_t * sibling
            else:
                lev_ref[lvl, tok, :] = k_t if r_t is None else k_t * r_t
                r_t = sibling if r_t is None else r_t * sibling
        lev_ref[levels, tok, :] = q_t * p_t
        lev_ref[levels + 1, tok, :] = k_t if r_t is None else k_t * r_t
        prefix.append(p_t)
    return prefix


def _level_scratch(rows, levels):
    return [
        pltpu.VMEM((3, rows // SUBLANES, HEADS * SUBLANES, HEAD_DIM), F32),
        pltpu.VMEM((2 * rows, HEADS, HEAD_DIM), F32),
        pltpu.VMEM((2, rows, HG_WIDTH), BF16),
        pltpu.VMEM((levels + 2, rows * HEADS, HEAD_DIM), F32),
        pltpu.VMEM((HEADS, rows, rows), F32),
    ]


def _level_masks(rows, levels):
    ti = lax.broadcasted_iota(jnp.int32, (rows, rows), 0)
    si = lax.broadcasted_iota(jnp.int32, (rows, rows), 1)
    x = ti ^ si
    masks = []
    for lvl in range(levels):
        h = 1 << lvl
        if h < SUBLANES:
            masks.append(((x >> lvl) == 1) & (((ti >> lvl) & 1) == 1))
        else:
            half = lax.broadcasted_iota(jnp.int32, (h, rows), 1) >> lvl
            masks.append([half == 2 * b for b in range(rows // (2 * h))])
    return masks, ti == si


def _nt(a, b):
    return lax.dot_general(a, b, (((1,), (1,)), ((), ())), preferred_element_type=F32)


def _score_level(j, lvl, rows, masks, eye, qk_ref, lev_ref, sc_ref):
    c0 = j * HEAD_DIM
    sc_ref = sc_ref.at[j]
    if lvl < 0:
        pltpu.store(sc_ref, _nt(qk_ref[0, :, c0:c0 + HEAD_DIM], qk_ref[1, :, c0:c0 + HEAD_DIM]), mask=eye)
        return
    h = 1 << lvl
    x32 = _head_rows(lev_ref, lvl, j, rows)
    x = x32.astype(BF16)
    if h < SUBLANES:
        pltpu.store(sc_ref, _nt(x, x), mask=masks[lvl])
        return
    blocks = rows // (2 * h)
    upper = x32.reshape(blocks, 2 * h, HEAD_DIM)[:, h:].reshape(rows // 2, HEAD_DIM).astype(BF16)
    s = _nt(upper, x)
    for b in range(blocks):
        pltpu.store(sc_ref.at[b * 2 * h + h:(b + 1) * 2 * h, :], s[b * h:(b + 1) * h], mask=masks[lvl][b])


def _intra_scores(j, rows, levels, masks, eye, qk_ref, lev_ref, sc_ref):
    for lvl in range(-1, levels):
        _score_level(j, lvl, rows, masks, eye, qk_ref, lev_ref, sc_ref)
    return sc_ref[j]


def _window_sums(e, w, axis, hist, rows):
    s, d = e, 1
    while d < min(w, SUBLANES):
        s = s + pltpu.roll(s, d, axis)
        d *= 2
    take = lambda a, lo: lax.slice_in_dim(a, lo, lo + rows, axis=axis)
    out = take(s, hist)
    if w > SUBLANES:
        assert w == 2 * SUBLANES
        out = out + take(s, hist - SUBLANES)
    return out


def _head_output(o, j, zg, hgn_ref, mix_ref):
    c0 = j * HEAD_DIM
    o = _rms(o, hgn_ref[:, c0:c0 + HEAD_DIM]) * (zg * _sigmoid(zg))
    mix_ref[:, POOL_WIDTH + c0:POOL_WIDTH + c0 + HEAD_DIM] = o.astype(BF16)


PROMPT_ROWS = 128
PROMPT_LEVELS = 7


PROMPT_SLOTS = 2


def _prompt_prepare(z_blk, t_blk, starts_seq, slot, lb, wpool_ref, pscale_ref,
                    ext_ref, pool_ref, decay_ref, nat_ref, seg_ref, qk_ref, lev_ref):
    rows, levels, hist = PROMPT_ROWS, PROMPT_LEVELS, HIST_ROWS
    nat, seg, qk, lev = nat_ref[slot], seg_ref[slot], qk_ref[slot], lev_ref[slot]

    def pool_group(gi):
        def run():
            if gi == 0:
                ext_ref[0:hist, :] = jnp.where(starts_seq, 0.0, ext_ref[rows:rows + hist, :])
                ext_ref[hist:hist + rows, :] = z_blk[:, 0:POOL_WIDTH]
            w, c0 = POOL_WINDOWS[gi], gi * POOL_GROUP_DIM
            e = ext_ref[:, c0:c0 + POOL_GROUP_DIM]
            pos = t_blk * rows + lax.broadcasted_iota(jnp.int32, (rows, 1), 0)
            cnt = jnp.minimum(pos + 1, w).astype(F32)
            _pool_project(_window_sums(e, w, 0, hist, rows), cnt, e[hist:hist + rows], gi,
                          wpool_ref, pscale_ref, pool_ref[slot])
        return run

    def gates(c):
        cols = slice(c * HEAD_DIM, (c + 1) * HEAD_DIM)
        return lambda: _head_gates(z_blk[:, Z_Q + c * HEAD_DIM:Z_Q + (c + 1) * HEAD_DIM],
                                   z_blk[:, Z_F + c * HEAD_DIM:Z_F + (c + 1) * HEAD_DIM], lb[:, cols], c, rows, nat, qk)

    def tokens(k, n):
        def run():
            prefix = _token_levels(range(k * n, (k + 1) * n), rows, levels, nat, seg, lev)
            if (k + 1) * n == rows:
                decay_ref[slot][...] = prefix[-1]
        return run

    a, b = [pool_group(g) for g in range(len(POOL_WINDOWS))], [gates(c) for c in range(HEADS)]
    t = [tokens(k, rows // HEADS) for k in range(HEADS)]
    segments = lambda: _segment_products(rows, levels, nat, seg)
    return [[a[0], a[1], b[0]], [a[2], a[3], b[1]], b[2:5], b[5:8], [segments, t[0]], t[1:3], t[3:6], t[6:8]]


def _prompt_heads(slot, z_blk, mix_blk, hgn_ref, s_ref, pool_ref, decay_ref, qk_ref, lev_ref, sc_ref,
                  masks, eye):
    rows, levels = PROMPT_ROWS, PROMPT_LEVELS
    qk, lev, sc = qk_ref[slot], lev_ref[slot], sc_ref[slot]
    value = lambda j: z_blk[:, Z_I + j * HEAD_DIM:Z_I + (j + 1) * HEAD_DIM].astype(BF16)

    def pool_part():
        mix_blk[:, 0:POOL_WIDTH] = pool_ref[slot][...]

    def output(j):
        c0 = j * HEAD_DIM
        qt = _head_rows(lev, levels, j, rows).astype(BF16)
        o = (jnp.dot(sc[j].astype(BF16), value(j), preferred_element_type=F32)
             + jnp.dot(qt, s_ref[j].astype(BF16), preferred_element_type=F32))
        _head_output(o, j, z_blk[:, Z_G + c0:Z_G + c0 + HEAD_DIM], hgn_ref, mix_blk)

    def state(j):
        decay_t = jnp.transpose(decay_ref[slot][...])
        kt = _head_rows(lev, levels + 1, j, rows).astype(BF16)
        s_ref[j] = decay_t[:, j:j + 1] * s_ref[j] + lax.dot_general(
            kt, value(j), (((0,), (0,)), ((), ())), preferred_element_type=F32)

    heads = range(HEADS)
    scores = [functools.partial(_score_level, j, lvl, rows, masks, eye, qk, lev, sc)
              for lvl in range(-1, levels) for j in heads]
    return ([pool_part] + scores + [functools.partial(output, j) for j in heads]
            + [functools.partial(state, j) for j in heads])


def _alternate(a, b):
    out, done = [], 0
    for k, item in enumerate(a):
        out.append(item)
        upto = (k + 1) * len(b) // len(a)
        out.extend(b[done:upto])
        done = upto
    return out


def _prompt_layer_body(xc_ref, xn_ref, g1_ref, win_ref, wo_ref, lbp_ref, wpool_ref, pscale_ref, hgn_ref,
                       x1_ref, pooln_ref, snew_ref,
                       z_ref, mix_ref, ext_ref, s_ref, pool_ref, decay_ref, nat_ref, seg_ref, qk_ref,
                       lev_ref, sc_ref, *, layer, nt):
    rows, hist = PROMPT_ROWS, HIST_ROWS
    i = pl.program_id(0)
    steps_per_seq = nt // PROMPT_SLOTS
    tt = i % steps_per_seq
    lb = _lower_bound(lbp_ref[...], layer)
    one = lambda ref: [ref]
    prepare = functools.partial(
        _prompt_prepare, lb=lb, wpool_ref=wpool_ref, pscale_ref=pscale_ref, ext_ref=ext_ref,
        pool_ref=one(pool_ref), decay_ref=one(decay_ref), nat_ref=one(nat_ref), seg_ref=one(seg_ref),
        qk_ref=one(qk_ref), lev_ref=one(lev_ref))
    masks, eye = _level_masks(rows, PROMPT_LEVELS)
    heads = functools.partial(
        _prompt_heads, hgn_ref=hgn_ref, s_ref=s_ref, pool_ref=one(pool_ref), decay_ref=one(decay_ref),
        qk_ref=one(qk_ref), lev_ref=one(lev_ref), sc_ref=one(sc_ref), masks=masks, eye=eye)

    def in_proj(x_ref):
        h = _rms(x_ref[...], g1_ref[...]).astype(BF16)
        z_ref[...] = jnp.dot(h, win_ref[...], preferred_element_type=F32)

    def mix_block(blk, t_blk, starts_seq):
        z_blk, mix_blk = z_ref.at[blk * rows:(blk + 1) * rows], mix_ref.at[blk * rows:(blk + 1) * rows]
        return [item for batch in prepare(z_blk, t_blk, starts_seq, 0) for item in batch] + heads(0, z_blk, mix_blk)

    def out_proj(blk):
        rs = slice(blk * rows, (blk + 1) * rows)

        def tile(c0):
            def run():
                cols = slice(c0, c0 + MXU_TILE)
                x1_ref[rs, cols] = xc_ref[rs, cols] + jnp.dot(mix_ref[rs, :], wo_ref[:, cols],
                                                              preferred_element_type=F32)
            return run
        return [tile(c0) for c0 in range(0, D_MODEL, MXU_TILE)]

    def run_all(items):
        for item in items:
            item()

    @pl.when(i == 0)
    def _():
        sc_ref[...] = jnp.zeros_like(sc_ref)
        ext_ref[...] = jnp.zeros_like(ext_ref)
        in_proj(xc_ref)

    @pl.when(tt == 0)
    def _():
        s_ref[...] = jnp.zeros_like(s_ref)

    region = pl.when(i < pl.num_programs(0))
    region(functools.partial(run_all, mix_block(0, PROMPT_SLOTS * tt, tt == 0)))
    region(functools.partial(run_all, _alternate(mix_block(1, PROMPT_SLOTS * tt + 1, False), out_proj(0))))
    region(functools.partial(run_all, out_proj(1)))
    region(functools.partial(in_proj, xn_ref))
    pooln_ref[0] = ext_ref[hist + rows - POOL_CACHE:hist + rows, :]

    @pl.when(tt == steps_per_seq - 1)
    def _():
        snew_ref[0] = s_ref[...]


def _prompt_layer(x, g1, w_in, w_o, lbp, wpool, pscale, hgn, batch, seq, layer):
    rows, slots = PROMPT_ROWS, PROMPT_SLOTS
    nt = seq // rows
    assert seq % (rows * slots) == 0
    nblk = batch * nt
    steps_per_seq = nt // slots
    const2 = lambda i: (0, 0)
    resident = lambda shape: pl.BlockSpec(shape, const2, pipeline_mode=pl.Buffered(1))
    return pl.pallas_call(
        functools.partial(_prompt_layer_body, layer=layer, nt=nt),
        grid=(nblk // slots,),
        in_specs=[
            pl.BlockSpec((slots * rows, D_MODEL), lambda i: (i, 0)),
            pl.BlockSpec((slots * rows, D_MODEL), lambda i: (jnp.minimum(i + 1, nblk // slots - 1), 0)),
            pl.BlockSpec((1, D_MODEL), const2),
            resident(w_in.shape),
            resident(w_o.shape),
            pl.BlockSpec(lbp.shape, const2),
            pl.BlockSpec(wpool.shape, lambda i: (0, 0, 0)),
            pl.BlockSpec((1, POOL_WIDTH), const2),
            pl.BlockSpec((1, HG_WIDTH), const2),
        ],
        out_specs=[
            pl.BlockSpec((slots * rows, D_MODEL), lambda i: (i, 0)),
            pl.BlockSpec((1, POOL_CACHE, POOL_WIDTH), lambda i: (i // steps_per_seq, 0, 0)),
            pl.BlockSpec((1, HEADS, HEAD_DIM, HEAD_DIM), lambda i: (i // steps_per_seq, 0, 0, 0)),
        ],
        out_shape=[
            jax.ShapeDtypeStruct((batch * seq, D_MODEL), F32),
            jax.ShapeDtypeStruct((batch, POOL_CACHE, POOL_WIDTH), F32),
            jax.ShapeDtypeStruct((batch, HEADS, HEAD_DIM, HEAD_DIM), F32),
        ],
        scratch_shapes=[
            pltpu.VMEM((slots * rows, IN_COLS), F32),
            pltpu.VMEM((slots * rows, D_MODEL), BF16),
            pltpu.VMEM((HIST_ROWS + rows, POOL_WIDTH), F32),
            pltpu.VMEM((HEADS, HEAD_DIM, HEAD_DIM), F32),
            pltpu.VMEM((rows, POOL_WIDTH), BF16),
            pltpu.VMEM((HEADS, HEAD_DIM), F32),
            *_level_scratch(rows, PROMPT_LEVELS),
        ],
        compiler_params=_cparams("prompt_layer", ("arbitrary",)),
        name="prompt_layer",
    )(x, x, g1, w_in, w_o, lbp, wpool, pscale, hgn)


def _mixer_sample_body(z_ref, cache_ref, sin_ref, lbp_ref, wpool_ref, pscale_ref, hgn_ref,
                       mix_ref, pooln_ref, snew_ref,
                       ext_ref, nat_ref, seg_ref, qk_ref, lev_ref, sc_ref, *, layer, nseq, steps, pos0):
    rows = nseq * steps
    levels = steps.bit_length() - 1
    hist = HIST_ROWS

    @pl.when(pl.program_id(0) == 0)
    def _():
        sc_ref[...] = jnp.zeros_like(sc_ref)

    u_nat = z_ref[:, 0:POOL_WIDTH]
    ext_ref[0:hist - POOL_CACHE] = jnp.zeros((hist - POOL_CACHE, nseq, POOL_WIDTH), F32)
    ext_ref[hist - POOL_CACHE:hist] = cache_ref[0]
    ext_ref[hist:hist + steps] = jnp.swapaxes(u_nat.reshape(nseq, steps, POOL_WIDTH), 0, 1)
    pos = pos0 + lax.broadcasted_iota(jnp.int32, (steps, 1, 1), 0)
    for gi, w in enumerate(POOL_WINDOWS):
        c0 = gi * POOL_GROUP_DIM
        acc = ext_ref[hist:hist + steps, :, c0:c0 + POOL_GROUP_DIM]
        for d in range(1, w):
            acc = acc + ext_ref[hist - d:hist - d + steps, :, c0:c0 + POOL_GROUP_DIM]
        mean = acc / jnp.minimum(pos + 1, w).astype(F32)
        mean = jnp.swapaxes(mean, 0, 1).reshape(rows, POOL_GROUP_DIM)
        _pool_project(mean, 1.0, u_nat[:, c0:c0 + POOL_GROUP_DIM], gi, wpool_ref, pscale_ref, mix_ref)
    pooln_ref[...] = ext_ref[hist + steps - POOL_CACHE:hist + steps]

    lb = _lower_bound(lbp_ref[...], layer)
    for c in range(HEADS):
        cols = slice(c * HEAD_DIM, (c + 1) * HEAD_DIM)
        _head_gates(z_ref[:, Z_Q + c * HEAD_DIM:Z_Q + (c + 1) * HEAD_DIM],
                    z_ref[:, Z_F + c * HEAD_DIM:Z_F + (c + 1) * HEAD_DIM], lb[:, cols], c, rows, nat_ref, qk_ref)
    _segment_products(rows, levels, nat_ref, seg_ref)
    prefix = _token_levels(range(rows), rows, levels, nat_ref, seg_ref, lev_ref)
    decay = jnp.stack(prefix[steps - 1::steps])
    decay_t = jnp.swapaxes(decay, 1, 2)
    masks, eye = _level_masks(rows, levels)
    for j in range(HEADS):
        c0 = j * HEAD_DIM
        scores = _intra_scores(j, rows, levels, masks, eye, qk_ref, lev_ref, sc_ref)
        v32 = z_ref[:, Z_I + c0:Z_I + c0 + HEAD_DIM]
        v = v32.astype(BF16)
        v3 = v32.reshape(nseq, steps, HEAD_DIM).astype(BF16)
        qt3 = _head_rows(lev_ref, levels, j, rows).reshape(nseq, steps, HEAD_DIM).astype(BF16)
        kt3 = _head_rows(lev_ref, levels + 1, j, rows).reshape(nseq, steps, HEAD_DIM).astype(BF16)
        s_in = sin_ref[:, j]
        o_state = lax.dot_general(qt3, s_in.astype(BF16), (((2,), (1,)), ((0,), (0,))),
                                  preferred_element_type=F32)
        o = jnp.dot(scores.astype(BF16), v, preferred_element_type=F32) + o_state.reshape(rows, HEAD_DIM)
        upd = lax.dot_general(kt3, v3, (((1,), (1,)), ((0,), (0,))), preferred_element_type=F32)
        snew_ref[:, j] = decay_t[:, :, j:j + 1] * s_in + upd
        _head_output(o, j, z_ref[:, Z_G + c0:Z_G + c0 + HEAD_DIM], hgn_ref, mix_ref)


def _mixer_sample(z, cache, state, lbp, wpool, pscale, hgn, nseq_total, steps, pos0, layer, nseq):
    rows = nseq * steps
    levels = steps.bit_length() - 1
    assert 1 << levels == steps and nseq_total % nseq == 0
    hist = HIST_ROWS
    const2 = lambda i: (0, 0)
    return pl.pallas_call(
        functools.partial(_mixer_sample_body, layer=layer, nseq=nseq, steps=steps, pos0=pos0),
        grid=(nseq_total // nseq,),
        in_specs=[
            pl.BlockSpec((rows, IN_COLS), lambda i: (i, 0)),
            pl.BlockSpec((1, POOL_CACHE, nseq, POOL_WIDTH), lambda i: (layer, 0, i, 0)),
            pl.BlockSpec((nseq, HEADS, HEAD_DIM, HEAD_DIM), lambda i: (i, 0, 0, 0)),
            pl.BlockSpec(lbp.shape, const2),
            pl.BlockSpec(wpool.shape, lambda i: (0, 0, 0)),
            pl.BlockSpec((1, POOL_WIDTH), const2),
            pl.BlockSpec((1, HG_WIDTH), const2),
        ],
        out_specs=[
            pl.BlockSpec((rows, D_MODEL), lambda i: (i, 0)),
            pl.BlockSpec((POOL_CACHE, nseq, POOL_WIDTH), lambda i: (0, i, 0)),
            pl.BlockSpec((nseq, HEADS, HEAD_DIM, HEAD_DIM), lambda i: (i, 0, 0, 0)),
        ],
        out_shape=[
            jax.ShapeDtypeStruct((nseq_total * steps, D_MODEL), BF16),
            jax.ShapeDtypeStruct((POOL_CACHE, nseq_total, POOL_WIDTH), F32),
            jax.ShapeDtypeStruct((nseq_total, HEADS, HEAD_DIM, HEAD_DIM), F32),
        ],
        scratch_shapes=[
            pltpu.VMEM((hist + steps, nseq, POOL_WIDTH), F32),
            *_level_scratch(rows, levels),
        ],
        compiler_params=_cparams("mixer_sample", ("arbitrary",)),
        name="mixer_sample",
    )(z, cache, state, lbp, wpool, pscale, hgn)


def _out_proj_body(x_ref, mix_ref, w_ref, o_ref, wb_ref):
    w = w_ref[...].astype(BF16)
    wb_ref[...] = w
    o_ref[...] = x_ref[...] + jnp.dot(mix_ref[...], w, preferred_element_type=F32)


def _out_proj(x, mix, w, tm, tn):
    m = x.shape[0]
    return pl.pallas_call(
        _out_proj_body,
        grid=(m // tm, D_MODEL // tn),
        in_specs=[
            pl.BlockSpec((tm, tn), lambda i, j: (i, j)),
            pl.BlockSpec((tm, D_MODEL), lambda i, j: (i, 0)),
            pl.BlockSpec((D_MODEL, tn), lambda i, j: (0, j)),
        ],
        out_specs=[pl.BlockSpec((tm, tn), lambda i, j: (i, j)), pl.BlockSpec((D_MODEL, tn), lambda i, j: (0, j))],
        out_shape=[jax.ShapeDtypeStruct((m, D_MODEL), F32), jax.ShapeDtypeStruct(w.shape, BF16)],
        compiler_params=_cparams("out_proj", ("arbitrary", "arbitrary")),
        name="out_proj",
    )(x, mix, w)


def _ffn_body(x_ref, g2_ref, wg_ref, wu_ref, wd_ref, gf_ref, y_ref, *rest):
    *wb_refs, h_ref = rest
    j = pl.program_id(1)
    last = pl.num_programs(1) - 1
    chunks = _row_chunks(x_ref.shape[0])

    def down(h, wg, wu, wd):
        gate = jnp.dot(h, wg, preferred_element_type=F32)
        up = jnp.dot(h, wu, preferred_element_type=F32)
        act = (gate * _sigmoid(gate) * up).astype(BF16)
        return jnp.dot(act, wd, preferred_element_type=F32)

    def weights():
        if not wb_refs:
            return wg_ref[...], wu_ref[...], wd_ref[...]
        w = tuple(ref[...].astype(BF16) for ref in (wg_ref, wu_ref, wd_ref))
        for wb_ref, wb in zip(wb_refs, w):
            wb_ref[...] = wb
        return w

    @pl.when(j == 0)
    def _():
        w = weights()
        for rs in chunks:
            h = _rms(x_ref[rs, :], g2_ref[...]).astype(BF16)
            h_ref[rs, :] = h
            y_ref[rs, :] = down(h, *w)

    @pl.when((j > 0) & (j < last))
    def _():
        y_ref[...] += down(h_ref[...], *weights())

    @pl.when(j == last)
    def _():
        w = weights()
        for rs in chunks:
            y_ref[rs, :] = _rms(x_ref[rs, :] + y_ref[rs, :] + down(h_ref[rs, :], *w), gf_ref[...])


def _ffn(x, g2, wg, wu, wd, gf, tm, tf):
    m = x.shape[0]
    d_ff = wg.shape[1]
    emit = wg.dtype != BF16
    w_specs = [
        pl.BlockSpec((D_MODEL, tf), lambda i, j: (0, j)),
        pl.BlockSpec((D_MODEL, tf), lambda i, j: (0, j)),
        pl.BlockSpec((tf, D_MODEL), lambda i, j: (j, 0)),
    ]
    x_mode = dict(pipeline_mode=pl.Buffered(1)) if m == tm else {}
    return pl.pallas_call(
        _ffn_body,
        grid=(m // tm, d_ff // tf),
        in_specs=[
            pl.BlockSpec((tm, D_MODEL), lambda i, j: (i, 0), **x_mode),
            pl.BlockSpec((1, D_MODEL), lambda i, j: (0, 0)),
            *w_specs,
            pl.BlockSpec((1, D_MODEL), lambda i, j: (0, 0)),
        ],
        out_specs=[pl.BlockSpec((tm, D_MODEL), lambda i, j: (i, 0))] + (w_specs if emit else []),
        out_shape=[jax.ShapeDtypeStruct((m, D_MODEL), F32)]
        + ([jax.ShapeDtypeStruct(a.shape, BF16) for a in (wg, wu, wd)] if emit else []),
        scratch_shapes=[pltpu.VMEM((tm, D_MODEL), BF16)],
        compiler_params=_cparams("ffn", ("arbitrary", "arbitrary")),
        name="ffn",
    )(x, g2, wg, wu, wd, gf)


def kernel(x_prompt, x_sample, cache_pool, state_hgrn, lb_param, norm1, w_in, w_pool, pool_scale,
           hg_norm, w_o, norm2, w_gate, w_up, w_down, norm_f):
    depth = w_in.shape[0]
    assert depth == 1, "single-layer trunk"
    layer = 0
    batch, seq, _ = x_prompt.shape
    dec_batch, dec_seq, _ = x_sample.shape

    row = lambda a: a.reshape(1, -1).astype(F32)
    g1, g2, gf = row(norm1[layer]), row(norm2[layer]), row(norm_f)
    lbp = lb_param.astype(F32)
    wpool = w_pool[layer].astype(BF16)
    pscale = row(pool_scale[layer])
    hgn = row(hg_norm[layer])

    m_s = dec_batch * dec_seq
    xs = x_sample.reshape(m_s, D_MODEL)
    z_s, w_in_b = _in_proj(xs, g1, w_in[layer], tm=m_s, tn=SAMPLE_PROJ_COLS)
    mix_s, pool_s, s_s = _mixer_sample(z_s, jnp.swapaxes(cache_pool, 1, 2), state_hgrn[layer], lbp, wpool, pscale,
                                       hgn, dec_batch, dec_seq, PAST_LEN, layer, nseq=SAMPLE_SEQS)
    pool_s = jnp.swapaxes(pool_s, 0, 1)
    x1_s, w_o_b = _out_proj(xs, mix_s, w_o[layer], tm=m_s, tn=SAMPLE_PROJ_COLS)
    y_s, w_gate_b, w_up_b, w_down_b = _ffn(x1_s, g2, w_gate[layer], w_up[layer], w_down[layer], gf,
                                           tm=m_s, tf=SAMPLE_FFN_COLS)

    x1_p, pool_p, s_p = _prompt_layer(x_prompt.reshape(batch * seq, D_MODEL), g1, w_in_b, w_o_b,
                                      lbp, wpool, pscale, hgn, batch, seq, layer)
    (y_p,) = _ffn(x1_p, g2, w_gate_b, w_up_b, w_down_b, gf, tm=PROMPT_FFN_ROWS, tf=PROMPT_FFN_COLS)

    return (y_p.reshape(batch, seq, D_MODEL), y_s.reshape(dec_batch, dec_seq, D_MODEL),
            pool_p[None], s_p[None], pool_s[None], s_s[None])
```

```python
import functools

import jax
import jax.numpy as jnp
from jax import lax
from jax.experimental import pallas as pl
from jax.experimental.pallas import tpu as pltpu

D_MODEL = 2048
POOL_WIDTH = 1024
POOL_WINDOWS = (2, 4, 8, 16)
POOL_GROUP_DIM = POOL_WIDTH // len(POOL_WINDOWS)
POOL_CACHE = max(POOL_WINDOWS) - 1
HG_WIDTH = 1024
HEAD_DIM = 128
HEADS = HG_WIDTH // HEAD_DIM
IN_COLS = POOL_WIDTH + 4 * HG_WIDTH
Z_Q, Z_F, Z_I, Z_G = (POOL_WIDTH + n * HG_WIDTH for n in range(4))
EPS = 1e-6
SUBLANES = 8
assert HEADS == SUBLANES
HIST_ROWS = 2 * SUBLANES
PAST_LEN = 16384

F32 = jnp.float32
BF16 = jnp.bfloat16

MIB = 1024 * 1024
SAMPLE_PROJ_COLS = 512
SAMPLE_FFN_COLS = 256
SAMPLE_SEQS = 8
PROMPT_FFN_ROWS = 1024
PROMPT_FFN_COLS = 512
VMEM_MIB = dict(in_proj=48, out_proj=48, mixer_sample=48, ffn=58, prompt_layer=62)


def _cparams(name, sem):
    return pltpu.CompilerParams(dimension_semantics=sem, vmem_limit_bytes=VMEM_MIB[name] * MIB)


def _rms(x, g):
    return x * lax.rsqrt(jnp.mean(x * x, axis=-1, keepdims=True) + EPS) * g


def _sigmoid(x):
    return 1.0 / (1.0 + jnp.exp(-x))


ROW_CHUNK = 256


def _row_chunks(n):
    return [slice(r, r + ROW_CHUNK) for r in range(0, n, ROW_CHUNK)]


def _in_proj_body(x_ref, g_ref, w_ref, z_ref, wb_ref, h_ref):
    j = pl.program_id(1)
    w = w_ref[...].astype(BF16)
    wb_ref[...] = w

    @pl.when(j == 0)
    def _():
        for rs in _row_chunks(x_ref.shape[0]):
            h = _rms(x_ref[rs, :], g_ref[...]).astype(BF16)
            h_ref[rs, :] = h
            z_ref[rs, :] = jnp.dot(h, w, preferred_element_type=F32)

    @pl.when(j > 0)
    def _():
        z_ref[...] = jnp.dot(h_ref[...], w, preferred_element_type=F32)


def _in_proj(x, g, w, tm, tn):
    m = x.shape[0]
    return pl.pallas_call(
        _in_proj_body,
        grid=(m // tm, IN_COLS // tn),
        in_specs=[
            pl.BlockSpec((tm, D_MODEL), lambda i, j: (i, 0)),
            pl.BlockSpec((1, D_MODEL), lambda i, j: (0, 0)),
            pl.BlockSpec((D_MODEL, tn), lambda i, j: (0, j)),
        ],
        out_specs=[pl.BlockSpec((tm, tn), lambda i, j: (i, j)), pl.BlockSpec((D_MODEL, tn), lambda i, j: (0, j))],
        out_shape=[jax.ShapeDtypeStruct((m, IN_COLS), F32), jax.ShapeDtypeStruct(w.shape, BF16)],
        scratch_shapes=[pltpu.VMEM((tm, D_MODEL), BF16)],
        compiler_params=_cparams("in_proj", ("arbitrary", "arbitrary")),
        name="in_proj",
    )(x, g, w)


def _lower_bound(lbp, layer):
    e = jnp.exp(lbp - jnp.max(lbp, axis=0, keepdims=True))
    return jnp.sum(e[: layer + 1], axis=0, keepdims=True) / jnp.sum(e, axis=0, keepdims=True)


def _pool_project(acc, cnt, u, gi, wpool_ref, pscale_ref, mix_ref):
    c0 = gi * POOL_GROUP_DIM
    pooled = acc / cnt - u
    out = jnp.dot(pooled.astype(BF16), wpool_ref[gi], preferred_element_type=F32)
    mix_ref[:, c0:c0 + POOL_GROUP_DIM] = (out * pscale_ref[:, c0:c0 + POOL_GROUP_DIM]).astype(BF16)


def _token_tile(nat_ref, a, t):
    return nat_ref[a, t // SUBLANES, pl.ds(t % SUBLANES, HEADS, stride=SUBLANES), :]


def _head_rows(lev_ref, lvl, j, rows):
    return lev_ref[lvl, pl.ds(j, rows, stride=HEADS), :]


def _head_gates(zq, zf, lb, c, rows, nat_ref, qk_ref):
    fg = lb + (1.0 - lb) * _sigmoid(zf)
    qs = zq * _sigmoid(zq)
    kk = 1.0 - fg
    qk_ref[0, :, c * HEAD_DIM:(c + 1) * HEAD_DIM] = qs.astype(BF16)
    qk_ref[1, :, c * HEAD_DIM:(c + 1) * HEAD_DIM] = kk.astype(BF16)
    for a, x in enumerate((qs, kk, fg)):
        nat_ref[a, :, c * SUBLANES:(c + 1) * SUBLANES, :] = x.reshape(rows // SUBLANES, SUBLANES, HEAD_DIM)


def _segment_offsets(rows, levels):
    off = [0]
    for lvl in range(levels):
        off.append(off[-1] + (rows >> lvl))
    return off


def _segment_products(rows, levels, nat_ref, seg_ref):
    off = _segment_offsets(rows, levels)
    for t in range(rows):
        seg_ref[t] = _token_tile(nat_ref, 2, t)
    for lvl in range(1, levels):
        for m in range(rows >> lvl):
            seg_ref[off[lvl] + m] = seg_ref[off[lvl - 1] + 2 * m] * seg_ref[off[lvl - 1] + 2 * m + 1]


def _token_levels(tokens, rows, levels, nat_ref, seg_ref, lev_ref):
    off = _segment_offsets(rows, levels)
    prefix = []
    for t in tokens:
        tok = slice(t * HEADS, (t + 1) * HEADS)
        q_t, k_t = _token_tile(nat_ref, 0, t), _token_tile(nat_ref, 1, t)
        p_t, r_t = seg_ref[t], None
        for lvl in range(levels):
            m = t >> lvl
            sibling = seg_ref[off[lvl] + (m ^ 1)]
            if m & 1:
                lev_ref[lvl, tok, :] = q_t * p_t
                p_t = p_t * sibling
            else:
                lev_ref[lvl, tok, :] = k_t if r_t is None else k_t * r_t
                r_t = sibling if r_t is None else r_t * sibling
        lev_ref[levels, tok, :] = q_t * p_t
        lev_ref[levels + 1, tok, :] = k_t if r_t is None else k_t * r_t
        prefix.append(p_t)
    return prefix


def _level_scratch(rows, levels):
    return [
        pltpu.VMEM((3, rows // SUBLANES, HEADS * SUBLANES, HEAD_DIM), F32),
        pltpu.VMEM((2 * rows, HEADS, HEAD_DIM), F32),
        pltpu.VMEM((2, rows, HG_WIDTH), BF16),
        pltpu.VMEM((levels + 2, rows * HEADS, HEAD_DIM), F32),
        pltpu.VMEM((HEADS, rows, rows), F32),
    ]


def _level_masks(rows, levels):
    ti = lax.broadcasted_iota(jnp.int32, (rows, rows), 0)
    si = lax.broadcasted_iota(jnp.int32, (rows, rows), 1)
    x = ti ^ si
    masks = []
    for lvl in range(levels):
        h = 1 << lvl
        if h < SUBLANES:
            masks.append(((x >> lvl) == 1) & (((ti >> lvl) & 1) == 1))
        else:
            half = lax.broadcasted_iota(jnp.int32, (h, rows), 1) >> lvl
            masks.append([half == 2 * b for b in range(rows // (2 * h))])
    return masks, ti == si


def _nt(a, b):
    return lax.dot_general(a, b, (((1,), (1,)), ((), ())), preferred_element_type=F32)


def _score_level(j, lvl, rows, masks, eye, qk_ref, lev_ref, sc_ref):
    c0 = j * HEAD_DIM
    sc_ref = sc_ref.at[j]
    if lvl < 0:
        pltpu.store(sc_ref, _nt(qk_ref[0, :, c0:c0 + HEAD_DIM], qk_ref[1, :, c0:c0 + HEAD_DIM]), mask=eye)
        return
    h = 1 << lvl
    x32 = _head_rows(lev_ref, lvl, j, rows)
    x = x32.astype(BF16)
    if h < SUBLANES:
        pltpu.store(sc_ref, _nt(x, x), mask=masks[lvl])
        return
    blocks = rows // (2 * h)
    upper = x32.reshape(blocks, 2 * h, HEAD_DIM)[:, h:].reshape(rows // 2, HEAD_DIM).astype(BF16)
    s = _nt(upper, x)
    for b in range(blocks):
        pltpu.store(sc_ref.at[b * 2 * h + h:(b + 1) * 2 * h, :], s[b * h:(b + 1) * h], mask=masks[lvl][b])


def _intra_scores(j, rows, levels, masks, eye, qk_ref, lev_ref, sc_ref):
    for lvl in range(-1, levels):
        _score_level(j, lvl, rows, masks, eye, qk_ref, lev_ref, sc_ref)
    return sc_ref[j]


def _window_sums(e, w, axis, hist, rows):
    s, d = e, 1
    while d < min(w, SUBLANES):
        s = s + pltpu.roll(s, d, axis)
        d *= 2
    take = lambda a, lo: lax.slice_in_dim(a, lo, lo + rows, axis=axis)
    out = take(s, hist)
    if w > SUBLANES:
        assert w == 2 * SUBLANES
        out = out + take(s, hist - SUBLANES)
    return out


def _head_output(o, j, zg, hgn_ref, mix_ref):
    c0 = j * HEAD_DIM
    o = _rms(o, hgn_ref[:, c0:c0 + HEAD_DIM]) * (zg * _sigmoid(zg))
    mix_ref[:, POOL_WIDTH + c0:POOL_WIDTH + c0 + HEAD_DIM] = o.astype(BF16)


PROMPT_ROWS = 128
PROMPT_LEVELS = 7


PROMPT_SLOTS = 2


def _prompt_prepare(z_blk, t_blk, starts_seq, slot, lb, wpool_ref, pscale_ref,
                    ext_ref, pool_ref, decay_ref, nat_ref, seg_ref, qk_ref, lev_ref):
    rows, levels, hist = PROMPT_ROWS, PROMPT_LEVELS, HIST_ROWS
    nat, seg, qk, lev = nat_ref[slot], seg_ref[slot], qk_ref[slot], lev_ref[slot]

    def pool_group(gi):
        def run():
            if gi == 0:
                ext_ref[0:hist, :] = jnp.where(starts_seq, 0.0, ext_ref[rows:rows + hist, :])
                ext_ref[hist:hist + rows, :] = z_blk[:, 0:POOL_WIDTH]
            w, c0 = POOL_WINDOWS[gi], gi * POOL_GROUP_DIM
            e = ext_ref[:, c0:c0 + POOL_GROUP_DIM]
            pos = t_blk * rows + lax.broadcasted_iota(jnp.int32, (rows, 1), 0)
            cnt = jnp.minimum(pos + 1, w).astype(F32)
            _pool_project(_window_sums(e, w, 0, hist, rows), cnt, e[hist:hist + rows], gi,
                          wpool_ref, pscale_ref, pool_ref[slot])
        return run

    def gates(c):
        cols = slice(c * HEAD_DIM, (c + 1) * HEAD_DIM)
        return lambda: _head_gates(z_blk[:, Z_Q + c * HEAD_DIM:Z_Q + (c + 1) * HEAD_DIM],
                                   z_blk[:, Z_F + c * HEAD_DIM:Z_F + (c + 1) * HEAD_DIM], lb[:, cols], c, rows, nat, qk)

    def tokens(k, n):
        def run():
            prefix = _token_levels(range(k * n, (k + 1) * n), rows, levels, nat, seg, lev)
            if (k + 1) * n == rows:
                decay_ref[slot][...] = prefix[-1]
        return run

    a, b = [pool_group(g) for g in range(len(POOL_WINDOWS))], [gates(c) for c in range(HEADS)]
    t = [tokens(k, rows // HEADS) for k in range(HEADS)]
    segments = lambda: _segment_products(rows, levels, nat, seg)
    return [[a[0], a[1], b[0]], [a[2], a[3], b[1]], b[2:5], b[5:8], [segments, t[0]], t[1:3], t[3:6], t[6:8]]


def _prompt_heads(slot, z_blk, mix_blk, hgn_ref, s_ref, pool_ref, decay_ref, qk_ref, lev_ref, sc_ref,
                  masks, eye):
    rows, levels = PROMPT_ROWS, PROMPT_LEVELS
    qk, lev, sc = qk_ref[slot], lev_ref[slot], sc_ref[slot]
    value = lambda j: z_blk[:, Z_I + j * HEAD_DIM:Z_I + (j + 1) * HEAD_DIM].astype(BF16)

    def pool_part():
        mix_blk[:, 0:POOL_WIDTH] = pool_ref[slot][...]

    def output(j):
        c0 = j * HEAD_DIM
        qt = _head_rows(lev, levels, j, rows).astype(BF16)
        o = (jnp.dot(sc[j].astype(BF16), value(j), preferred_element_type=F32)
             + jnp.dot(qt, s_ref[j].astype(BF16), preferred_element_type=F32))
        _head_output(o, j, z_blk[:, Z_G + c0:Z_G + c0 + HEAD_DIM], hgn_ref, mix_blk)

    def state(j):
        decay_t = jnp.transpose(decay_ref[slot][...])
        kt = _head_rows(lev, levels + 1, j, rows).astype(BF16)
        s_ref[j] = decay_t[:, j:j + 1] * s_ref[j] + lax.dot_general(
            kt, value(j), (((0,), (0,)), ((), ())), preferred_element_type=F32)

    heads = range(HEADS)
    scores = [functools.partial(_score_level, j, lvl, rows, masks, eye, qk, lev, sc)
              for lvl in range(-1, levels) for j in heads]
    return ([pool_part] + scores + [functools.partial(output, j) for j in heads]
            + [functools.partial(state, j) for j in heads])


def _alternate(a, b):
    out, done = [], 0
    for k, item in enumerate(a):
        out.append(item)
        upto = (k + 1) * len(b) // len(a)
        out.extend(b[done:upto])
        done = upto
    return out


def _prompt_layer_body(xc_ref, xn_ref, g1_ref, win_ref, wo_ref, lbp_ref, wpool_ref, pscale_ref, hgn_ref,
                       x1_ref, pooln_ref, snew_ref,
                       z_ref, mix_ref, ext_ref, s_ref, pool_ref, decay_ref, nat_ref, seg_ref, qk_ref,
                       lev_ref, sc_ref, *, layer, nt):
    rows, hist = PROMPT_ROWS, HIST_ROWS
    i = pl.program_id(0)
    steps_per_seq = nt // PROMPT_SLOTS
    tt = i % steps_per_seq
    lb = _lower_bound(lbp_ref[...], layer)
    one = lambda ref: [ref]
    prepare = functools.partial(
        _prompt_prepare, lb=lb, wpool_ref=wpool_ref, pscale_ref=pscale_ref, ext_ref=ext_ref,
        pool_ref=one(pool_ref), decay_ref=one(decay_ref), nat_ref=one(nat_ref), seg_ref=one(seg_ref),
        qk_ref=one(qk_ref), lev_ref=one(lev_ref))
    masks, eye = _level_masks(rows, PROMPT_LEVELS)
    heads = functools.partial(
        _prompt_heads, hgn_ref=hgn_ref, s_ref=s_ref, pool_ref=one(pool_ref), decay_ref=one(decay_ref),
        qk_ref=one(qk_ref), lev_ref=one(lev_ref), sc_ref=one(sc_ref), masks=masks, eye=eye)

    def in_proj(x_ref):
        h = _rms(x_ref[...], g1_ref[...]).astype(BF16)
        z_ref[...] = jnp.dot(h, win_ref[...], preferred_element_type=F32)

    def mix_parts(blk, t_blk, starts_seq):
        z_blk, mix_blk = z_ref.at[blk * rows:(blk + 1) * rows], mix_ref.at[blk * rows:(blk + 1) * rows]
        flat = [item for batch in prepare(z_blk, t_blk, starts_seq, 0) for item in batch]
        early = len(POOL_WINDOWS) + HEADS + 1
        return flat[:early], flat[early:], heads(0, z_blk, mix_blk)

    def mixers():
        run = lambda items: [item() for item in items]
        early0, tokens0, heads0 = mix_parts(0, PROMPT_SLOTS * tt, tt == 0)
        early1, tokens1, heads1 = mix_parts(1, PROMPT_SLOTS * tt + 1, False)
        first = 1 + HEADS
        run(early0 + tokens0 + heads0[:first])
        run(_alternate(heads0[first:], early1))
        run(tokens1 + heads1)

    def out_proj():
        x1_ref[...] = xc_ref[...] + jnp.dot(mix_ref[...], wo_ref[...], preferred_element_type=F32)

    @pl.when(i == 0)
    def _():
        sc_ref[...] = jnp.zeros_like(sc_ref)
        ext_ref[...] = jnp.zeros_like(ext_ref)
        in_proj(xc_ref)

    @pl.when(tt == 0)
    def _():
        s_ref[...] = jnp.zeros_like(s_ref)

    region = pl.when(i < pl.num_programs(0))
    region(mixers)
    region(out_proj)
    region(functools.partial(in_proj, xn_ref))
    pooln_ref[0] = ext_ref[hist + rows - POOL_CACHE:hist + rows, :]

    @pl.when(tt == steps_per_seq - 1)
    def _():
        snew_ref[0] = s_ref[...]


def _prompt_layer(x, g1, w_in, w_o, lbp, wpool, pscale, hgn, batch, seq, layer):
    rows, slots = PROMPT_ROWS, PROMPT_SLOTS
    nt = seq // rows
    assert seq % (rows * slots) == 0
    nblk = batch * nt
    steps_per_seq = nt // slots
    const2 = lambda i: (0, 0)
    resident = lambda shape: pl.BlockSpec(shape, const2, pipeline_mode=pl.Buffered(1))
    return pl.pallas_call(
        functools.partial(_prompt_layer_body, layer=layer, nt=nt),
        grid=(nblk // slots,),
        in_specs=[
            pl.BlockSpec((slots * rows, D_MODEL), lambda i: (i, 0)),
            pl.BlockSpec((slots * rows, D_MODEL), lambda i: (jnp.minimum(i + 1, nblk // slots - 1), 0)),
            pl.BlockSpec((1, D_MODEL), const2),
            resident(w_in.shape),
            resident(w_o.shape),
            pl.BlockSpec(lbp.shape, const2),
            pl.BlockSpec(wpool.shape, lambda i: (0, 0, 0)),
            pl.BlockSpec((1, POOL_WIDTH), const2),
            pl.BlockSpec((1, HG_WIDTH), const2),
        ],
        out_specs=[
            pl.BlockSpec((slots * rows, D_MODEL), lambda i: (i, 0)),
            pl.BlockSpec((1, POOL_CACHE, POOL_WIDTH), lambda i: (i // steps_per_seq, 0, 0)),
            pl.BlockSpec((1, HEADS, HEAD_DIM, HEAD_DIM), lambda i: (i // steps_per_seq, 0, 0, 0)),
        ],
        out_shape=[
            jax.ShapeDtypeStruct((batch * seq, D_MODEL), F32),
            jax.ShapeDtypeStruct((batch, POOL_CACHE, POOL_WIDTH), F32),
            jax.ShapeDtypeStruct((batch, HEADS, HEAD_DIM, HEAD_DIM), F32),
        ],
        scratch_shapes=[
            pltpu.VMEM((slots * rows, IN_COLS), F32),
            pltpu.VMEM((slots * rows, D_MODEL), BF16),
            pltpu.VMEM((HIST_ROWS + rows, POOL_WIDTH), F32),
            pltpu.VMEM((HEADS, HEAD_DIM, HEAD_DIM), F32),
            pltpu.VMEM((rows, POOL_WIDTH), BF16),
            pltpu.VMEM((HEADS, HEAD_DIM), F32),
            *_level_scratch(rows, PROMPT_LEVELS),
        ],
        compiler_params=_cparams("prompt_layer", ("arbitrary",)),
        name="prompt_layer",
    )(x, x, g1, w_in, w_o, lbp, wpool, pscale, hgn)


def _mixer_sample_body(z_ref, cache_ref, sin_ref, lbp_ref, wpool_ref, pscale_ref, hgn_ref,
                       mix_ref, pooln_ref, snew_ref,
                       ext_ref, nat_ref, seg_ref, qk_ref, lev_ref, sc_ref, *, layer, nseq, steps, pos0):
    rows = nseq * steps
    levels = steps.bit_length() - 1
    hist = HIST_ROWS

    @pl.when(pl.program_id(0) == 0)
    def _():
        sc_ref[...] = jnp.zeros_like(sc_ref)

    u_nat = z_ref[:, 0:POOL_WIDTH]
    ext_ref[0:hist - POOL_CACHE] = jnp.zeros((hist - POOL_CACHE, nseq, POOL_WIDTH), F32)
    ext_ref[hist - POOL_CACHE:hist] = cache_ref[0]
    ext_ref[hist:hist + steps] = jnp.swapaxes(u_nat.reshape(nseq, steps, POOL_WIDTH), 0, 1)
    pos = pos0 + lax.broadcasted_iota(jnp.int32, (steps, 1, 1), 0)
    for gi, w in enumerate(POOL_WINDOWS):
        c0 = gi * POOL_GROUP_DIM
        acc = ext_ref[hist:hist + steps, :, c0:c0 + POOL_GROUP_DIM]
        for d in range(1, w):
            acc = acc + ext_ref[hist - d:hist - d + steps, :, c0:c0 + POOL_GROUP_DIM]
        mean = acc / jnp.minimum(pos + 1, w).astype(F32)
        mean = jnp.swapaxes(mean, 0, 1).reshape(rows, POOL_GROUP_DIM)
        _pool_project(mean, 1.0, u_nat[:, c0:c0 + POOL_GROUP_DIM], gi, wpool_ref, pscale_ref, mix_ref)
    pooln_ref[...] = ext_ref[hist + steps - POOL_CACHE:hist + steps]

    lb = _lower_bound(lbp_ref[...], layer)
    for c in range(HEADS):
        cols = slice(c * HEAD_DIM, (c + 1) * HEAD_DIM)
        _head_gates(z_ref[:, Z_Q + c * HEAD_DIM:Z_Q + (c + 1) * HEAD_DIM],
                    z_ref[:, Z_F + c * HEAD_DIM:Z_F + (c + 1) * HEAD_DIM], lb[:, cols], c, rows, nat_ref, qk_ref)
    _segment_products(rows, levels, nat_ref, seg_ref)
    prefix = _token_levels(range(rows), rows, levels, nat_ref, seg_ref, lev_ref)
    decay = jnp.stack(prefix[steps - 1::steps])
    decay_t = jnp.swapaxes(decay, 1, 2)
    masks, eye = _level_masks(rows, levels)
    for j in range(HEADS):
        c0 = j * HEAD_DIM
        scores = _intra_scores(j, rows, levels, masks, eye, qk_ref, lev_ref, sc_ref)
        v32 = z_ref[:, Z_I + c0:Z_I + c0 + HEAD_DIM]
        v = v32.astype(BF16)
        v3 = v32.reshape(nseq, steps, HEAD_DIM).astype(BF16)
        qt3 = _head_rows(lev_ref, levels, j, rows).reshape(nseq, steps, HEAD_DIM).astype(BF16)
        kt3 = _head_rows(lev_ref, levels + 1, j, rows).reshape(nseq, steps, HEAD_DIM).astype(BF16)
        s_in = sin_ref[:, j]
        o_state = lax.dot_general(qt3, s_in.astype(BF16), (((2,), (1,)), ((0,), (0,))),
                                  preferred_element_type=F32)
        o = jnp.dot(scores.astype(BF16), v, preferred_element_type=F32) + o_state.reshape(rows, HEAD_DIM)
        upd = lax.dot_general(kt3, v3, (((1,), (1,)), ((0,), (0,))), preferred_element_type=F32)
        snew_ref[:, j] = decay_t[:, :, j:j + 1] * s_in + upd
        _head_output(o, j, z_ref[:, Z_G + c0:Z_G + c0 + HEAD_DIM], hgn_ref, mix_ref)


def _mixer_sample(z, cache, state, lbp, wpool, pscale, hgn, nseq_total, steps, pos0, layer, nseq):
    rows = nseq * steps
    levels = steps.bit_length() - 1
    assert 1 << levels == steps and nseq_total % nseq == 0
    hist = HIST_ROWS
    const2 = lambda i: (0, 0)
    return pl.pallas_call(
        functools.partial(_mixer_sample_body, layer=layer, nseq=nseq, steps=steps, pos0=pos0),
        grid=(nseq_total // nseq,),
        in_specs=[
            pl.BlockSpec((rows, IN_COLS), lambda i: (i, 0)),
            pl.BlockSpec((1, POOL_CACHE, nseq, POOL_WIDTH), lambda i: (layer, 0, i, 0)),
            pl.BlockSpec((nseq, HEADS, HEAD_DIM, HEAD_DIM), lambda i: (i, 0, 0, 0)),
            pl.BlockSpec(lbp.shape, const2),
            pl.BlockSpec(wpool.shape, lambda i: (0, 0, 0)),
            pl.BlockSpec((1, POOL_WIDTH), const2),
            pl.BlockSpec((1, HG_WIDTH), const2),
        ],
        out_specs=[
            pl.BlockSpec((rows, D_MODEL), lambda i: (i, 0)),
            pl.BlockSpec((POOL_CACHE, nseq, POOL_WIDTH), lambda i: (0, i, 0)),
            pl.BlockSpec((nseq, HEADS, HEAD_DIM, HEAD_DIM), lambda i: (i, 0, 0, 0)),
        ],
        out_shape=[
            jax.ShapeDtypeStruct((nseq_total * steps, D_MODEL), BF16),
            jax.ShapeDtypeStruct((POOL_CACHE, nseq_total, POOL_WIDTH), F32),
            jax.ShapeDtypeStruct((nseq_total, HEADS, HEAD_DIM, HEAD_DIM), F32),
        ],
        scratch_shapes=[
            pltpu.VMEM((hist + steps, nseq, POOL_WIDTH), F32),
            *_level_scratch(rows, levels),
        ],
        compiler_params=_cparams("mixer_sample", ("arbitrary",)),
        name="mixer_sample",
    )(z, cache, state, lbp, wpool, pscale, hgn)


def _out_proj_body(x_ref, mix_ref, w_ref, o_ref, wb_ref):
    w = w_ref[...].astype(BF16)
    wb_ref[...] = w
    o_ref[...] = x_ref[...] + jnp.dot(mix_ref[...], w, preferred_element_type=F32)


def _out_proj(x, mix, w, tm, tn):
    m = x.shape[0]
    return pl.pallas_call(
        _out_proj_body,
        grid=(m // tm, D_MODEL // tn),
        in_specs=[
            pl.BlockSpec((tm, tn), lambda i, j: (i, j)),
            pl.BlockSpec((tm, D_MODEL), lambda i, j: (i, 0)),
            pl.BlockSpec((D_MODEL, tn), lambda i, j: (0, j)),
        ],
        out_specs=[pl.BlockSpec((tm, tn), lambda i, j: (i, j)), pl.BlockSpec((D_MODEL, tn), lambda i, j: (0, j))],
        out_shape=[jax.ShapeDtypeStruct((m, D_MODEL), F32), jax.ShapeDtypeStruct(w.shape, BF16)],
        compiler_params=_cparams("out_proj", ("arbitrary", "arbitrary")),
        name="out_proj",
    )(x, mix, w)


def _ffn_body(x_ref, g2_ref, wg_ref, wu_ref, wd_ref, gf_ref, y_ref, *rest):
    *wb_refs, h_ref = rest
    j = pl.program_id(1)
    last = pl.num_programs(1) - 1
    chunks = _row_chunks(x_ref.shape[0])

    def down(h, wg, wu, wd):
        gate = jnp.dot(h, wg, preferred_element_type=F32)
        up = jnp.dot(h, wu, preferred_element_type=F32)
        act = (gate * _sigmoid(gate) * up).astype(BF16)
        return jnp.dot(act, wd, preferred_element_type=F32)

    def weights():
        if not wb_refs:
            return wg_ref[...], wu_ref[...], wd_ref[...]
        w = tuple(ref[...].astype(BF16) for ref in (wg_ref, wu_ref, wd_ref))
        for wb_ref, wb in zip(wb_refs, w):
            wb_ref[...] = wb
        return w

    @pl.when(j == 0)
    def _():
        w = weights()
        for rs in chunks:
            h = _rms(x_ref[rs, :], g2_ref[...]).astype(BF16)
            h_ref[rs, :] = h
            y_ref[rs, :] = down(h, *w)

    @pl.when((j > 0) & (j < last))
    def _():
        y_ref[...] += down(h_ref[...], *weights())

    @pl.when(j == last)
    def _():
        w = weights()
        for rs in chunks:
            y_ref[rs, :] = _rms(x_ref[rs, :] + y_ref[rs, :] + down(h_ref[rs, :], *w), gf_ref[...])


def _ffn(x, g2, wg, wu, wd, gf, tm, tf):
    m = x.shape[0]
    d_ff = wg.shape[1]
    emit = wg.dtype != BF16
    w_specs = [
        pl.BlockSpec((D_MODEL, tf), lambda i, j: (0, j)),
        pl.BlockSpec((D_MODEL, tf), lambda i, j: (0, j)),
        pl.BlockSpec((tf, D_MODEL), lambda i, j: (j, 0)),
    ]
    x_mode = dict(pipeline_mode=pl.Buffered(1)) if m == tm else {}
    return pl.pallas_call(
        _ffn_body,
        grid=(m // tm, d_ff // tf),
        in_specs=[
            pl.BlockSpec((tm, D_MODEL), lambda i, j: (i, 0), **x_mode),
            pl.BlockSpec((1, D_MODEL), lambda i, j: (0, 0)),
            *w_specs,
            pl.BlockSpec((1, D_MODEL), lambda i, j: (0, 0)),
        ],
        out_specs=[pl.BlockSpec((tm, D_MODEL), lambda i, j: (i, 0))] + (w_specs if emit else []),
        out_shape=[jax.ShapeDtypeStruct((m, D_MODEL), F32)]
        + ([jax.ShapeDtypeStruct(a.shape, BF16) for a in (wg, wu, wd)] if emit else []),
        scratch_shapes=[pltpu.VMEM((tm, D_MODEL), BF16)],
        compiler_params=_cparams("ffn", ("arbitrary", "arbitrary")),
        name="ffn",
    )(x, g2, wg, wu, wd, gf)


def kernel(x_prompt, x_sample, cache_pool, state_hgrn, lb_param, norm1, w_in, w_pool, pool_scale,
           hg_norm, w_o, norm2, w_gate, w_up, w_down, norm_f):
    depth = w_in.shape[0]
    assert depth == 1, "single-layer trunk"
    layer = 0
    batch, seq, _ = x_prompt.shape
    dec_batch, dec_seq, _ = x_sample.shape

    row = lambda a: a.reshape(1, -1).astype(F32)
    g1, g2, gf = row(norm1[layer]), row(norm2[layer]), row(norm_f)
    lbp = lb_param.astype(F32)
    wpool = w_pool[layer].astype(BF16)
    pscale = row(pool_scale[layer])
    hgn = row(hg_norm[layer])

    m_s = dec_batch * dec_seq
    xs = x_sample.reshape(m_s, D_MODEL)
    z_s, w_in_b = _in_proj(xs, g1, w_in[layer], tm=m_s, tn=SAMPLE_PROJ_COLS)
    mix_s, pool_s, s_s = _mixer_sample(z_s, jnp.swapaxes(cache_pool, 1, 2), state_hgrn[layer], lbp, wpool, pscale,
                                       hgn, dec_batch, dec_seq, PAST_LEN, layer, nseq=SAMPLE_SEQS)
    pool_s = jnp.swapaxes(pool_s, 0, 1)
    x1_s, w_o_b = _out_proj(xs, mix_s, w_o[layer], tm=m_s, tn=SAMPLE_PROJ_COLS)
    y_s, w_gate_b, w_up_b, w_down_b = _ffn(x1_s, g2, w_gate[layer], w_up[layer], w_down[layer], gf,
                                           tm=m_s, tf=SAMPLE_FFN_COLS)

    x1_p, pool_p, s_p = _prompt_layer(x_prompt.reshape(batch * seq, D_MODEL), g1, w_in_b, w_o_b,
                                      lbp, wpool, pscale, hgn, batch, seq, layer)
    (y_p,) = _ffn(x1_p, g2, w_gate_b, w_up_b, w_down_b, gf, tm=PROMPT_FFN_ROWS, tf=PROMPT_FFN_COLS)

    return (y_p.reshape(batch, seq, D_MODEL), y_s.reshape(dec_batch, dec_seq, D_MODEL),
            pool_p[None], s_p[None], pool_s[None], s_s[None])
```

```python
import functools

import jax
import jax.numpy as jnp
from jax import lax
from jax.experimental import pallas as pl
from jax.experimental.pallas import tpu as pltpu

D_MODEL = 2048
POOL_WIDTH = 1024
POOL_WINDOWS = (2, 4, 8, 16)
POOL_GROUP_DIM = POOL_WIDTH // len(POOL_WINDOWS)
POOL_CACHE = max(POOL_WINDOWS) - 1
HG_WIDTH = 1024
HEAD_DIM = 128
HEADS = HG_WIDTH // HEAD_DIM
IN_COLS = POOL_WIDTH + 4 * HG_WIDTH
Z_Q, Z_F, Z_I, Z_G = (POOL_WIDTH + n * HG_WIDTH for n in range(4))
EPS = 1e-6
SUBLANES = 8
assert HEADS == SUBLANES
HIST_ROWS = 2 * SUBLANES
PAST_LEN = 16384

F32 = jnp.float32
BF16 = jnp.bfloat16

MIB = 1024 * 1024
SAMPLE_PROJ_COLS = 512
SAMPLE_FFN_COLS = 256
SAMPLE_SEQS = 8
PROMPT_FFN_ROWS = 1024
PROMPT_FFN_COLS = 512
VMEM_MIB = dict(in_proj=48, out_proj=48, mixer_sample=48, ffn=58, prompt_layer=60)


def _cparams(name, sem):
    return pltpu.CompilerParams(dimension_semantics=sem, vmem_limit_bytes=VMEM_MIB[name] * MIB)


def _rms(x, g):
    return x * lax.rsqrt(jnp.mean(x * x, axis=-1, keepdims=True) + EPS) * g


def _sigmoid(x):
    return 1.0 / (1.0 + jnp.exp(-x))


ROW_CHUNK = 256


def _row_chunks(n):
    return [slice(r, r + ROW_CHUNK) for r in range(0, n, ROW_CHUNK)]


def _in_proj_body(x_ref, g_ref, w_ref, z_ref, wb_ref, h_ref):
    j = pl.program_id(1)
    w = w_ref[...].astype(BF16)
    wb_ref[...] = w

    @pl.when(j == 0)
    def _():
        for rs in _row_chunks(x_ref.shape[0]):
            h = _rms(x_ref[rs, :], g_ref[...]).astype(BF16)
            h_ref[rs, :] = h
            z_ref[rs, :] = jnp.dot(h, w, preferred_element_type=F32)

    @pl.when(j > 0)
    def _():
        z_ref[...] = jnp.dot(h_ref[...], w, preferred_element_type=F32)


def _in_proj(x, g, w, tm, tn):
    m = x.shape[0]
    return pl.pallas_call(
        _in_proj_body,
        grid=(m // tm, IN_COLS // tn),
        in_specs=[
            pl.BlockSpec((tm, D_MODEL), lambda i, j: (i, 0)),
            pl.BlockSpec((1, D_MODEL), lambda i, j: (0, 0)),
            pl.BlockSpec((D_MODEL, tn), lambda i, j: (0, j)),
        ],
        out_specs=[pl.BlockSpec((tm, tn), lambda i, j: (i, j)), pl.BlockSpec((D_MODEL, tn), lambda i, j: (0, j))],
        out_shape=[jax.ShapeDtypeStruct((m, IN_COLS), F32), jax.ShapeDtypeStruct(w.shape, BF16)],
        scratch_shapes=[pltpu.VMEM((tm, D_MODEL), BF16)],
        compiler_params=_cparams("in_proj", ("arbitrary", "arbitrary")),
        name="in_proj",
    )(x, g, w)


def _lower_bound(lbp, layer):
    e = jnp.exp(lbp - jnp.max(lbp, axis=0, keepdims=True))
    return jnp.sum(e[: layer + 1], axis=0, keepdims=True) / jnp.sum(e, axis=0, keepdims=True)


def _pool_project(acc, cnt, u, gi, wpool_ref, pscale_ref, mix_ref):
    c0 = gi * POOL_GROUP_DIM
    pooled = acc / cnt - u
    out = jnp.dot(pooled.astype(BF16), wpool_ref[gi], preferred_element_type=F32)
    mix_ref[:, c0:c0 + POOL_GROUP_DIM] = (out * pscale_ref[:, c0:c0 + POOL_GROUP_DIM]).astype(BF16)


def _token_tile(nat_ref, a, t):
    return nat_ref[a, t // SUBLANES, pl.ds(t % SUBLANES, HEADS, stride=SUBLANES), :]


def _head_rows(lev_ref, lvl, j, rows):
    return lev_ref[lvl, pl.ds(j, rows, stride=HEADS), :]


def _head_gates(zq, zf, lb, c, rows, nat_ref, qk_ref):
    fg = lb + (1.0 - lb) * _sigmoid(zf)
    qs = zq * _sigmoid(zq)
    kk = 1.0 - fg
    qk_ref[0, :, c * HEAD_DIM:(c + 1) * HEAD_DIM] = qs.astype(BF16)
    qk_ref[1, :, c * HEAD_DIM:(c + 1) * HEAD_DIM] = kk.astype(BF16)
    for a, x in enumerate((qs, kk, fg)):
        nat_ref[a, :, c * SUBLANES:(c + 1) * SUBLANES, :] = x.reshape(rows // SUBLANES, SUBLANES, HEAD_DIM)


def _segment_offsets(rows, levels):
    off = [0]
    for lvl in range(levels):
        off.append(off[-1] + (rows >> lvl))
    return off


def _segment_products(rows, levels, nat_ref, seg_ref):
    off = _segment_offsets(rows, levels)
    for t in range(rows):
        seg_ref[t] = _token_tile(nat_ref, 2, t)
    for lvl in range(1, levels):
        for m in range(rows >> lvl):
            seg_ref[off[lvl] + m] = seg_ref[off[lvl - 1] + 2 * m] * seg_ref[off[lvl - 1] + 2 * m + 1]


def _token_levels(tokens, rows, levels, nat_ref, seg_ref, lev_ref):
    off = _segment_offsets(rows, levels)
    prefix = []
    for t in tokens:
        tok = slice(t * HEADS, (t + 1) * HEADS)
        q_t, k_t = _token_tile(nat_ref, 0, t), _token_tile(nat_ref, 1, t)
        p_t, r_t = seg_ref[t], None
        for lvl in range(levels):
            m = t >> lvl
            sibling = seg_ref[off[lvl] + (m ^ 1)]
            if m & 1:
                lev_ref[lvl, tok, :] = q_t * p_t
                p_t = p_t * sibling
            else:
                lev_ref[lvl, tok, :] = k_t if r_t is None else k_t * r_t
                r_t = sibling if r_t is None else r_t * sibling
        lev_ref[levels, tok, :] = q_t * p_t
        lev_ref[levels + 1, tok, :] = k_t if r_t is None else k_t * r_t
        prefix.append(p_t)
    return prefix


def _level_scratch(rows, levels):
    return [
        pltpu.VMEM((3, rows // SUBLANES, HEADS * SUBLANES, HEAD_DIM), F32),
        pltpu.VMEM((2 * rows, HEADS, HEAD_DIM), F32),
        pltpu.VMEM((2, rows, HG_WIDTH), BF16),
        pltpu.VMEM((levels + 2, rows * HEADS, HEAD_DIM), F32),
        pltpu.VMEM((HEADS, rows, rows), F32),
    ]


def _level_masks(rows, levels):
    ti = lax.broadcasted_iota(jnp.int32, (rows, rows), 0)
    si = lax.broadcasted_iota(jnp.int32, (rows, rows), 1)
    x = ti ^ si
    masks = []
    for lvl in range(levels):
        h = 1 << lvl
        if h < SUBLANES:
            masks.append(((x >> lvl) == 1) & (((ti >> lvl) & 1) == 1))
        else:
            half = lax.broadcasted_iota(jnp.int32, (h, rows), 1) >> lvl
            masks.append([half == 2 * b for b in range(rows // (2 * h))])
    return masks, ti == si


def _nt(a, b):
    return lax.dot_general(a, b, (((1,), (1,)), ((), ())), preferred_element_type=F32)


def _score_level(j, lvl, rows, masks, eye, qk_ref, lev_ref, sc_ref):
    c0 = j * HEAD_DIM
    sc_ref = sc_ref.at[j]
    if lvl < 0:
        pltpu.store(sc_ref, _nt(qk_ref[0, :, c0:c0 + HEAD_DIM], qk_ref[1, :, c0:c0 + HEAD_DIM]), mask=eye)
        return
    h = 1 << lvl
    x32 = _head_rows(lev_ref, lvl, j, rows)
    x = x32.astype(BF16)
    if h < SUBLANES:
        pltpu.store(sc_ref, _nt(x, x), mask=masks[lvl])
        return
    blocks = rows // (2 * h)
    upper = x32.reshape(blocks, 2 * h, HEAD_DIM)[:, h:].reshape(rows // 2, HEAD_DIM).astype(BF16)
    s = _nt(upper, x)
    for b in range(blocks):
        pltpu.store(sc_ref.at[b * 2 * h + h:(b + 1) * 2 * h, :], s[b * h:(b + 1) * h], mask=masks[lvl][b])


def _intra_scores(j, rows, levels, masks, eye, qk_ref, lev_ref, sc_ref):
    for lvl in range(-1, levels):
        _score_level(j, lvl, rows, masks, eye, qk_ref, lev_ref, sc_ref)
    return sc_ref[j]


def _window_sums(e, w, axis, hist, rows):
    s, d = e, 1
    while d < min(w, SUBLANES):
        s = s + pltpu.roll(s, d, axis)
        d *= 2
    take = lambda a, lo: lax.slice_in_dim(a, lo, lo + rows, axis=axis)
    out = take(s, hist)
    if w > SUBLANES:
        assert w == 2 * SUBLANES
        out = out + take(s, hist - SUBLANES)
    return out


def _head_output(o, j, zg, hgn_ref, mix_ref):
    c0 = j * HEAD_DIM
    o = _rms(o, hgn_ref[:, c0:c0 + HEAD_DIM]) * (zg * _sigmoid(zg))
    mix_ref[:, POOL_WIDTH + c0:POOL_WIDTH + c0 + HEAD_DIM] = o.astype(BF16)


PROMPT_ROWS = 128
PROMPT_LEVELS = 7


PROMPT_SLOTS = 2


def _prompt_prepare(z_blk, t_blk, starts_seq, slot, lb, wpool_ref, pscale_ref,
                    ext_ref, pool_ref, decay_ref, nat_ref, seg_ref, qk_ref, lev_ref):
    rows, levels, hist = PROMPT_ROWS, PROMPT_LEVELS, HIST_ROWS
    nat, seg, qk, lev = nat_ref[slot], seg_ref[slot], qk_ref[slot], lev_ref[slot]

    def pool_group(gi):
        def run():
            if gi == 0:
                ext_ref[0:hist, :] = jnp.where(starts_seq, 0.0, ext_ref[rows:rows + hist, :])
                ext_ref[hist:hist + rows, :] = z_blk[:, 0:POOL_WIDTH]
            w, c0 = POOL_WINDOWS[gi], gi * POOL_GROUP_DIM
            e = ext_ref[:, c0:c0 + POOL_GROUP_DIM]
            pos = t_blk * rows + lax.broadcasted_iota(jnp.int32, (rows, 1), 0)
            cnt = jnp.minimum(pos + 1, w).astype(F32)
            _pool_project(_window_sums(e, w, 0, hist, rows), cnt, e[hist:hist + rows], gi,
                          wpool_ref, pscale_ref, pool_ref[slot])
        return run

    def gates(c):
        cols = slice(c * HEAD_DIM, (c + 1) * HEAD_DIM)
        return lambda: _head_gates(z_blk[:, Z_Q + c * HEAD_DIM:Z_Q + (c + 1) * HEAD_DIM],
                                   z_blk[:, Z_F + c * HEAD_DIM:Z_F + (c + 1) * HEAD_DIM], lb[:, cols], c, rows, nat, qk)

    def tokens(k, n):
        def run():
            prefix = _token_levels(range(k * n, (k + 1) * n), rows, levels, nat, seg, lev)
            if (k + 1) * n == rows:
                decay_ref[slot][...] = prefix[-1]
        return run

    a, b = [pool_group(g) for g in range(len(POOL_WINDOWS))], [gates(c) for c in range(HEADS)]
    t = [tokens(k, rows // HEADS) for k in range(HEADS)]
    segments = lambda: _segment_products(rows, levels, nat, seg)
    return [[a[0], a[1], b[0]], [a[2], a[3], b[1]], b[2:5], b[5:8], [segments, t[0]], t[1:3], t[3:6], t[6:8]]


def _prompt_heads(slot, z_blk, mix_blk, hgn_ref, s_ref, pool_ref, decay_ref, qk_ref, lev_ref, sc_ref,
                  masks, eye):
    rows, levels = PROMPT_ROWS, PROMPT_LEVELS
    qk, lev, sc = qk_ref[slot], lev_ref[slot], sc_ref[slot]
    value = lambda j: z_blk[:, Z_I + j * HEAD_DIM:Z_I + (j + 1) * HEAD_DIM].astype(BF16)

    def pool_part():
        mix_blk[:, 0:POOL_WIDTH] = pool_ref[slot][...]

    def output(j):
        c0 = j * HEAD_DIM
        qt = _head_rows(lev, levels, j, rows).astype(BF16)
        o = (jnp.dot(sc[j].astype(BF16), value(j), preferred_element_type=F32)
             + jnp.dot(qt, s_ref[j].astype(BF16), preferred_element_type=F32))
        _head_output(o, j, z_blk[:, Z_G + c0:Z_G + c0 + HEAD_DIM], hgn_ref, mix_blk)

    def state(j):
        decay_t = jnp.transpose(decay_ref[slot][...])
        kt = _head_rows(lev, levels + 1, j, rows).astype(BF16)
        s_ref[j] = decay_t[:, j:j + 1] * s_ref[j] + lax.dot_general(
            kt, value(j), (((0,), (0,)), ((), ())), preferred_element_type=F32)

    heads = range(HEADS)
    scores = [functools.partial(_score_level, j, lvl, rows, masks, eye, qk, lev, sc)
              for lvl in range(-1, levels) for j in heads]
    return ([pool_part] + scores + [functools.partial(output, j) for j in heads]
            + [functools.partial(state, j) for j in heads])


def _prompt_layer_body(xc_ref, g1_ref, win_ref, wo_ref, lbp_ref, wpool_ref, pscale_ref, hgn_ref,
                       x1_ref, pooln_ref, snew_ref,
                       z_ref, mix_ref, ext_ref, s_ref, pool_ref, decay_ref, nat_ref, seg_ref, qk_ref,
                       lev_ref, sc_ref, *, layer, nt):
    rows, hist = PROMPT_ROWS, HIST_ROWS
    i = pl.program_id(0)
    steps_per_seq = nt // PROMPT_SLOTS
    tt = i % steps_per_seq
    lb = _lower_bound(lbp_ref[...], layer)
    one = lambda ref: [ref]
    prepare = functools.partial(
        _prompt_prepare, lb=lb, wpool_ref=wpool_ref, pscale_ref=pscale_ref, ext_ref=ext_ref,
        pool_ref=one(pool_ref), decay_ref=one(decay_ref), nat_ref=one(nat_ref), seg_ref=one(seg_ref),
        qk_ref=one(qk_ref), lev_ref=one(lev_ref))
    masks, eye = _level_masks(rows, PROMPT_LEVELS)
    heads = functools.partial(
        _prompt_heads, hgn_ref=hgn_ref, s_ref=s_ref, pool_ref=one(pool_ref), decay_ref=one(decay_ref),
        qk_ref=one(qk_ref), lev_ref=one(lev_ref), sc_ref=one(sc_ref), masks=masks, eye=eye)

    def in_proj(x_ref):
        h = _rms(x_ref[...], g1_ref[...]).astype(BF16)
        z_ref[...] = jnp.dot(h, win_ref[...], preferred_element_type=F32)

    def mix_block(blk, t_blk, starts_seq):
        z_blk, mix_blk = z_ref.at[blk * rows:(blk + 1) * rows], mix_ref.at[blk * rows:(blk + 1) * rows]
        for batch in prepare(z_blk, t_blk, starts_seq, 0):
            for item in batch:
                item()
        for head in heads(0, z_blk, mix_blk):
            head()

    def out_proj():
        x1_ref[...] = xc_ref[...] + jnp.dot(mix_ref[...], wo_ref[...], preferred_element_type=F32)

    @pl.when(i == 0)
    def _():
        sc_ref[...] = jnp.zeros_like(sc_ref)
        ext_ref[...] = jnp.zeros_like(ext_ref)

    @pl.when(tt == 0)
    def _():
        s_ref[...] = jnp.zeros_like(s_ref)

    region = pl.when(i < pl.num_programs(0))
    region(functools.partial(in_proj, xc_ref))
    region(functools.partial(mix_block, 0, PROMPT_SLOTS * tt, tt == 0))
    region(functools.partial(mix_block, 1, PROMPT_SLOTS * tt + 1, False))
    region(out_proj)
    pooln_ref[0] = ext_ref[hist + rows - POOL_CACHE:hist + rows, :]

    @pl.when(tt == steps_per_seq - 1)
    def _():
        snew_ref[0] = s_ref[...]


def _prompt_layer(x, g1, w_in, w_o, lbp, wpool, pscale, hgn, batch, seq, layer):
    rows, slots = PROMPT_ROWS, PROMPT_SLOTS
    nt = seq // rows
    assert seq % (rows * slots) == 0
    nblk = batch * nt
    steps_per_seq = nt // slots
    const2 = lambda i: (0, 0)
    resident = lambda shape: pl.BlockSpec(shape, const2, pipeline_mode=pl.Buffered(1))
    return pl.pallas_call(
        functools.partial(_prompt_layer_body, layer=layer, nt=nt),
        grid=(nblk // slots,),
        in_specs=[
            pl.BlockSpec((slots * rows, D_MODEL), lambda i: (i, 0)),
            pl.BlockSpec((1, D_MODEL), const2),
            resident(w_in.shape),
            resident(w_o.shape),
            pl.BlockSpec(lbp.shape, const2),
            pl.BlockSpec(wpool.shape, lambda i: (0, 0, 0)),
            pl.BlockSpec((1, POOL_WIDTH), const2),
            pl.BlockSpec((1, HG_WIDTH), const2),
        ],
        out_specs=[
            pl.BlockSpec((slots * rows, D_MODEL), lambda i: (i, 0)),
            pl.BlockSpec((1, POOL_CACHE, POOL_WIDTH), lambda i: (i // steps_per_seq, 0, 0)),
            pl.BlockSpec((1, HEADS, HEAD_DIM, HEAD_DIM), lambda i: (i // steps_per_seq, 0, 0, 0)),
        ],
        out_shape=[
            jax.ShapeDtypeStruct((batch * seq, D_MODEL), F32),
            jax.ShapeDtypeStruct((batch, POOL_CACHE, POOL_WIDTH), F32),
            jax.ShapeDtypeStruct((batch, HEADS, HEAD_DIM, HEAD_DIM), F32),
        ],
        scratch_shapes=[
            pltpu.VMEM((slots * rows, IN_COLS), F32),
            pltpu.VMEM((slots * rows, D_MODEL), BF16),
            pltpu.VMEM((HIST_ROWS + rows, POOL_WIDTH), F32),
            pltpu.VMEM((HEADS, HEAD_DIM, HEAD_DIM), F32),
            pltpu.VMEM((rows, POOL_WIDTH), BF16),
            pltpu.VMEM((HEADS, HEAD_DIM), F32),
            *_level_scratch(rows, PROMPT_LEVELS),
        ],
        compiler_params=_cparams("prompt_layer", ("arbitrary",)),
        name="prompt_layer",
    )(x, g1, w_in, w_o, lbp, wpool, pscale, hgn)


def _mixer_sample_body(z_ref, cache_ref, sin_ref, lbp_ref, wpool_ref, pscale_ref, hgn_ref,
                       mix_ref, pooln_ref, snew_ref,
                       ext_ref, nat_ref, seg_ref, qk_ref, lev_ref, sc_ref, *, layer, nseq, steps, pos0):
    rows = nseq * steps
    levels = steps.bit_length() - 1
    hist = HIST_ROWS

    @pl.when(pl.program_id(0) == 0)
    def _():
        sc_ref[...] = jnp.zeros_like(sc_ref)

    u_nat = z_ref[:, 0:POOL_WIDTH]
    ext_ref[0:hist - POOL_CACHE] = jnp.zeros((hist - POOL_CACHE, nseq, POOL_WIDTH), F32)
    ext_ref[hist - POOL_CACHE:hist] = cache_ref[0]
    ext_ref[hist:hist + steps] = jnp.swapaxes(u_nat.reshape(nseq, steps, POOL_WIDTH), 0, 1)
    pos = pos0 + lax.broadcasted_iota(jnp.int32, (steps, 1, 1), 0)
    for gi, w in enumerate(POOL_WINDOWS):
        c0 = gi * POOL_GROUP_DIM
        acc = ext_ref[hist:hist + steps, :, c0:c0 + POOL_GROUP_DIM]
        for d in range(1, w):
            acc = acc + ext_ref[hist - d:hist - d + steps, :, c0:c0 + POOL_GROUP_DIM]
        mean = acc / jnp.minimum(pos + 1, w).astype(F32)
        mean = jnp.swapaxes(mean, 0, 1).reshape(rows, POOL_GROUP_DIM)
        _pool_project(mean, 1.0, u_nat[:, c0:c0 + POOL_GROUP_DIM], gi, wpool_ref, pscale_ref, mix_ref)
    pooln_ref[...] = ext_ref[hist + steps - POOL_CACHE:hist + steps]

    lb = _lower_bound(lbp_ref[...], layer)
    for c in range(HEADS):
        cols = slice(c * HEAD_DIM, (c + 1) * HEAD_DIM)
        _head_gates(z_ref[:, Z_Q + c * HEAD_DIM:Z_Q + (c + 1) * HEAD_DIM],
                    z_ref[:, Z_F + c * HEAD_DIM:Z_F + (c + 1) * HEAD_DIM], lb[:, cols], c, rows, nat_ref, qk_ref)
    _segment_products(rows, levels, nat_ref, seg_ref)
    prefix = _token_levels(range(rows), rows, levels, nat_ref, seg_ref, lev_ref)
    decay = jnp.stack(prefix[steps - 1::steps])
    decay_t = jnp.swapaxes(decay, 1, 2)
    masks, eye = _level_masks(rows, levels)
    for j in range(HEADS):
        c0 = j * HEAD_DIM
        scores = _intra_scores(j, rows, levels, masks, eye, qk_ref, lev_ref, sc_ref)
        v32 = z_ref[:, Z_I + c0:Z_I + c0 + HEAD_DIM]
        v = v32.astype(BF16)
        v3 = v32.reshape(nseq, steps, HEAD_DIM).astype(BF16)
        qt3 = _head_rows(lev_ref, levels, j, rows).reshape(nseq, steps, HEAD_DIM).astype(BF16)
        kt3 = _head_rows(lev_ref, levels + 1, j, rows).reshape(nseq, steps, HEAD_DIM).astype(BF16)
        s_in = sin_ref[:, j]
        o_state = lax.dot_general(qt3, s_in.astype(BF16), (((2,), (1,)), ((0,), (0,))),
                                  preferred_element_type=F32)
        o = jnp.dot(scores.astype(BF16), v, preferred_element_type=F32) + o_state.reshape(rows, HEAD_DIM)
        upd = lax.dot_general(kt3, v3, (((1,), (1,)), ((0,), (0,))), preferred_element_type=F32)
        snew_ref[:, j] = decay_t[:, :, j:j + 1] * s_in + upd
        _head_output(o, j, z_ref[:, Z_G + c0:Z_G + c0 + HEAD_DIM], hgn_ref, mix_ref)


def _mixer_sample(z, cache, state, lbp, wpool, pscale, hgn, nseq_total, steps, pos0, layer, nseq):
    rows = nseq * steps
    levels = steps.bit_length() - 1
    assert 1 << levels == steps and nseq_total % nseq == 0
    hist = HIST_ROWS
    const2 = lambda i: (0, 0)
    return pl.pallas_call(
        functools.partial(_mixer_sample_body, layer=layer, nseq=nseq, steps=steps, pos0=pos0),
        grid=(nseq_total // nseq,),
        in_specs=[
            pl.BlockSpec((rows, IN_COLS), lambda i: (i, 0)),
            pl.BlockSpec((1, POOL_CACHE, nseq, POOL_WIDTH), lambda i: (layer, 0, i, 0)),
            pl.BlockSpec((nseq, HEADS, HEAD_DIM, HEAD_DIM), lambda i: (i, 0, 0, 0)),
            pl.BlockSpec(lbp.shape, const2),
            pl.BlockSpec(wpool.shape, lambda i: (0, 0, 0)),
            pl.BlockSpec((1, POOL_WIDTH), const2),
            pl.BlockSpec((1, HG_WIDTH), const2),
        ],
        out_specs=[
            pl.BlockSpec((rows, D_MODEL), lambda i: (i, 0)),
            pl.BlockSpec((POOL_CACHE, nseq, POOL_WIDTH), lambda i: (0, i, 0)),
            pl.BlockSpec((nseq, HEADS, HEAD_DIM, HEAD_DIM), lambda i: (i, 0, 0, 0)),
        ],
        out_shape=[
            jax.ShapeDtypeStruct((nseq_total * steps, D_MODEL), BF16),
            jax.ShapeDtypeStruct((POOL_CACHE, nseq_total, POOL_WIDTH), F32),
            jax.ShapeDtypeStruct((nseq_total, HEADS, HEAD_DIM, HEAD_DIM), F32),
        ],
        scratch_shapes=[
            pltpu.VMEM((hist + steps, nseq, POOL_WIDTH), F32),
            *_level_scratch(rows, levels),
        ],
        compiler_params=_cparams("mixer_sample", ("arbitrary",)),
        name="mixer_sample",
    )(z, cache, state, lbp, wpool, pscale, hgn)


def _out_proj_body(x_ref, mix_ref, w_ref, o_ref, wb_ref):
    w = w_ref[...].astype(BF16)
    wb_ref[...] = w
    o_ref[...] = x_ref[...] + jnp.dot(mix_ref[...], w, preferred_element_type=F32)


def _out_proj(x, mix, w, tm, tn):
    m = x.shape[0]
    return pl.pallas_call(
        _out_proj_body,
        grid=(m // tm, D_MODEL // tn),
        in_specs=[
            pl.BlockSpec((tm, tn), lambda i, j: (i, j)),
            pl.BlockSpec((tm, D_MODEL), lambda i, j: (i, 0)),
            pl.BlockSpec((D_MODEL, tn), lambda i, j: (0, j)),
        ],
        out_specs=[pl.BlockSpec((tm, tn), lambda i, j: (i, j)), pl.BlockSpec((D_MODEL, tn), lambda i, j: (0, j))],
        out_shape=[jax.ShapeDtypeStruct((m, D_MODEL), F32), jax.ShapeDtypeStruct(w.shape, BF16)],
        compiler_params=_cparams("out_proj", ("arbitrary", "arbitrary")),
        name="out_proj",
    )(x, mix, w)


def _ffn_body(x_ref, g2_ref, wg_ref, wu_ref, wd_ref, gf_ref, y_ref, *rest):
    *wb_refs, h_ref = rest
    j = pl.program_id(1)
    last = pl.num_programs(1) - 1
    chunks = _row_chunks(x_ref.shape[0])

    def down(h, wg, wu, wd):
        gate = jnp.dot(h, wg, preferred_element_type=F32)
        up = jnp.dot(h, wu, preferred_element_type=F32)
        act = (gate * _sigmoid(gate) * up).astype(BF16)
        return jnp.dot(act, wd, preferred_element_type=F32)

    def weights():
        if not wb_refs:
            return wg_ref[...], wu_ref[...], wd_ref[...]
        w = tuple(ref[...].astype(BF16) for ref in (wg_ref, wu_ref, wd_ref))
        for wb_ref, wb in zip(wb_refs, w):
            wb_ref[...] = wb
        return w

    @pl.when(j == 0)
    def _():
        w = weights()
        for rs in chunks:
            h = _rms(x_ref[rs, :], g2_ref[...]).astype(BF16)
            h_ref[rs, :] = h
            y_ref[rs, :] = down(h, *w)

    @pl.when((j > 0) & (j < last))
    def _():
        y_ref[...] += down(h_ref[...], *weights())

    @pl.when(j == last)
    def _():
        w = weights()
        for rs in chunks:
            y_ref[rs, :] = _rms(x_ref[rs, :] + y_ref[rs, :] + down(h_ref[rs, :], *w), gf_ref[...])


def _ffn(x, g2, wg, wu, wd, gf, tm, tf):
    m = x.shape[0]
    d_ff = wg.shape[1]
    emit = wg.dtype != BF16
    w_specs = [
        pl.BlockSpec((D_MODEL, tf), lambda i, j: (0, j)),
        pl.BlockSpec((D_MODEL, tf), lambda i, j: (0, j)),
        pl.BlockSpec((tf, D_MODEL), lambda i, j: (j, 0)),
    ]
    x_mode = dict(pipeline_mode=pl.Buffered(1)) if m == tm else {}
    return pl.pallas_call(
        _ffn_body,
        grid=(m // tm, d_ff // tf),
        in_specs=[
            pl.BlockSpec((tm, D_MODEL), lambda i, j: (i, 0), **x_mode),
            pl.BlockSpec((1, D_MODEL), lambda i, j: (0, 0)),
            *w_specs,
            pl.BlockSpec((1, D_MODEL), lambda i, j: (0, 0)),
        ],
        out_specs=[pl.BlockSpec((tm, D_MODEL), lambda i, j: (i, 0))] + (w_specs if emit else []),
        out_shape=[jax.ShapeDtypeStruct((m, D_MODEL), F32)]
        + ([jax.ShapeDtypeStruct(a.shape, BF16) for a in (wg, wu, wd)] if emit else []),
        scratch_shapes=[pltpu.VMEM((tm, D_MODEL), BF16)],
        compiler_params=_cparams("ffn", ("arbitrary", "arbitrary")),
        name="ffn",
    )(x, g2, wg, wu, wd, gf)


def kernel(x_prompt, x_sample, cache_pool, state_hgrn, lb_param, norm1, w_in, w_pool, pool_scale,
           hg_norm, w_o, norm2, w_gate, w_up, w_down, norm_f):
    depth = w_in.shape[0]
    assert depth == 1, "single-layer trunk"
    layer = 0
    batch, seq, _ = x_prompt.shape
    dec_batch, dec_seq, _ = x_sample.shape

    row = lambda a: a.reshape(1, -1).astype(F32)
    g1, g2, gf = row(norm1[layer]), row(norm2[layer]), row(norm_f)
    lbp = lb_param.astype(F32)
    wpool = w_pool[layer].astype(BF16)
    pscale = row(pool_scale[layer])
    hgn = row(hg_norm[layer])

    m_s = dec_batch * dec_seq
    xs = x_sample.reshape(m_s, D_MODEL)
    z_s, w_in_b = _in_proj(xs, g1, w_in[layer], tm=m_s, tn=SAMPLE_PROJ_COLS)
    mix_s, pool_s, s_s = _mixer_sample(z_s, jnp.swapaxes(cache_pool, 1, 2), state_hgrn[layer], lbp, wpool, pscale,
                                       hgn, dec_batch, dec_seq, PAST_LEN, layer, nseq=SAMPLE_SEQS)
    pool_s = jnp.swapaxes(pool_s, 0, 1)
    x1_s, w_o_b = _out_proj(xs, mix_s, w_o[layer], tm=m_s, tn=SAMPLE_PROJ_COLS)
    y_s, w_gate_b, w_up_b, w_down_b = _ffn(x1_s, g2, w_gate[layer], w_up[layer], w_down[layer], gf,
                                           tm=m_s, tf=SAMPLE_FFN_COLS)

    x1_p, pool_p, s_p = _prompt_layer(x_prompt.reshape(batch * seq, D_MODEL), g1, w_in_b, w_o_b,
                                      lbp, wpool, pscale, hgn, batch, seq, layer)
    (y_p,) = _ffn(x1_p, g2, w_gate_b, w_up_b, w_down_b, gf, tm=PROMPT_FFN_ROWS, tf=PROMPT_FFN_COLS)

    return (y_p.reshape(batch, seq, D_MODEL), y_s.reshape(dec_batch, dec_seq, D_MODEL),
            pool_p[None], s_p[None], pool_s[None], s_s[None])
```

```python
import functools

import jax
import jax.numpy as jnp
from jax import lax
from jax.experimental import pallas as pl
from jax.experimental.pallas import tpu as pltpu

D_MODEL = 2048
POOL_WIDTH = 1024
POOL_WINDOWS = (2, 4, 8, 16)
POOL_GROUP_DIM = POOL_WIDTH // len(POOL_WINDOWS)
POOL_CACHE = max(POOL_WINDOWS) - 1
HG_WIDTH = 1024
HEAD_DIM = 128
HEADS = HG_WIDTH // HEAD_DIM
IN_COLS = POOL_WIDTH + 4 * HG_WIDTH
Z_Q, Z_F, Z_I, Z_G = (POOL_WIDTH + n * HG_WIDTH for n in range(4))
EPS = 1e-6
SUBLANES = 8
assert HEADS == SUBLANES
HIST_ROWS = 2 * SUBLANES
PAST_LEN = 16384

F32 = jnp.float32
BF16 = jnp.bfloat16

MIB = 1024 * 1024
SAMPLE_PROJ_COLS = 512
SAMPLE_FFN_COLS = 256
SAMPLE_SEQS = 8
PROMPT_FFN_ROWS = 1024
PROMPT_FFN_COLS = 512
VMEM_MIB = dict(in_proj=48, out_proj=48, mixer_sample=48, ffn=58, prompt_layer=58)


def _cparams(name, sem):
    return pltpu.CompilerParams(dimension_semantics=sem, vmem_limit_bytes=VMEM_MIB[name] * MIB)


def _rms(x, g):
    return x * lax.rsqrt(jnp.mean(x * x, axis=-1, keepdims=True) + EPS) * g


def _sigmoid(x):
    return 1.0 / (1.0 + jnp.exp(-x))


ROW_CHUNK = 256


def _row_chunks(n):
    return [slice(r, r + ROW_CHUNK) for r in range(0, n, ROW_CHUNK)]


def _in_proj_body(x_ref, g_ref, w_ref, z_ref, wb_ref, h_ref):
    j = pl.program_id(1)
    w = w_ref[...].astype(BF16)
    wb_ref[...] = w

    @pl.when(j == 0)
    def _():
        for rs in _row_chunks(x_ref.shape[0]):
            h = _rms(x_ref[rs, :], g_ref[...]).astype(BF16)
            h_ref[rs, :] = h
            z_ref[rs, :] = jnp.dot(h, w, preferred_element_type=F32)

    @pl.when(j > 0)
    def _():
        z_ref[...] = jnp.dot(h_ref[...], w, preferred_element_type=F32)


def _in_proj(x, g, w, tm, tn):
    m = x.shape[0]
    return pl.pallas_call(
        _in_proj_body,
        grid=(m // tm, IN_COLS // tn),
        in_specs=[
            pl.BlockSpec((tm, D_MODEL), lambda i, j: (i, 0)),
            pl.BlockSpec((1, D_MODEL), lambda i, j: (0, 0)),
            pl.BlockSpec((D_MODEL, tn), lambda i, j: (0, j)),
        ],
        out_specs=[pl.BlockSpec((tm, tn), lambda i, j: (i, j)), pl.BlockSpec((D_MODEL, tn), lambda i, j: (0, j))],
        out_shape=[jax.ShapeDtypeStruct((m, IN_COLS), F32), jax.ShapeDtypeStruct(w.shape, BF16)],
        scratch_shapes=[pltpu.VMEM((tm, D_MODEL), BF16)],
        compiler_params=_cparams("in_proj", ("arbitrary", "arbitrary")),
        name="in_proj",
    )(x, g, w)


def _lower_bound(lbp, layer):
    e = jnp.exp(lbp - jnp.max(lbp, axis=0, keepdims=True))
    return jnp.sum(e[: layer + 1], axis=0, keepdims=True) / jnp.sum(e, axis=0, keepdims=True)


def _pool_project(acc, cnt, u, gi, wpool_ref, pscale_ref, mix_ref):
    c0 = gi * POOL_GROUP_DIM
    pooled = acc / cnt - u
    out = jnp.dot(pooled.astype(BF16), wpool_ref[gi], preferred_element_type=F32)
    mix_ref[:, c0:c0 + POOL_GROUP_DIM] = (out * pscale_ref[:, c0:c0 + POOL_GROUP_DIM]).astype(BF16)


def _token_tile(nat_ref, a, t):
    return nat_ref[a, t // SUBLANES, pl.ds(t % SUBLANES, HEADS, stride=SUBLANES), :]


def _head_rows(lev_ref, lvl, j, rows):
    return lev_ref[lvl, pl.ds(j, rows, stride=HEADS), :]


def _head_gates(zq, zf, lb, c, rows, nat_ref, qk_ref):
    fg = lb + (1.0 - lb) * _sigmoid(zf)
    qs = zq * _sigmoid(zq)
    kk = 1.0 - fg
    qk_ref[0, :, c * HEAD_DIM:(c + 1) * HEAD_DIM] = qs.astype(BF16)
    qk_ref[1, :, c * HEAD_DIM:(c + 1) * HEAD_DIM] = kk.astype(BF16)
    for a, x in enumerate((qs, kk, fg)):
        nat_ref[a, :, c * SUBLANES:(c + 1) * SUBLANES, :] = x.reshape(rows // SUBLANES, SUBLANES, HEAD_DIM)


def _segment_offsets(rows, levels):
    off = [0]
    for lvl in range(levels):
        off.append(off[-1] + (rows >> lvl))
    return off


def _segment_products(rows, levels, nat_ref, seg_ref):
    off = _segment_offsets(rows, levels)
    for t in range(rows):
        seg_ref[t] = _token_tile(nat_ref, 2, t)
    for lvl in range(1, levels):
        for m in range(rows >> lvl):
            seg_ref[off[lvl] + m] = seg_ref[off[lvl - 1] + 2 * m] * seg_ref[off[lvl - 1] + 2 * m + 1]


def _token_levels(tokens, rows, levels, nat_ref, seg_ref, lev_ref):
    off = _segment_offsets(rows, levels)
    prefix = []
    for t in tokens:
        tok = slice(t * HEADS, (t + 1) * HEADS)
        q_t, k_t = _token_tile(nat_ref, 0, t), _token_tile(nat_ref, 1, t)
        p_t, r_t = seg_ref[t], None
        for lvl in range(levels):
            m = t >> lvl
            sibling = seg_ref[off[lvl] + (m ^ 1)]
            if m & 1:
                lev_ref[lvl, tok, :] = q_t * p_t
                p_t = p_t * sibling
            else:
                lev_ref[lvl, tok, :] = k_t if r_t is None else k_t * r_t
                r_t = sibling if r_t is None else r_t * sibling
        lev_ref[levels, tok, :] = q_t * p_t
        lev_ref[levels + 1, tok, :] = k_t if r_t is None else k_t * r_t
        prefix.append(p_t)
    return prefix


def _level_scratch(rows, levels):
    return [
        pltpu.VMEM((3, rows // SUBLANES, HEADS * SUBLANES, HEAD_DIM), F32),
        pltpu.VMEM((2 * rows, HEADS, HEAD_DIM), F32),
        pltpu.VMEM((2, rows, HG_WIDTH), BF16),
        pltpu.VMEM((levels + 2, rows * HEADS, HEAD_DIM), F32),
        pltpu.VMEM((HEADS, rows, rows), F32),
    ]


def _level_masks(rows, levels):
    ti = lax.broadcasted_iota(jnp.int32, (rows, rows), 0)
    si = lax.broadcasted_iota(jnp.int32, (rows, rows), 1)
    x = ti ^ si
    masks = []
    for lvl in range(levels):
        h = 1 << lvl
        if h < SUBLANES:
            masks.append(((x >> lvl) == 1) & (((ti >> lvl) & 1) == 1))
        else:
            half = lax.broadcasted_iota(jnp.int32, (h, rows), 1) >> lvl
            masks.append([half == 2 * b for b in range(rows // (2 * h))])
    return masks, ti == si


def _nt(a, b):
    return lax.dot_general(a, b, (((1,), (1,)), ((), ())), preferred_element_type=F32)


def _score_level(j, lvl, rows, masks, eye, qk_ref, lev_ref, sc_ref):
    c0 = j * HEAD_DIM
    sc_ref = sc_ref.at[j]
    if lvl < 0:
        pltpu.store(sc_ref, _nt(qk_ref[0, :, c0:c0 + HEAD_DIM], qk_ref[1, :, c0:c0 + HEAD_DIM]), mask=eye)
        return
    h = 1 << lvl
    x32 = _head_rows(lev_ref, lvl, j, rows)
    x = x32.astype(BF16)
    if h < SUBLANES:
        pltpu.store(sc_ref, _nt(x, x), mask=masks[lvl])
        return
    blocks = rows // (2 * h)
    upper = x32.reshape(blocks, 2 * h, HEAD_DIM)[:, h:].reshape(rows // 2, HEAD_DIM).astype(BF16)
    s = _nt(upper, x)
    for b in range(blocks):
        pltpu.store(sc_ref.at[b * 2 * h + h:(b + 1) * 2 * h, :], s[b * h:(b + 1) * h], mask=masks[lvl][b])


def _intra_scores(j, rows, levels, masks, eye, qk_ref, lev_ref, sc_ref):
    for lvl in range(-1, levels):
        _score_level(j, lvl, rows, masks, eye, qk_ref, lev_ref, sc_ref)
    return sc_ref[j]


def _window_sums(e, w, axis, hist, rows):
    s, d = e, 1
    while d < min(w, SUBLANES):
        s = s + pltpu.roll(s, d, axis)
        d *= 2
    take = lambda a, lo: lax.slice_in_dim(a, lo, lo + rows, axis=axis)
    out = take(s, hist)
    if w > SUBLANES:
        assert w == 2 * SUBLANES
        out = out + take(s, hist - SUBLANES)
    return out


def _head_output(o, j, zg, hgn_ref, mix_ref):
    c0 = j * HEAD_DIM
    o = _rms(o, hgn_ref[:, c0:c0 + HEAD_DIM]) * (zg * _sigmoid(zg))
    mix_ref[:, POOL_WIDTH + c0:POOL_WIDTH + c0 + HEAD_DIM] = o.astype(BF16)


PROMPT_ROWS = 128
PROMPT_LEVELS = 7


PROMPT_SLOTS = 2


def _prompt_prepare(z_blk, t_blk, starts_seq, slot, lb, wpool_ref, pscale_ref,
                    ext_ref, pool_ref, decay_ref, nat_ref, seg_ref, qk_ref, lev_ref):
    rows, levels, hist = PROMPT_ROWS, PROMPT_LEVELS, HIST_ROWS
    nat, seg, qk, lev = nat_ref[slot], seg_ref[slot], qk_ref[slot], lev_ref[slot]

    def pool_group(gi):
        def run():
            if gi == 0:
                ext_ref[0:hist, :] = jnp.where(starts_seq, 0.0, ext_ref[rows:rows + hist, :])
                ext_ref[hist:hist + rows, :] = z_blk[:, 0:POOL_WIDTH]
            w, c0 = POOL_WINDOWS[gi], gi * POOL_GROUP_DIM
            e = ext_ref[:, c0:c0 + POOL_GROUP_DIM]
            pos = t_blk * rows + lax.broadcasted_iota(jnp.int32, (rows, 1), 0)
            cnt = jnp.minimum(pos + 1, w).astype(F32)
            _pool_project(_window_sums(e, w, 0, hist, rows), cnt, e[hist:hist + rows], gi,
                          wpool_ref, pscale_ref, pool_ref[slot])
        return run

    def gates(c):
        cols = slice(c * HEAD_DIM, (c + 1) * HEAD_DIM)
        return lambda: _head_gates(z_blk[:, Z_Q + c * HEAD_DIM:Z_Q + (c + 1) * HEAD_DIM],
                                   z_blk[:, Z_F + c * HEAD_DIM:Z_F + (c + 1) * HEAD_DIM], lb[:, cols], c, rows, nat, qk)

    def tokens(k, n):
        def run():
            prefix = _token_levels(range(k * n, (k + 1) * n), rows, levels, nat, seg, lev)
            if (k + 1) * n == rows:
                decay_ref[slot][...] = prefix[-1]
        return run

    a, b = [pool_group(g) for g in range(len(POOL_WINDOWS))], [gates(c) for c in range(HEADS)]
    t = [tokens(k, rows // HEADS) for k in range(HEADS)]
    segments = lambda: _segment_products(rows, levels, nat, seg)
    return [[a[0], a[1], b[0]], [a[2], a[3], b[1]], b[2:5], b[5:8], [segments, t[0]], t[1:3], t[3:6], t[6:8]]


def _prompt_heads(slot, z_blk, mix_blk, hgn_ref, s_ref, pool_ref, decay_ref, qk_ref, lev_ref, sc_ref,
                  masks, eye):
    rows, levels = PROMPT_ROWS, PROMPT_LEVELS
    qk, lev, sc = qk_ref[slot], lev_ref[slot], sc_ref[slot]
    value = lambda j: z_blk[:, Z_I + j * HEAD_DIM:Z_I + (j + 1) * HEAD_DIM].astype(BF16)

    def pool_part():
        mix_blk[:, 0:POOL_WIDTH] = pool_ref[slot][...]

    def output(j):
        c0 = j * HEAD_DIM
        qt = _head_rows(lev, levels, j, rows).astype(BF16)
        o = (jnp.dot(sc[j].astype(BF16), value(j), preferred_element_type=F32)
             + jnp.dot(qt, s_ref[j].astype(BF16), preferred_element_type=F32))
        _head_output(o, j, z_blk[:, Z_G + c0:Z_G + c0 + HEAD_DIM], hgn_ref, mix_blk)

    def state(j):
        decay_t = jnp.transpose(decay_ref[slot][...])
        kt = _head_rows(lev, levels + 1, j, rows).astype(BF16)
        s_ref[j] = decay_t[:, j:j + 1] * s_ref[j] + lax.dot_general(
            kt, value(j), (((0,), (0,)), ((), ())), preferred_element_type=F32)

    heads = range(HEADS)
    scores = [functools.partial(_score_level, j, lvl, rows, masks, eye, qk, lev, sc)
              for lvl in range(-1, levels) for j in heads]
    return ([pool_part] + scores + [functools.partial(output, j) for j in heads]
            + [functools.partial(state, j) for j in heads])


def _prompt_layer_body(xc_ref, g1_ref, win_ref, wo_ref, lbp_ref, wpool_ref, pscale_ref, hgn_ref,
                       x1_ref, pooln_ref, snew_ref,
                       z_ref, mix_ref, ext_ref, s_ref, pool_ref, decay_ref, nat_ref, seg_ref, qk_ref,
                       lev_ref, sc_ref, *, layer, nt):
    rows, hist = PROMPT_ROWS, HIST_ROWS
    i = pl.program_id(0)
    steps_per_seq = nt // PROMPT_SLOTS
    tt = i % steps_per_seq
    lb = _lower_bound(lbp_ref[...], layer)
    one = lambda ref: [ref]
    prepare = functools.partial(
        _prompt_prepare, lb=lb, wpool_ref=wpool_ref, pscale_ref=pscale_ref, ext_ref=ext_ref,
        pool_ref=one(pool_ref), decay_ref=one(decay_ref), nat_ref=one(nat_ref), seg_ref=one(seg_ref),
        qk_ref=one(qk_ref), lev_ref=one(lev_ref))
    masks, eye = _level_masks(rows, PROMPT_LEVELS)
    heads = functools.partial(
        _prompt_heads, hgn_ref=hgn_ref, s_ref=s_ref, pool_ref=one(pool_ref), decay_ref=one(decay_ref),
        qk_ref=one(qk_ref), lev_ref=one(lev_ref), sc_ref=one(sc_ref), masks=masks, eye=eye)

    def in_proj(x_ref):
        h = _rms(x_ref[...], g1_ref[...]).astype(BF16)
        z_ref[...] = jnp.dot(h, win_ref[...], preferred_element_type=F32)

    def mix_block(blk, t_blk, starts_seq):
        z_blk, mix_blk = z_ref.at[blk * rows:(blk + 1) * rows], mix_ref.at[blk * rows:(blk + 1) * rows]
        for batch in prepare(z_blk, t_blk, starts_seq, 0):
            for item in batch:
                item()
        for head in heads(0, z_blk, mix_blk):
            head()

    def out_proj():
        x1_ref[...] = xc_ref[...] + jnp.dot(mix_ref[...], wo_ref[...], preferred_element_type=F32)

    @pl.when(i == 0)
    def _():
        sc_ref[...] = jnp.zeros_like(sc_ref)
        ext_ref[...] = jnp.zeros_like(ext_ref)

    @pl.when(tt == 0)
    def _():
        s_ref[...] = jnp.zeros_like(s_ref)

    region = pl.when(i < pl.num_programs(0))
    region(functools.partial(in_proj, xc_ref))
    region(functools.partial(mix_block, 0, PROMPT_SLOTS * tt, tt == 0))
    region(functools.partial(mix_block, 1, PROMPT_SLOTS * tt + 1, False))
    region(out_proj)
    pooln_ref[0] = ext_ref[hist + rows - POOL_CACHE:hist + rows, :]

    @pl.when(tt == steps_per_seq - 1)
    def _():
        snew_ref[0] = s_ref[...]


def _prompt_layer(x, g1, w_in, w_o, lbp, wpool, pscale, hgn, batch, seq, layer):
    rows, slots = PROMPT_ROWS, PROMPT_SLOTS
    nt = seq // rows
    assert seq % (rows * slots) == 0
    nblk = batch * nt
    steps_per_seq = nt // slots
    const2 = lambda i: (0, 0)
    resident = lambda shape: pl.BlockSpec(shape, const2, pipeline_mode=pl.Buffered(1))
    return pl.pallas_call(
        functools.partial(_prompt_layer_body, layer=layer, nt=nt),
        grid=(nblk // slots,),
        in_specs=[
            pl.BlockSpec((slots * rows, D_MODEL), lambda i: (i, 0)),
            pl.BlockSpec((1, D_MODEL), const2),
            resident(w_in.shape),
            resident(w_o.shape),
            pl.BlockSpec(lbp.shape, const2),
            pl.BlockSpec(wpool.shape, lambda i: (0, 0, 0)),
            pl.BlockSpec((1, POOL_WIDTH), const2),
            pl.BlockSpec((1, HG_WIDTH), const2),
        ],
        out_specs=[
            pl.BlockSpec((slots * rows, D_MODEL), lambda i: (i, 0)),
            pl.BlockSpec((1, POOL_CACHE, POOL_WIDTH), lambda i: (i // steps_per_seq, 0, 0)),
            pl.BlockSpec((1, HEADS, HEAD_DIM, HEAD_DIM), lambda i: (i // steps_per_seq, 0, 0, 0)),
        ],
        out_shape=[
            jax.ShapeDtypeStruct((batch * seq, D_MODEL), F32),
            jax.ShapeDtypeStruct((batch, POOL_CACHE, POOL_WIDTH), F32),
            jax.ShapeDtypeStruct((batch, HEADS, HEAD_DIM, HEAD_DIM), F32),
        ],
        scratch_shapes=[
            pltpu.VMEM((slots * rows, IN_COLS), F32),
            pltpu.VMEM((slots * rows, D_MODEL), BF16),
            pltpu.VMEM((HIST_ROWS + rows, POOL_WIDTH), F32),
            pltpu.VMEM((HEADS, HEAD_DIM, HEAD_DIM), F32),
            pltpu.VMEM((rows, POOL_WIDTH), BF16),
            pltpu.VMEM((HEADS, HEAD_DIM), F32),
            *_level_scratch(rows, PROMPT_LEVELS),
        ],
        compiler_params=_cparams("prompt_layer", ("arbitrary",)),
        name="prompt_layer",
    )(x, g1, w_in, w_o, lbp, wpool, pscale, hgn)


def _mixer_sample_body(z_ref, cache_ref, sin_ref, lbp_ref, wpool_ref, pscale_ref, hgn_ref,
                       mix_ref, pooln_ref, snew_ref,
                       ext_ref, nat_ref, seg_ref, qk_ref, lev_ref, sc_ref, *, layer, nseq, steps, pos0):
    rows = nseq * steps
    levels = steps.bit_length() - 1
    hist = HIST_ROWS

    @pl.when(pl.program_id(0) == 0)
    def _():
        sc_ref[...] = jnp.zeros_like(sc_ref)

    u_nat = z_ref[:, 0:POOL_WIDTH]
    ext_ref[0:hist - POOL_CACHE] = jnp.zeros((hist - POOL_CACHE, nseq, POOL_WIDTH), F32)
    ext_ref[hist - POOL_CACHE:hist] = cache_ref[0]
    ext_ref[hist:hist + steps] = jnp.swapaxes(u_nat.reshape(nseq, steps, POOL_WIDTH), 0, 1)
    pos = pos0 + lax.broadcasted_iota(jnp.int32, (steps, 1, 1), 0)
    for gi, w in enumerate(POOL_WINDOWS):
        c0 = gi * POOL_GROUP_DIM
        acc = ext_ref[hist:hist + steps, :, c0:c0 + POOL_GROUP_DIM]
        for d in range(1, w):
            acc = acc + ext_ref[hist - d:hist - d + steps, :, c0:c0 + POOL_GROUP_DIM]
        mean = acc / jnp.minimum(pos + 1, w).astype(F32)
        mean = jnp.swapaxes(mean, 0, 1).reshape(rows, POOL_GROUP_DIM)
        _pool_project(mean, 1.0, u_nat[:, c0:c0 + POOL_GROUP_DIM], gi, wpool_ref, pscale_ref, mix_ref)
    pooln_ref[...] = ext_ref[hist + steps - POOL_CACHE:hist + steps]

    lb = _lower_bound(lbp_ref[...], layer)
    for c in range(HEADS):
        cols = slice(c * HEAD_DIM, (c + 1) * HEAD_DIM)
        _head_gates(z_ref[:, Z_Q + c * HEAD_DIM:Z_Q + (c + 1) * HEAD_DIM],
                    z_ref[:, Z_F + c * HEAD_DIM:Z_F + (c + 1) * HEAD_DIM], lb[:, cols], c, rows, nat_ref, qk_ref)
    _segment_products(rows, levels, nat_ref, seg_ref)
    prefix = _token_levels(range(rows), rows, levels, nat_ref, seg_ref, lev_ref)
    decay = jnp.stack(prefix[steps - 1::steps])
    decay_t = jnp.swapaxes(decay, 1, 2)
    masks, eye = _level_masks(rows, levels)
    for j in range(HEADS):
        c0 = j * HEAD_DIM
        scores = _intra_scores(j, rows, levels, masks, eye, qk_ref, lev_ref, sc_ref)
        v32 = z_ref[:, Z_I + c0:Z_I + c0 + HEAD_DIM]
        v = v32.astype(BF16)
        v3 = v32.reshape(nseq, steps, HEAD_DIM).astype(BF16)
        qt3 = _head_rows(lev_ref, levels, j, rows).reshape(nseq, steps, HEAD_DIM).astype(BF16)
        kt3 = _head_rows(lev_ref, levels + 1, j, rows).reshape(nseq, steps, HEAD_DIM).astype(BF16)
        s_in = sin_ref[:, j]
        o_state = lax.dot_general(qt3, s_in.astype(BF16), (((2,), (1,)), ((0,), (0,))),
                                  preferred_element_type=F32)
        o = jnp.dot(scores.astype(BF16), v, preferred_element_type=F32) + o_state.reshape(rows, HEAD_DIM)
        upd = lax.dot_general(kt3, v3, (((1,), (1,)), ((0,), (0,))), preferred_element_type=F32)
        snew_ref[:, j] = decay_t[:, :, j:j + 1] * s_in + upd
        _head_output(o, j, z_ref[:, Z_G + c0:Z_G + c0 + HEAD_DIM], hgn_ref, mix_ref)


def _mixer_sample(z, cache, state, lbp, wpool, pscale, hgn, nseq_total, steps, pos0, layer, nseq):
    rows = nseq * steps
    levels = steps.bit_length() - 1
    assert 1 << levels == steps and nseq_total % nseq == 0
    hist = HIST_ROWS
    const2 = lambda i: (0, 0)
    return pl.pallas_call(
        functools.partial(_mixer_sample_body, layer=layer, nseq=nseq, steps=steps, pos0=pos0),
        grid=(nseq_total // nseq,),
        in_specs=[
            pl.BlockSpec((rows, IN_COLS), lambda i: (i, 0)),
            pl.BlockSpec((1, POOL_CACHE, nseq, POOL_WIDTH), lambda i: (layer, 0, i, 0)),
            pl.BlockSpec((nseq, HEADS, HEAD_DIM, HEAD_DIM), lambda i: (i, 0, 0, 0)),
            pl.BlockSpec(lbp.shape, const2),
            pl.BlockSpec(wpool.shape, lambda i: (0, 0, 0)),
            pl.BlockSpec((1, POOL_WIDTH), const2),
            pl.BlockSpec((1, HG_WIDTH), const2),
        ],
        out_specs=[
            pl.BlockSpec((rows, D_MODEL), lambda i: (i, 0)),
            pl.BlockSpec((POOL_CACHE, nseq, POOL_WIDTH), lambda i: (0, i, 0)),
            pl.BlockSpec((nseq, HEADS, HEAD_DIM, HEAD_DIM), lambda i: (i, 0, 0, 0)),
        ],
        out_shape=[
            jax.ShapeDtypeStruct((nseq_total * steps, D_MODEL), BF16),
            jax.ShapeDtypeStruct((POOL_CACHE, nseq_total, POOL_WIDTH), F32),
            jax.ShapeDtypeStruct((nseq_total, HEADS, HEAD_DIM, HEAD_DIM), F32),
        ],
        scratch_shapes=[
            pltpu.VMEM((hist + steps, nseq, POOL_WIDTH), F32),
            *_level_scratch(rows, levels),
        ],
        compiler_params=_cparams("mixer_sample", ("arbitrary",)),
        name="mixer_sample",
    )(z, cache, state, lbp, wpool, pscale, hgn)


def _out_proj_body(x_ref, mix_ref, w_ref, o_ref, wb_ref):
    w = w_ref[...].astype(BF16)
    wb_ref[...] = w
    o_ref[...] = x_ref[...] + jnp.dot(mix_ref[...], w, preferred_element_type=F32)


def _out_proj(x, mix, w, tm, tn):
    m = x.shape[0]
    return pl.pallas_call(
        _out_proj_body,
        grid=(m // tm, D_MODEL // tn),
        in_specs=[
            pl.BlockSpec((tm, tn), lambda i, j: (i, j)),
            pl.BlockSpec((tm, D_MODEL), lambda i, j: (i, 0)),
            pl.BlockSpec((D_MODEL, tn), lambda i, j: (0, j)),
        ],
        out_specs=[pl.BlockSpec((tm, tn), lambda i, j: (i, j)), pl.BlockSpec((D_MODEL, tn), lambda i, j: (0, j))],
        out_shape=[jax.ShapeDtypeStruct((m, D_MODEL), F32), jax.ShapeDtypeStruct(w.shape, BF16)],
        compiler_params=_cparams("out_proj", ("arbitrary", "arbitrary")),
        name="out_proj",
    )(x, mix, w)


def _ffn_body(x_ref, g2_ref, wg_ref, wu_ref, wd_ref, gf_ref, y_ref, *rest):
    *wb_refs, h_ref = rest
    j = pl.program_id(1)
    last = pl.num_programs(1) - 1
    chunks = _row_chunks(x_ref.shape[0])

    def down(h, wg, wu, wd):
        gate = jnp.dot(h, wg, preferred_element_type=F32)
        up = jnp.dot(h, wu, preferred_element_type=F32)
        act = (gate * _sigmoid(gate) * up).astype(BF16)
        return jnp.dot(act, wd, preferred_element_type=F32)

    def weights():
        if not wb_refs:
            return wg_ref[...], wu_ref[...], wd_ref[...]
        w = tuple(ref[...].astype(BF16) for ref in (wg_ref, wu_ref, wd_ref))
        for wb_ref, wb in zip(wb_refs, w):
            wb_ref[...] = wb
        return w

    @pl.when(j == 0)
    def _():
        w = weights()
        for rs in chunks:
            h = _rms(x_ref[rs, :], g2_ref[...]).astype(BF16)
            h_ref[rs, :] = h
            y_ref[rs, :] = down(h, *w)

    @pl.when((j > 0) & (j < last))
    def _():
        y_ref[...] += down(h_ref[...], *weights())

    @pl.when(j == last)
    def _():
        w = weights()
        for rs in chunks:
            y_ref[rs, :] = _rms(x_ref[rs, :] + y_ref[rs, :] + down(h_ref[rs, :], *w), gf_ref[...])


def _ffn(x, g2, wg, wu, wd, gf, tm, tf):
    m = x.shape[0]
    d_ff = wg.shape[1]
    emit = wg.dtype != BF16
    w_specs = [
        pl.BlockSpec((D_MODEL, tf), lambda i, j: (0, j)),
        pl.BlockSpec((D_MODEL, tf), lambda i, j: (0, j)),
        pl.BlockSpec((tf, D_MODEL), lambda i, j: (j, 0)),
    ]
    x_mode = dict(pipeline_mode=pl.Buffered(1)) if m == tm else {}
    return pl.pallas_call(
        _ffn_body,
        grid=(m // tm, d_ff // tf),
        in_specs=[
            pl.BlockSpec((tm, D_MODEL), lambda i, j: (i, 0), **x_mode),
            pl.BlockSpec((1, D_MODEL), lambda i, j: (0, 0)),
            *w_specs,
            pl.BlockSpec((1, D_MODEL), lambda i, j: (0, 0)),
        ],
        out_specs=[pl.BlockSpec((tm, D_MODEL), lambda i, j: (i, 0))] + (w_specs if emit else []),
        out_shape=[jax.ShapeDtypeStruct((m, D_MODEL), F32)]
        + ([jax.ShapeDtypeStruct(a.shape, BF16) for a in (wg, wu, wd)] if emit else []),
        scratch_shapes=[pltpu.VMEM((tm, D_MODEL), BF16)],
        compiler_params=_cparams("ffn", ("arbitrary", "arbitrary")),
        name="ffn",
    )(x, g2, wg, wu, wd, gf)


def kernel(x_prompt, x_sample, cache_pool, state_hgrn, lb_param, norm1, w_in, w_pool, pool_scale,
           hg_norm, w_o, norm2, w_gate, w_up, w_down, norm_f):
    depth = w_in.shape[0]
    assert depth == 1, "single-layer trunk"
    layer = 0
    batch, seq, _ = x_prompt.shape
    dec_batch, dec_seq, _ = x_sample.shape

    row = lambda a: a.reshape(1, -1).astype(F32)
    g1, g2, gf = row(norm1[layer]), row(norm2[layer]), row(norm_f)
    lbp = lb_param.astype(F32)
    wpool = w_pool[layer].astype(BF16)
    pscale = row(pool_scale[layer])
    hgn = row(hg_norm[layer])

    m_s = dec_batch * dec_seq
    xs = x_sample.reshape(m_s, D_MODEL)
    z_s, w_in_b = _in_proj(xs, g1, w_in[layer], tm=m_s, tn=SAMPLE_PROJ_COLS)
    mix_s, pool_s, s_s = _mixer_sample(z_s, jnp.swapaxes(cache_pool, 1, 2), state_hgrn[layer], lbp, wpool, pscale,
                                       hgn, dec_batch, dec_seq, PAST_LEN, layer, nseq=SAMPLE_SEQS)
    pool_s = jnp.swapaxes(pool_s, 0, 1)
    x1_s, w_o_b = _out_proj(xs, mix_s, w_o[layer], tm=m_s, tn=SAMPLE_PROJ_COLS)
    y_s, w_gate_b, w_up_b, w_down_b = _ffn(x1_s, g2, w_gate[layer], w_up[layer], w_down[layer], gf,
                                           tm=m_s, tf=SAMPLE_FFN_COLS)

    x1_p, pool_p, s_p = _prompt_layer(x_prompt.reshape(batch * seq, D_MODEL), g1, w_in_b, w_o_b,
                                      lbp, wpool, pscale, hgn, batch, seq, layer)
    (y_p,) = _ffn(x1_p, g2, w_gate_b, w_up_b, w_down_b, gf, tm=PROMPT_FFN_ROWS, tf=PROMPT_FFN_COLS)

    return (y_p.reshape(batch, seq, D_MODEL), y_s.reshape(dec_batch, dec_seq, D_MODEL),
            pool_p[None], s_p[None], pool_s[None], s_s[None])
```
